```python
import jax, jax.numpy as jnp
from jax import lax
import numpy as np

D_MODEL = 1024
BATCH = 8
SEQ = 8192
DEPTH = 1
DEC_BATCH = 32
DEC_SEQ = 64
PAST_LEN = 4096

CHUNK = 64
HEAD_DIM = 64
SB_HEADS = 8
RET_HEADS = 4
MEM_HEADS = 4
N_MEM = 256
SB_WIDTH = SB_HEADS * HEAD_DIM
RET_WIDTH = RET_HEADS * HEAD_DIM
MEM_WIDTH = MEM_HEADS * HEAD_DIM
MIX_WIDTH = SB_WIDTH + RET_WIDTH + MEM_WIDTH
IN_WIDTH = 3 * SB_WIDTH + 4 * RET_WIDTH + MEM_WIDTH
Q_BLOCK = 128
ROPE_BASE = 10000.0
PEER_HEADS = 8
PEER_KEYS = 128
PEER_TOPK = 16
PEER_QDIM = 256
PEER_HALF = PEER_QDIM // 2
N_EXPERTS = PEER_KEYS * PEER_KEYS
PEER_BLOCK = 512
PEER_V_SCALE = 0.3
EPS = 1e-6

kernel_name = 'hymba_sb_retnet_peer_stream_step'


def rmsnorm(x, g):
    xf = x.astype(jnp.float32)
    y = xf * lax.rsqrt(jnp.mean(xf * xf, axis=-1, keepdims=True) + EPS)
    return (y * g.astype(jnp.float32)).astype(x.dtype)


def rope(x, pos):
    half = HEAD_DIM // 2
    freqs = ROPE_BASE ** (-jnp.arange(half, dtype=jnp.float32) / half)
    ang = pos.astype(jnp.float32)[:, None] * freqs[None, :]
    cos = jnp.cos(ang)[None, :, None, :]
    sin = jnp.sin(ang)[None, :, None, :]
    xf = x.astype(jnp.float32)
    x1, x2 = xf[..., :half], xf[..., half:]
    return jnp.concatenate([x1 * cos - x2 * sin, x1 * sin + x2 * cos], axis=-1)


def split_proj(h, w_in):
    B, T, _ = h.shape
    p = h @ w_in
    cuts = [SB_WIDTH, 2 * SB_WIDTH, 3 * SB_WIDTH,
            3 * SB_WIDTH + RET_WIDTH, 3 * SB_WIDTH + 2 * RET_WIDTH,
            3 * SB_WIDTH + 3 * RET_WIDTH, 3 * SB_WIDTH + 4 * RET_WIDTH]
    q_sb, k_sb, v_sb, q_ret, k_ret, v_ret, gate, q_mem = jnp.split(p, cuts, axis=-1)
    sb = lambda a: a.reshape(B, T, SB_HEADS, HEAD_DIM)
    rt = lambda a: a.reshape(B, T, RET_HEADS, HEAD_DIM)
    return (sb(q_sb), sb(k_sb), sb(v_sb), rt(q_ret), rt(k_ret), rt(v_ret), gate,
            q_mem.reshape(B, T, MEM_HEADS, HEAD_DIM))


def sb_block(q, k, v, pos_q, pos_k):
    z = jnp.einsum('bqhd,bkhd->bhqk', q, k).astype(jnp.float32) * (HEAD_DIM ** -0.5)
    strict = pos_k[None, :] < pos_q[:, None]
    log_beta = jax.nn.log_sigmoid(z)
    log_keep = jnp.where(strict, jax.nn.log_sigmoid(-z), 0.0)
    between = lax.cumsum(log_keep, axis=3, reverse=True) - log_keep
    a = jnp.where(strict, jnp.exp(log_beta + between), 0.0)
    return jnp.einsum('bhqk,bkhd->bqhd', a.astype(v.dtype), v)


def stick_breaking_prompt(q, k, v):
    B, T, H, d = q.shape
    nb = T // Q_BLOCK
    pos = jnp.arange(T, dtype=jnp.int32)
    qb = q.reshape(B, nb, Q_BLOCK, H, d).transpose(1, 0, 2, 3, 4)
    pb = pos.reshape(nb, Q_BLOCK)
    out = lax.map(lambda a: sb_block(a[0], k, v, a[1], pos), (qb, pb))
    return out.transpose(1, 0, 2, 3, 4).reshape(B, T, H, d)


def ret_log_gamma():
    return jnp.log1p(-jnp.exp2(-5.0 - jnp.arange(RET_HEADS, dtype=jnp.float32)))


def ret_chunk(state, qkv):
    q, k, v = qkv
    L = q.shape[1]
    lg = ret_log_gamma()
    idx = jnp.arange(L, dtype=jnp.float32)
    diff = idx[:, None] - idx[None, :]
    causal = diff >= 0
    decay = jnp.where(causal[None], jnp.exp(jnp.where(causal, diff, 0.0)[None] * lg[:, None, None]), 0.0)
    scores = jnp.einsum('bihd,bjhd->bhij', q, k) * decay[None]
    intra = jnp.einsum('bhij,bjhe->bihe', scores, v)
    q_dec = jnp.exp((idx + 1.0)[:, None] * lg[None, :])
    cross = jnp.einsum('bihd,bhde->bihe', q, state) * q_dec[None, :, :, None]
    k_dec = jnp.exp((L - 1.0 - idx)[:, None] * lg[None, :])
    new_state = (jnp.exp(L * lg)[None, :, None, None] * state
                 + jnp.einsum('bjhd,bjhe->bhde', k * k_dec[None, :, :, None], v))
    return new_state, intra + cross


def retention_prompt(q, k, v):
    B, T, H, d = q.shape
    nc = T // CHUNK
    to_chunks = lambda a: a.reshape(B, nc, CHUNK, H, d).transpose(1, 0, 2, 3, 4)
    init = jnp.zeros((B, H, d, d), jnp.float32)
    st, out = lax.scan(ret_chunk, init, (to_chunks(q), to_chunks(k), to_chunks(v)))
    return out.transpose(1, 0, 2, 3, 4).reshape(B, T, H, d), st


def memory_kv(mem, g_mem, w_mem_kv):
    B, M, _ = mem.shape
    kv = rmsnorm(mem, g_mem) @ w_mem_kv
    mk, mv = jnp.split(kv, 2, axis=-1)
    return mk.reshape(B, M, MEM_HEADS, HEAD_DIM), mv.reshape(B, M, MEM_HEADS, HEAD_DIM)


def memory_attention(q, mk, mv):
    s = jnp.einsum('bthd,bmhd->bhtm', q, mk).astype(jnp.float32) * (HEAD_DIM ** -0.5)
    p = jax.nn.softmax(s, axis=-1)
    return jnp.einsum('bhtm,bmhd->bthd', p.astype(mv.dtype), mv)


def merge(sb_o, ret_o, gate, mem_o, g_sb, g_ret, g_mo, w_out):
    B, T = sb_o.shape[:2]
    sb_n = rmsnorm(sb_o, g_sb.reshape(SB_HEADS, HEAD_DIM)).reshape(B, T, SB_WIDTH)
    ret_n = rmsnorm(ret_o, g_ret.reshape(RET_HEADS, HEAD_DIM)).reshape(B, T, RET_WIDTH) * jax.nn.silu(gate)
    mem_n = rmsnorm(mem_o, g_mo.reshape(MEM_HEADS, HEAD_DIM)).reshape(B, T, MEM_WIDTH)
    return jnp.concatenate([sb_n, ret_n, mem_n], axis=-1) @ w_out


def peer(h, w_q, sub_keys, u_tab, v_tab):
    B, T, D = h.shape
    n = B * T
    blk = min(PEER_BLOCK, n)
    nb = -(-n // blk)
    hf = jnp.pad(h.reshape(n, D), ((0, nb * blk - n), (0, 0))).reshape(nb, blk, D)

    def block(hb):
        q = (hb @ w_q).reshape(blk, PEER_HEADS, 2, PEER_HALF)
        s = jnp.einsum('nhpc,pkc->nhpk', q, sub_keys).astype(jnp.float32)
        sv, si = lax.top_k(s, PEER_TOPK)
        cand = (sv[:, :, 0, :, None] + sv[:, :, 1, None, :]).reshape(blk, PEER_HEADS, PEER_TOPK * PEER_TOPK)
        cv, ci = lax.top_k(cand, PEER_TOPK)
        e1 = jnp.take_along_axis(si[:, :, 0, :], ci // PEER_TOPK, axis=-1)
        e2 = jnp.take_along_axis(si[:, :, 1, :], ci % PEER_TOPK, axis=-1)
        expert = e1 * PEER_KEYS + e2
        g = jax.nn.softmax(cv, axis=-1)
        act = jax.nn.gelu(jnp.einsum('nd,nhkd->nhk', hb, u_tab[expert]).astype(jnp.float32), approximate=False)
        return jnp.einsum('nhk,nhkd->nd', (g * act).astype(v_tab.dtype), v_tab[expert])

    out = lax.map(block, hf)
    return out.reshape(nb * blk, D)[:n].reshape(B, T, D)


def setup_inputs(seed: int = 0) -> dict:
    key = jax.random.key(seed)
    ks = jax.random.split(key, 22)
    nrm = lambda k, shape, s: jax.random.normal(k, shape, jnp.float32) * s
    gain = lambda k, shape: 1.0 + 0.01 * jax.random.normal(k, shape, jnp.float32)
    return {
        'x_prompt': nrm(ks[0], (BATCH, SEQ, D_MODEL), 1.0),
        'x_sample': nrm(ks[1], (DEC_BATCH, DEC_SEQ, D_MODEL), 1.0),
        'mem_prompt': nrm(ks[2], (BATCH, N_MEM, D_MODEL), 1.0),
        'cache_sb_k': nrm(ks[3], (DEPTH, DEC_BATCH, PAST_LEN, SB_HEADS, HEAD_DIM), 1.0),
        'cache_sb_v': nrm(ks[4], (DEPTH, DEC_BATCH, PAST_LEN, SB_HEADS, HEAD_DIM), 1.0),
        'state_ret': nrm(ks[5], (DEPTH, DEC_BATCH, RET_HEADS, HEAD_DIM, HEAD_DIM), 0.5),
        'cache_mem_k': nrm(ks[6], (DEPTH, DEC_BATCH, N_MEM, MEM_HEADS, HEAD_DIM), 1.0),
        'cache_mem_v': nrm(ks[7], (DEPTH, DEC_BATCH, N_MEM, MEM_HEADS, HEAD_DIM), 1.0),
        'g_mix': gain(ks[8], (DEPTH, D_MODEL)),
        'w_in': nrm(ks[9], (DEPTH, D_MODEL, IN_WIDTH), D_MODEL ** -0.5),
        'w_out': nrm(ks[10], (DEPTH, MIX_WIDTH, D_MODEL), MIX_WIDTH ** -0.5),
        'g_sb_out': gain(ks[11], (DEPTH, SB_WIDTH)),
        'g_ret_out': gain(ks[12], (DEPTH, RET_WIDTH)),
        'g_mem_out': gain(ks[13], (DEPTH, MEM_WIDTH)),
        'g_mem': gain(ks[14], (DEPTH, D_MODEL)),
        'w_mem_kv': nrm(ks[15], (DEPTH, D_MODEL, 2 * MEM_WIDTH), D_MODEL ** -0.5),
        'g_ffn': gain(ks[16], (DEPTH, D_MODEL)),
        'w_peer_q': nrm(ks[17], (DEPTH, D_MODEL, PEER_HEADS * PEER_QDIM), D_MODEL ** -0.5),
        'peer_sub_keys': nrm(ks[18], (DEPTH, 2, PEER_KEYS, PEER_HALF), PEER_HALF ** -0.5),
        'peer_u': nrm(ks[19], (DEPTH, N_EXPERTS, D_MODEL), D_MODEL ** -0.5),
        'peer_v': nrm(ks[20], (DEPTH, N_EXPERTS, D_MODEL), PEER_V_SCALE),
        'g_final': gain(ks[21], (D_MODEL,)),
    }


def reference(x_prompt, x_sample, mem_prompt, cache_sb_k, cache_sb_v, state_ret, cache_mem_k, cache_mem_v,
              g_mix, w_in, w_out, g_sb_out, g_ret_out, g_mem_out, g_mem, w_mem_kv,
              g_ffn, w_peer_q, peer_sub_keys, peer_u, peer_v, g_final):
    f32 = jnp.float32
    Tp = x_prompt.shape[1]
    Ts = x_sample.shape[1]
    past = cache_sb_k.shape[2]
    pos_p = jnp.arange(Tp, dtype=jnp.int32)
    pos_s = past + jnp.arange(Ts, dtype=jnp.int32)
    pos_all = jnp.arange(past + Ts, dtype=jnp.int32)
    k_scale = HEAD_DIM ** -0.5
    xp, xs = x_prompt, x_sample
    sbk_p, sbv_p, st_p_l, mk_p_l, mv_p_l, sbk_s, sbv_s, st_s_l = [], [], [], [], [], [], [], []
    for l in range(DEPTH):
        hp = rmsnorm(xp, g_mix[l])
        hs = rmsnorm(xs, g_mix[l])
        qsp, ksp, vsp, qrp, krp, vrp, gtp, qmp = split_proj(hp, w_in[l])
        qss, kss, vss, qrs, krs, vrs, gts, qms = split_proj(hs, w_in[l])
        sb_p = stick_breaking_prompt(qsp, ksp, vsp)
        k_all = jnp.concatenate([cache_sb_k[l], kss], axis=1)
        v_all = jnp.concatenate([cache_sb_v[l], vss], axis=1)
        sb_s = sb_block(qss, k_all, v_all, pos_s, pos_all)
        ret_p, st_p = retention_prompt(rope(qrp, pos_p), rope(krp, pos_p) * k_scale, vrp.astype(f32))
        st_s, ret_s = ret_chunk(state_ret[l].astype(f32),
                                (rope(qrs, pos_s), rope(krs, pos_s) * k_scale, vrs.astype(f32)))
        mk_p, mv_p = memory_kv(mem_prompt, g_mem[l], w_mem_kv[l])
        mem_p = memory_attention(qmp, mk_p, mv_p)
        mem_s = memory_attention(qms, cache_mem_k[l], cache_mem_v[l])
        xp = xp + merge(sb_p, ret_p.astype(xp.dtype), gtp, mem_p, g_sb_out[l], g_ret_out[l], g_mem_out[l], w_out[l])
        xs = xs + merge(sb_s, ret_s.astype(xs.dtype), gts, mem_s, g_sb_out[l], g_ret_out[l], g_mem_out[l], w_out[l])
        xp = xp + peer(rmsnorm(xp, g_ffn[l]), w_peer_q[l], peer_sub_keys[l], peer_u[l], peer_v[l])
        xs = xs + peer(rmsnorm(xs, g_ffn[l]), w_peer_q[l], peer_sub_keys[l], peer_u[l], peer_v[l])
        sbk_p.append(ksp)
        sbv_p.append(vsp)
        st_p_l.append(st_p.astype(x_prompt.dtype))
        mk_p_l.append(mk_p)
        mv_p_l.append(mv_p)
        sbk_s.append(kss)
        sbv_s.append(vss)
        st_s_l.append(st_s.astype(state_ret.dtype))
    y_prompt = rmsnorm(xp, g_final)
    y_sample = rmsnorm(xs, g_final)
    sb_k_prompt = jnp.stack(sbk_p)
    sb_v_prompt = jnp.stack(sbv_p)
    ret_state_prompt = jnp.stack(st_p_l)
    mem_k_prompt = jnp.stack(mk_p_l)
    mem_v_prompt = jnp.stack(mv_p_l)
    sb_k_sample = jnp.stack(sbk_s)
    sb_v_sample = jnp.stack(sbv_s)
    ret_state_sample = jnp.stack(st_s_l)
    return (y_prompt, y_sample, sb_k_prompt, sb_v_prompt, ret_state_prompt, mem_k_prompt, mem_v_prompt,
            sb_k_sample, sb_v_sample, ret_state_sample)
```

```python
import functools

import numpy as np
import jax
import jax.numpy as jnp
from jax import lax
from jax.experimental import pallas as pl
from jax.experimental.pallas import tpu as pltpu

F32 = jnp.float32
BF16 = jnp.bfloat16
I32 = jnp.int32

EPS = 1e-6
ROPE_BASE = 10000.0
PEER_TOPK = 16
RET_CHUNK = 64
LANES = 128
HALF_ROWS = 4
VMEM_LIMIT = 56 * 1024 * 1024


def _cparams(*sem):
    return pltpu.CompilerParams(dimension_semantics=sem, vmem_limit_bytes=VMEM_LIMIT)


def _dot(a, b):
    return jnp.dot(a, b, preferred_element_type=F32)


def _dot_nt(a, b):
    return lax.dot_general(a, b, (((1,), (1,)), ((), ())), preferred_element_type=F32)


def _split_bf16(x):
    hi = x.astype(BF16)
    lo = (x - hi.astype(F32)).astype(BF16)
    return hi, lo


def _dot_split(x, m_bf16):
    hi, lo = _split_bf16(x)
    return _dot(hi, m_bf16) + _dot(lo, m_bf16)


def _norm_proj_kernel(x_ref, g_ref, w_ref, *out_refs, cols):
    x = x_ref[...]
    ms = jnp.mean(x * x, axis=-1, keepdims=True)
    h = x * lax.rsqrt(ms + EPS) * g_ref[...]
    p = _dot(h.astype(BF16), w_ref[...])
    for ref, (lo, hi, scale) in zip(out_refs, cols):
        v = p[:, lo:hi]
        if scale != 1.0:
            v = v * scale
        ref[...] = v.astype(ref.dtype)


def _norm_proj(x, g, w_bf16, outs, tm):
    n, d = x.shape
    wcols = w_bf16.shape[1]
    out_shape = [jax.ShapeDtypeStruct((n, hi - lo), dt) for lo, hi, _, dt in outs]
    out_specs = [pl.BlockSpec((tm, hi - lo), lambda i: (i, 0)) for lo, hi, _, _ in outs]
    return pl.pallas_call(
        functools.partial(_norm_proj_kernel, cols=tuple((lo, hi, sc) for lo, hi, sc, _ in outs)),
        grid=(n // tm,),
        in_specs=[pl.BlockSpec((tm, d), lambda i: (i, 0)),
                  pl.BlockSpec((1, d), lambda i: (0, 0)),
                  pl.BlockSpec((d, wcols), lambda i: (0, 0))],
        out_specs=out_specs,
        out_shape=out_shape,
        compiler_params=_cparams("parallel"),
        name="norm_proj",
    )(x, g.reshape(1, d), w_bf16)


def _suffix_matrix(tk):
    j = np.arange(2 * tk)[:, None]
    s = np.arange(2 * tk)[None, :]
    return jnp.asarray(((j // tk == s // tk) & (j >= s)).astype(np.float32), dtype=BF16)


def _sb_step(q2, kb, vb, u, carry, tk, hd, valid):
    c0, c1, acc = carry
    tq = q2.shape[0]
    lane = lax.broadcasted_iota(I32, kb.shape, 1)
    first = lane < hd
    zero = jnp.zeros_like(kb)
    kk = jnp.concatenate([jnp.where(first, kb, zero), jnp.where(first, zero, kb)], axis=0)
    vv = jnp.concatenate([jnp.where(first, vb, zero), jnp.where(first, zero, vb)], axis=0)
    z = _dot_nt(q2, kk)
    sp = jnp.log(1.0 + jnp.exp(-jnp.abs(z)))
    lk = jnp.minimum(-z, 0.0) - sp
    if valid is not None:
        lk = jnp.where(valid, lk, 0.0)
    m = _dot_split(lk, u)
    c = jnp.concatenate([jnp.broadcast_to(c0, (tq, tk)), jnp.broadcast_to(c1, (tq, tk))], axis=1)
    a = jnp.exp(z + c + m)
    if valid is not None:
        a = jnp.where(valid, a, 0.0)
    acc = acc + _dot(a.astype(BF16), vv)
    c0 = c0 + jnp.sum(lk[:, :tk], axis=-1, keepdims=True)
    c1 = c1 + jnp.sum(lk[:, tk:], axis=-1, keepdims=True)
    return c0, c1, acc


def _sb_prompt_kernel(q_ref, k_ref, v_ref, u_ref, o_ref, *, tq, tk, hd):
    q0 = pl.program_id(2) * tq
    q2 = q_ref[0]
    u = u_ref[...]
    row = lax.broadcasted_iota(I32, (tq, 2 * tk), 0)
    koff = lax.broadcasted_iota(I32, (tq, 2 * tk), 1) & (tk - 1)
    carry = (jnp.zeros((tq, 1), F32), jnp.zeros((tq, 1), F32), jnp.zeros((tq, LANES), F32))
    nd = tq // tk
    for d in range(nd):
        off = (nd - 1 - d) * tk
        valid = (koff + off) < row
        ks = pl.multiple_of(q0 + off, tk)
        carry = _sb_step(q2, k_ref[0, pl.ds(ks, tk), :], v_ref[0, pl.ds(ks, tk), :], u, carry, tk, hd, valid)

    def body(i, carry):
        ks = pl.multiple_of(q0 - (i + 1) * tk, tk)
        return _sb_step(q2, k_ref[0, pl.ds(ks, tk), :], v_ref[0, pl.ds(ks, tk), :], u, carry, tk, hd, None)

    carry = lax.fori_loop(0, q0 // tk, body, carry)
    o_ref[0] = carry[2]


def _sb_prompt(q, k, v, hd, tq=256, tk=128):
    b, t, w = q.shape
    return pl.pallas_call(
        functools.partial(_sb_prompt_kernel, tq=tq, tk=tk, hd=hd),
        grid=(b, w // LANES, t // tq),
        in_specs=[pl.BlockSpec((1, tq, LANES), lambda bi, hp, qi: (bi, qi, hp)),
                  pl.BlockSpec((1, t, LANES), lambda bi, hp, qi: (bi, 0, hp)),
                  pl.BlockSpec((1, t, LANES), lambda bi, hp, qi: (bi, 0, hp)),
                  pl.BlockSpec((2 * tk, 2 * tk), lambda bi, hp, qi: (0, 0))],
        out_specs=pl.BlockSpec((1, tq, LANES), lambda bi, hp, qi: (bi, qi, hp)),
        out_shape=jax.ShapeDtypeStruct((b, t, w), F32),
        compiler_params=_cparams("parallel", "parallel", "arbitrary"),
        name="sb_prompt",
    )(q, k, v, _suffix_matrix(tk))


def _sb_sample_kernel(q_ref, kn_ref, vn_ref, kc_ref, vc_ref, un_ref, u_ref, o_ref, *, ts, tk, hd, past):
    q2 = q_ref[0]
    row = lax.broadcasted_iota(I32, (ts, 2 * ts), 0)
    koff = lax.broadcasted_iota(I32, (ts, 2 * ts), 1) & (ts - 1)
    carry = (jnp.zeros((ts, 1), F32), jnp.zeros((ts, 1), F32), jnp.zeros((ts, LANES), F32))
    carry = _sb_step(q2, kn_ref[0], vn_ref[0], un_ref[...], carry, ts, hd, koff < row)
    u = u_ref[...]

    def body(i, carry):
        ks = pl.multiple_of(past - (i + 1) * tk, tk)
        kb = kc_ref[0, pl.ds(ks, tk), :].astype(BF16)
        vb = vc_ref[0, pl.ds(ks, tk), :].astype(BF16)
        return _sb_step(q2, kb, vb, u, carry, tk, hd, None)

    carry = lax.fori_loop(0, past // tk, body, carry)
    o_ref[0] = carry[2]


def _sb_sample(q, kn, vn, kc, vc, hd, tk=128):
    b, ts, w = q.shape
    past = kc.shape[1]
    new_spec = pl.BlockSpec((1, ts, LANES), lambda bi, hp: (bi, 0, hp))
    cache_spec = pl.BlockSpec((1, past, LANES), lambda bi, hp: (bi, 0, hp))
    return pl.pallas_call(
        functools.partial(_sb_sample_kernel, ts=ts, tk=tk, hd=hd, past=past),
        grid=(b, w // LANES),
        in_specs=[new_spec, new_spec, new_spec, cache_spec, cache_spec,
                  pl.BlockSpec((2 * ts, 2 * ts), lambda bi, hp: (0, 0)),
                  pl.BlockSpec((2 * tk, 2 * tk), lambda bi, hp: (0, 0))],
        out_specs=new_spec,
        out_shape=jax.ShapeDtypeStruct((b, ts, w), F32),
        compiler_params=_cparams("parallel", "parallel"),
        name="sb_sample",
    )(q, kn, vn, kc, vc, _suffix_matrix(ts), _suffix_matrix(tk))


def _ret_kernel(q_ref, k_ref, v_ref, cos_ref, sin_ref, dec_ref, qd_ref, kd_ref, gl_ref, bd_ref, s0_ref,
                o_ref, st_ref, s_scr, *, tc, nh, hd, k_scale):
    t = pl.program_id(1)
    w = nh * hd
    L = RET_CHUNK

    @pl.when(t == 0)
    def _():
        s_scr[...] = s0_ref[0]

    lane = lax.broadcasted_iota(I32, (L, w), 1)
    first_half = (lane & (hd - 1)) < hd // 2
    head = lane // hd

    def rope(x, cos, sin):
        swapped = jnp.where(first_half, pltpu.roll(x, w - hd // 2, 1), pltpu.roll(x, hd // 2, 1))
        return x * cos + swapped * sin

    s = s_scr[...]
    for ci in range(tc // L):
        sl = slice(ci * L, (ci + 1) * L)
        cos = cos_ref[sl, :]
        sin = sin_ref[sl, :]
        q = rope(q_ref[0, sl, :], cos, sin)
        k = rope(k_ref[0, sl, :], cos, sin) * k_scale
        qb = q.astype(BF16)
        kb = k.astype(BF16)
        vb = v_ref[0, sl, :].astype(BF16)
        zero = jnp.zeros_like(qb)
        out = _dot(qb, s.astype(BF16)) * qd_ref[...]
        for h in range(nh):
            mh = head == h
            sc = _dot_nt(jnp.where(mh, qb, zero), kb)
            p = (sc * dec_ref[h]).astype(BF16)
            out = out + _dot(p, jnp.where(mh, vb, zero))
        o_ref[0, sl, :] = out
        kd = (k * kd_ref[...]).astype(BF16)
        outer = lax.dot_general(kd, vb, (((0,), (0,)), ((), ())), preferred_element_type=F32)
        s = gl_ref[...] * s + bd_ref[...] * outer
    s_scr[...] = s

    @pl.when(t == pl.num_programs(1) - 1)
    def _():
        st_ref[0] = s


def _retention(ret, pos, s0, nh, hd, tc):
    b, t, _ = ret.shape
    w = nh * hd
    L = RET_CHUNK
    half = hd // 2
    freqs = ROPE_BASE ** (-jnp.arange(half, dtype=F32) / half)
    ang = pos.astype(F32)[:, None] * freqs[None, :]
    cos = jnp.tile(jnp.cos(ang), (1, 2 * nh))
    sin = jnp.tile(jnp.concatenate([-jnp.sin(ang), jnp.sin(ang)], axis=1), (1, nh))
    lg = jnp.log1p(-jnp.exp2(-5.0 - jnp.arange(nh, dtype=F32)))
    idx = jnp.arange(L, dtype=F32)
    diff = idx[:, None] - idx[None, :]
    causal = diff >= 0
    dec = jnp.where(causal[None], jnp.exp(jnp.where(causal, diff, 0.0)[None] * lg[:, None, None]), 0.0)
    lg_lane = jnp.repeat(lg, hd)[None, :]
    qd = jnp.exp((idx + 1.0)[:, None] * lg_lane)
    kd = jnp.exp((L - 1.0 - idx)[:, None] * lg_lane)
    gl = jnp.exp(L * lg_lane)
    hid = jnp.arange(w) // hd
    bd = (hid[:, None] == hid[None, :]).astype(F32)
    tok = lambda j: pl.BlockSpec((1, tc, w), lambda bi, ti: (bi, ti, j))
    tab = pl.BlockSpec((tc, w), lambda bi, ti: (ti, 0))
    const = lambda shape: pl.BlockSpec(shape, lambda bi, ti: (0,) * len(shape))
    state = pl.BlockSpec((1, w, w), lambda bi, ti: (bi, 0, 0))
    return pl.pallas_call(
        functools.partial(_ret_kernel, tc=tc, nh=nh, hd=hd, k_scale=hd ** -0.5),
        grid=(b, t // tc),
        in_specs=[tok(0), tok(1), tok(2), tab, tab, const((nh, L, L)), const((L, w)), const((L, w)),
                  const((1, w)), const((w, w)), state],
        out_specs=[pl.BlockSpec((1, tc, w), lambda bi, ti: (bi, ti, 0)), state],
        out_shape=[jax.ShapeDtypeStruct((b, t, w), F32), jax.ShapeDtypeStruct((b, w, w), F32)],
        scratch_shapes=[pltpu.VMEM((w, w), F32)],
        compiler_params=_cparams("parallel", "arbitrary"),
        name="retention",
    )(ret, ret, ret, cos, sin, dec, qd, kd, gl, bd, s0)


def _block_diag_state(st, nh, hd):
    b = st.shape[0]
    eye = jnp.eye(nh, dtype=st.dtype)
    return jnp.einsum('bhde,hg->bhdge', st, eye).reshape(b, nh * hd, nh * hd)


def _diag_blocks(s, nh, hd):
    b = s.shape[0]
    s5 = s.reshape(b, nh, hd, nh, hd)
    return jnp.stack([s5[:, h, :, h, :] for h in range(nh)], axis=1)


def _head_mean_matrix(w, hd):
    hid = np.arange(w) // hd
    return jnp.asarray((hid[:, None] == hid[None, :]).astype(np.float32) / hd, dtype=BF16)


def _headnorm(x, g_mat, gain):
    ms = _dot_split(x * x, g_mat)
    return x * lax.rsqrt(ms + EPS) * gain


def _merge_kernel(x_ref, sb_ref, ret_ref, gate_ref, qm_ref, mk_ref, mv_ref, gsb_ref, gret_ref, gmo_ref,
                  msb_ref, mrt_ref, wo_ref, xlo_ref, xhi_ref, *, nh_mem, hd):
    tm = x_ref.shape[1]
    wsb = sb_ref.shape[2]
    wrt = ret_ref.shape[2]
    qm = qm_ref[0]
    mkb = mk_ref[0].astype(BF16)
    mvb = mv_ref[0].astype(BF16)
    head_q = lax.broadcasted_iota(I32, qm.shape, 1) // hd
    head_m = lax.broadcasted_iota(I32, mkb.shape, 1) // hd
    mem_o = jnp.zeros(qm.shape, F32)
    for h in range(nh_mem):
        s = _dot_nt(jnp.where(head_q == h, qm, jnp.zeros_like(qm)), mkb)
        e = jnp.exp(s - jnp.max(s, axis=-1, keepdims=True))
        p = e / jnp.sum(e, axis=-1, keepdims=True)
        mem_o = mem_o + _dot(p.astype(BF16), jnp.where(head_m == h, mvb, jnp.zeros_like(mvb)))
    sb_n = _headnorm(sb_ref[0], msb_ref[...], gsb_ref[...])
    gate = gate_ref[0]
    ret_n = _headnorm(ret_ref[0], mrt_ref[...], gret_ref[...]) * (gate / (1.0 + jnp.exp(-gate)))
    mem_n = _headnorm(mem_o, mrt_ref[...], gmo_ref[...])
    y = (_dot(sb_n.astype(BF16), wo_ref[0:wsb, :])
         + _dot(ret_n.astype(BF16), wo_ref[wsb:wsb + wrt, :])
         + _dot(mem_n.astype(BF16), wo_ref[wsb + wrt:, :]))
    x1 = x_ref[0] + y
    half = x1.shape[1] // 2
    xlo_ref[0] = x1[:, :half]
    xhi_ref[0] = x1[:, half:]


def _merge(x, sb_o, ret_o, ret, qm, mk, mv, g_sb, g_ret, g_mo, wo_bf16, hd, tm):
    b, t, d = x.shape
    wsb = sb_o.shape[2]
    wrt = ret_o.shape[2]
    nmem = mk.shape[1]
    half = d // 2
    tok = lambda wd, j=0: pl.BlockSpec((1, tm, wd), lambda bi, ti: (bi, ti, j))
    const = lambda shape: pl.BlockSpec(shape, lambda bi, ti: (0,) * len(shape))
    memspec = pl.BlockSpec((1, nmem, wrt), lambda bi, ti: (bi, 0, 0))
    return pl.pallas_call(
        functools.partial(_merge_kernel, nh_mem=wrt // hd, hd=hd),
        grid=(b, t // tm),
        in_specs=[tok(d), tok(wsb), tok(wrt), tok(wrt, 3), tok(wrt), memspec, memspec,
                  const((1, wsb)), const((1, wrt)), const((1, wrt)),
                  const((wsb, wsb)), const((wrt, wrt)), const((d, d))],
        out_specs=[tok(half), tok(half)],
        out_shape=[jax.ShapeDtypeStruct((b, t, half), F32)] * 2,
        compiler_params=_cparams("parallel", "parallel"),
        name="merge",
    )(x, sb_o, ret_o, ret, qm, mk, mv, g_sb.reshape(1, wsb), g_ret.reshape(1, wrt), g_mo.reshape(1, wrt),
      _head_mean_matrix(wsb, hd), _head_mean_matrix(wrt, hd), wo_bf16)


def _topk_rows(s, k):
    r = s.shape[0]
    rid = lax.broadcasted_iota(I32, s.shape, 0)
    vals, idxs = [], []
    for _ in range(k):
        m = jnp.max(s, axis=0, keepdims=True)
        i = jnp.min(jnp.where(s == m, rid, r), axis=0, keepdims=True)
        vals.append(m)
        idxs.append(i)
        s = jnp.where(rid == i, -jnp.inf, s)
    return jnp.concatenate(vals, axis=0), jnp.concatenate(idxs, axis=0)


def _take_rows(tbl, idx):
    out = jnp.zeros(idx.shape, tbl.dtype)
    for a in range(tbl.shape[0]):
        out = out + jnp.where(idx == a, tbl[a:a + 1, :], 0)
    return out


def _peer_score_kernel(xlo_ref, xhi_ref, g_ref, wq_ref, sk_ref, hlo_ref, hhi_ref, eid_ref, gw_ref, hb_scr,
                       *, topk, nkeys):
    @pl.when(pl.program_id(1) == 0)
    def _():
        x = jnp.concatenate([xlo_ref[...], xhi_ref[...]], axis=1)
        ms = jnp.mean(x * x, axis=-1, keepdims=True)
        hn = x * lax.rsqrt(ms + EPS) * g_ref[...]
        half = hn.shape[1] // 2
        hlo_ref[...] = hn[:, :half]
        hhi_ref[...] = hn[:, half:]
        hb_scr[...] = hn.astype(BF16)

    qb = _dot(hb_scr[...], wq_ref[...]).astype(BF16)
    hq = qb.shape[1] // 2
    s0 = _dot_nt(sk_ref[0], qb[:, :hq])
    s1 = _dot_nt(sk_ref[1], qb[:, hq:])
    sv0, si0 = _topk_rows(s0, topk)
    sv1, si1 = _topk_rows(s1, topk)
    cand = jnp.concatenate([sv0[a:a + 1, :] + sv1 for a in range(topk)], axis=0)
    cv, ci = _topk_rows(cand, topk)
    e1 = _take_rows(si0, ci // topk)
    e2 = _take_rows(si1, ci % topk)
    eid_ref[...] = e1 * nkeys + e2
    e = jnp.exp(cv - cv[0:1, :])
    gw_ref[...] = e / jnp.sum(e, axis=0, keepdims=True)


def _peer_score(xlo, xhi, g, wq_bf16, sk_bf16, tm):
    n, half = xlo.shape
    d = 2 * half
    nkeys, khalf = sk_bf16.shape[1], sk_bf16.shape[2]
    nheads = wq_bf16.shape[1] // (2 * khalf)
    k = PEER_TOPK
    xs = pl.BlockSpec((tm, half), lambda i, h: (i, 0))
    return pl.pallas_call(
        functools.partial(_peer_score_kernel, topk=k, nkeys=nkeys),
        grid=(n // tm, nheads),
        in_specs=[xs, xs, pl.BlockSpec((1, d), lambda i, h: (0, 0)),
                  pl.BlockSpec((d, 2 * khalf), lambda i, h: (0, h)),
                  pl.BlockSpec((2, nkeys, khalf), lambda i, h: (0, 0, 0))],
        out_specs=[xs, xs, pl.BlockSpec((k, tm), lambda i, h: (h, i)), pl.BlockSpec((k, tm), lambda i, h: (h, i))],
        out_shape=[jax.ShapeDtypeStruct((n, half), F32), jax.ShapeDtypeStruct((n, half), F32),
                   jax.ShapeDtypeStruct((nheads * k, n), I32), jax.ShapeDtypeStruct((nheads * k, n), F32)],
        scratch_shapes=[pltpu.VMEM((tm, d), BF16)],
        compiler_params=_cparams("parallel", "arbitrary"),
        name="peer_score",
    )(xlo, xhi, g.reshape(1, d), wq_bf16, sk_bf16)


def _pack_table(tab):
    e, d = tab.shape
    bits = lax.bitcast_convert_type(tab.astype(BF16), jnp.uint16).astype(jnp.uint32)
    word = bits[:, :d // 2] | (bits[:, d // 2:] << 16)
    return lax.bitcast_convert_type(word, I32).reshape(e * HALF_ROWS, LANES)


def _unpack(wd):
    lo = lax.bitcast_convert_type(wd << 16, F32)
    hi = lax.bitcast_convert_type(wd & (-65536), F32)
    return lo, hi


def _erf(x):
    return lax.erf(x)


def _peer_u_kernel(eid_ref, hlo_ref, hhi_ref, gw_ref, tab_ref, w_ref, p_scr, act_scr, *, tg, npairs):
    ones = jnp.ones((8, LANES), BF16)

    def token(t, carry):
        r4 = pl.multiple_of(t * HALF_ROWS, HALF_ROWS)
        hl = hlo_ref[pl.ds(r4, HALF_ROWS), :]
        hh = hhi_ref[pl.ds(r4, HALF_ROWS), :]
        for j in range(npairs):
            e4 = pl.multiple_of(eid_ref[j, t] * HALF_ROWS, HALF_ROWS)
            lo, hi = _unpack(tab_ref[pl.ds(e4, HALF_ROWS), :])
            p_scr[j * HALF_ROWS:(j + 1) * HALF_ROWS, :] = lo * hl + hi * hh
        q = p_scr[pl.ds(0, npairs, stride=HALF_ROWS), :]
        for s in range(1, HALF_ROWS):
            q = q + p_scr[pl.ds(s, npairs, stride=HALF_ROWS), :]
        qh, ql = _split_bf16(q)
        r = _dot_nt(ones, qh) + _dot_nt(ones, ql)
        act_scr[pl.ds(t, 1), :] = r[0:1, :]
        return carry

    lax.fori_loop(0, tg, token, 0)
    act = act_scr[...].T
    gelu = 0.5 * act * (1.0 + _erf(act * (2.0 ** -0.5)))
    w_ref[...] = gw_ref[...] * gelu


def _peer_u(eid, hlo, hhi, gw, tab_packed, tg):
    npairs, n = eid.shape
    rows = tg * HALF_ROWS
    hspec = pl.BlockSpec((rows, LANES), lambda i: (i, 0))
    return pl.pallas_call(
        functools.partial(_peer_u_kernel, tg=tg, npairs=npairs),
        grid=(n // tg,),
        in_specs=[pl.BlockSpec((npairs, tg), lambda i: (0, i), memory_space=pltpu.SMEM),
                  hspec, hspec,
                  pl.BlockSpec((npairs, tg), lambda i: (0, i)),
                  pl.BlockSpec(memory_space=pltpu.VMEM)],
        out_specs=pl.BlockSpec((npairs, tg), lambda i: (0, i)),
        out_shape=jax.ShapeDtypeStruct((npairs, n), F32),
        scratch_shapes=[pltpu.VMEM((npairs * HALF_ROWS, LANES), F32), pltpu.VMEM((tg, npairs), F32)],
        compiler_params=_cparams("arbitrary"),
        name="peer_u",
    )(eid, hlo.reshape(n * HALF_ROWS, LANES), hhi.reshape(n * HALF_ROWS, LANES), gw, tab_packed)


def _peer_v_kernel(eid_ref, w_ref, tab_ref, olo_ref, ohi_ref, *, tg, npairs, nacc):
    def token(t, carry):
        acc_lo = [jnp.zeros((HALF_ROWS, LANES), F32) for _ in range(nacc)]
        acc_hi = [jnp.zeros((HALF_ROWS, LANES), F32) for _ in range(nacc)]
        for j in range(npairs):
            e4 = pl.multiple_of(eid_ref[j, t] * HALF_ROWS, HALF_ROWS)
            lo, hi = _unpack(tab_ref[pl.ds(e4, HALF_ROWS), :])
            wt = w_ref[j, t]
            acc_lo[j % nacc] = acc_lo[j % nacc] + wt * lo
            acc_hi[j % nacc] = acc_hi[j % nacc] + wt * hi
        r4 = pl.multiple_of(t * HALF_ROWS, HALF_ROWS)
        olo_ref[pl.ds(r4, HALF_ROWS), :] = functools.reduce(lambda a, b: a + b, acc_lo)
        ohi_ref[pl.ds(r4, HALF_ROWS), :] = functools.reduce(lambda a, b: a + b, acc_hi)
        return carry

    lax.fori_loop(0, tg, token, 0)


def _peer_v(eid, w, tab_packed, tg):
    npairs, n = eid.shape
    rows = tg * HALF_ROWS
    ospec = pl.BlockSpec((rows, LANES), lambda i: (i, 0))
    sspec = pl.BlockSpec((npairs, tg), lambda i: (0, i), memory_space=pltpu.SMEM)
    olo, ohi = pl.pallas_call(
        functools.partial(_peer_v_kernel, tg=tg, npairs=npairs, nacc=2),
        grid=(n // tg,),
        in_specs=[sspec, sspec, pl.BlockSpec(memory_space=pltpu.VMEM)],
        out_specs=[ospec, ospec],
        out_shape=[jax.ShapeDtypeStruct((n * HALF_ROWS, LANES), F32)] * 2,
        compiler_params=_cparams("arbitrary"),
        name="peer_v",
    )(eid, w, tab_packed)
    return olo.reshape(n, HALF_ROWS * LANES), ohi.reshape(n, HALF_ROWS * LANES)


def _final_kernel(xlo_ref, xhi_ref, plo_ref, phi_ref, g_ref, y_ref):
    x = jnp.concatenate([xlo_ref[...] + plo_ref[...], xhi_ref[...] + phi_ref[...]], axis=1)
    ms = jnp.mean(x * x, axis=-1, keepdims=True)
    y_ref[...] = x * lax.rsqrt(ms + EPS) * g_ref[...]


def _final(xlo, xhi, plo, phi, g, tm):
    n, half = xlo.shape
    hs = pl.BlockSpec((tm, half), lambda i: (i, 0))
    return pl.pallas_call(
        _final_kernel,
        grid=(n // tm,),
        in_specs=[hs, hs, hs, hs, pl.BlockSpec((1, 2 * half), lambda i: (0, 0))],
        out_specs=pl.BlockSpec((tm, 2 * half), lambda i: (i, 0)),
        out_shape=jax.ShapeDtypeStruct((n, 2 * half), F32),
        compiler_params=_cparams("parallel"),
        name="final_norm",
    )(xlo, xhi, plo, phi, g.reshape(1, 2 * half))


def _row_tile(n, pref):
    t = min(pref, n)
    assert n % t == 0, (n, t)
    return t


def _stream(x, w_in_bf, g_mix, dims):
    b, t, d = x.shape
    wsb, wrt, wmm = dims
    scale = 64 ** -0.5
    o = 3 * wsb
    outs = [(0, wsb, scale, BF16),
            (wsb, 2 * wsb, 1.0, F32), (2 * wsb, o, 1.0, F32),
            (wsb, 2 * wsb, 1.0, BF16), (2 * wsb, o, 1.0, BF16),
            (o, o + 4 * wrt, 1.0, F32),
            (o + 4 * wrt, o + 4 * wrt + wmm, scale, BF16)]
    res = _norm_proj(x.reshape(b * t, d), g_mix, w_in_bf, outs, _row_tile(b * t, 512))
    return [r.reshape(b, t, r.shape[1]) for r in res]


def _peer_and_final(xlo, xhi, g_ffn, wq_bf, sk_bf, u_packed, v_packed, g_final):
    n = xlo.shape[0]
    hlo, hhi, eid, gw = _peer_score(xlo, xhi, g_ffn, wq_bf, sk_bf, _row_tile(n, 256))
    tg = _row_tile(n, 128)
    w = _peer_u(eid, hlo, hhi, gw, u_packed, tg)
    plo, phi = _peer_v(eid, w, v_packed, tg)
    return _final(xlo, xhi, plo, phi, g_final, _row_tile(n, 512))


def kernel(x_prompt, x_sample, mem_prompt, cache_sb_k, cache_sb_v, state_ret, cache_mem_k, cache_mem_v,
           g_mix, w_in, w_out, g_sb_out, g_ret_out, g_mem_out, g_mem, w_mem_kv,
           g_ffn, w_peer_q, peer_sub_keys, peer_u, peer_v, g_final):
    depth = w_in.shape[0]
    assert depth == 1, "single-layer step"
    bp, tp, d = x_prompt.shape
    bs, ts, _ = x_sample.shape
    _, _, past, sb_heads, hd = cache_sb_k.shape
    ret_heads = state_ret.shape[2]
    mem_heads = cache_mem_k.shape[3]
    nmem = mem_prompt.shape[1]
    wsb, wrt, wmm = sb_heads * hd, ret_heads * hd, mem_heads * hd
    assert hd == 64 and wrt == wmm and d == 2 * HALF_ROWS * LANES
    l = 0
    w_in_bf = w_in[l].astype(BF16)
    wo_bf = w_out[l].astype(BF16)
    dims = (wsb, wrt, wmm)

    q_p, k_p, v_p, kb_p, vb_p, ret_p, qm_p = _stream(x_prompt, w_in_bf, g_mix[l], dims)
    sb_p = _sb_prompt(q_p, kb_p, vb_p, hd)
    zeros_state = jnp.zeros((bp, wrt, wrt), F32)
    ro_p, st_p = _retention(ret_p, jnp.arange(tp, dtype=I32), zeros_state, ret_heads, hd, _row_tile(tp, 512))
    mk_p, mv_p = _norm_proj(mem_prompt.reshape(bp * nmem, d), g_mem[l], w_mem_kv[l].astype(BF16),
                            [(0, wmm, 1.0, F32), (wmm, 2 * wmm, 1.0, F32)], _row_tile(bp * nmem, 512))
    mk_p = mk_p.reshape(bp, nmem, wmm)
    mv_p = mv_p.reshape(bp, nmem, wmm)
    xlo_p, xhi_p = _merge(x_prompt, sb_p, ro_p, ret_p, qm_p, mk_p, mv_p, g_sb_out[l], g_ret_out[l], g_mem_out[l],
                          wo_bf, hd, _row_tile(tp, 256))

    q_s, k_s, v_s, kb_s, vb_s, ret_s, qm_s = _stream(x_sample, w_in_bf, g_mix[l], dims)
    sb_s = _sb_sample(q_s, kb_s, vb_s, cache_sb_k[l].reshape(bs, past, wsb), cache_sb_v[l].reshape(bs, past, wsb), hd)
    s0 = _block_diag_state(state_ret[l].astype(F32), ret_heads, hd)
    ro_s, st_s = _retention(ret_s, past + jnp.arange(ts, dtype=I32), s0, ret_heads, hd, _row_tile(ts, 512))
    xlo_s, xhi_s = _merge(x_sample, sb_s, ro_s, ret_s, qm_s, cache_mem_k[l].reshape(bs, nmem, wmm),
                          cache_mem_v[l].reshape(bs, nmem, wmm), g_sb_out[l], g_ret_out[l], g_mem_out[l],
                          wo_bf, hd, _row_tile(ts, 256))

    wq_bf = w_peer_q[l].astype(BF16)
    sk_bf = peer_sub_keys[l].astype(BF16)
    u_packed = _pack_table(peer_u[l])
    v_packed = _pack_table(peer_v[l])
    half = d // 2
    y_p = _peer_and_final(xlo_p.reshape(bp * tp, half), xhi_p.reshape(bp * tp, half), g_ffn[l], wq_bf, sk_bf,
                          u_packed, v_packed, g_final)
    y_s = _peer_and_final(xlo_s.reshape(bs * ts, half), xhi_s.reshape(bs * ts, half), g_ffn[l], wq_bf, sk_bf,
                          u_packed, v_packed, g_final)

    return (y_p.reshape(bp, tp, d), y_s.reshape(bs, ts, d),
            k_p.reshape(1, bp, tp, sb_heads, hd), v_p.reshape(1, bp, tp, sb_heads, hd),
            _diag_blocks(st_p, ret_heads, hd)[None],
            mk_p.reshape(1, bp, nmem, mem_heads, hd), mv_p.reshape(1, bp, nmem, mem_heads, hd),
            k_s.reshape(1, bs, ts, sb_heads, hd), v_s.reshape(1, bs, ts, sb_heads, hd),
            _diag_blocks(st_s, ret_heads, hd)[None].astype(state_ret.dtype))
```

```python
import functools

import numpy as np
import jax
import jax.numpy as jnp
from jax import lax
from jax.experimental import pallas as pl
from jax.experimental.pallas import tpu as pltpu

F32 = jnp.float32
BF16 = jnp.bfloat16
I32 = jnp.int32

EPS = 1e-6
ROPE_BASE = 10000.0
PEER_TOPK = 16
RET_CHUNK = 64
LANES = 128
HALF_ROWS = 4
VMEM_LIMIT = 56 * 1024 * 1024
SB_DEAD_LOG = -105.0


def _cparams(*sem):
    return pltpu.CompilerParams(dimension_semantics=sem, vmem_limit_bytes=VMEM_LIMIT)


def _dot(a, b):
    return jnp.dot(a, b, preferred_element_type=F32)


def _dot_nt(a, b):
    return lax.dot_general(a, b, (((1,), (1,)), ((), ())), preferred_element_type=F32)


def _split_bf16(x):
    hi = x.astype(BF16)
    lo = (x - hi.astype(F32)).astype(BF16)
    return hi, lo


def _dot_split(x, m_bf16):
    hi, lo = _split_bf16(x)
    return _dot(hi, m_bf16) + _dot(lo, m_bf16)


def _norm_proj_kernel(x_ref, g_ref, w_ref, *out_refs, cols):
    x = x_ref[...]
    ms = jnp.mean(x * x, axis=-1, keepdims=True)
    h = x * lax.rsqrt(ms + EPS) * g_ref[...]
    p = _dot(h.astype(BF16), w_ref[...])
    for ref, (lo, hi, scale) in zip(out_refs, cols):
        v = p[:, lo:hi]
        if scale != 1.0:
            v = v * scale
        ref[...] = v.astype(ref.dtype)


def _norm_proj(x, g, w_bf16, outs, tm):
    n, d = x.shape
    wcols = w_bf16.shape[1]
    out_shape = [jax.ShapeDtypeStruct((n, hi - lo), dt) for lo, hi, _, dt in outs]
    out_specs = [pl.BlockSpec((tm, hi - lo), lambda i: (i, 0)) for lo, hi, _, _ in outs]
    return pl.pallas_call(
        functools.partial(_norm_proj_kernel, cols=tuple((lo, hi, sc) for lo, hi, sc, _ in outs)),
        grid=(n // tm,),
        in_specs=[pl.BlockSpec((tm, d), lambda i: (i, 0)),
                  pl.BlockSpec((1, d), lambda i: (0, 0)),
                  pl.BlockSpec((d, wcols), lambda i: (0, 0))],
        out_specs=out_specs,
        out_shape=out_shape,
        compiler_params=_cparams("parallel"),
        name="norm_proj",
    )(x, g.reshape(1, d), w_bf16)


def _suffix_matrix(tk):
    j = np.arange(2 * tk)[:, None]
    s = np.arange(2 * tk)[None, :]
    return jnp.asarray(((j // tk == s // tk) & (j >= s)).astype(np.float32), dtype=BF16)


def _sb_step(q2, kb, vb, u, carry, tk, hd, valid):
    c0, c1, acc = carry
    tq = q2.shape[0]
    lane = lax.broadcasted_iota(I32, kb.shape, 1)
    first = lane < hd
    zero = jnp.zeros_like(kb)
    kk = jnp.concatenate([jnp.where(first, kb, zero), jnp.where(first, zero, kb)], axis=0)
    vv = jnp.concatenate([jnp.where(first, vb, zero), jnp.where(first, zero, vb)], axis=0)
    z = _dot_nt(q2, kk)
    sp = jnp.log(1.0 + jnp.exp(-jnp.abs(z)))
    lk = jnp.minimum(-z, 0.0) - sp
    if valid is not None:
        lk = jnp.where(valid, lk, 0.0)
    m = _dot_split(lk, u)
    c = jnp.concatenate([jnp.broadcast_to(c0, (tq, tk)), jnp.broadcast_to(c1, (tq, tk))], axis=1)
    a = jnp.exp(z + c + m)
    if valid is not None:
        a = jnp.where(valid, a, 0.0)
    acc = acc + _dot(a.astype(BF16), vv)
    c0 = c0 + jnp.sum(lk[:, :tk], axis=-1, keepdims=True)
    c1 = c1 + jnp.sum(lk[:, tk:], axis=-1, keepdims=True)
    return c0, c1, acc


def _sb_alive(c0, c1):
    return jnp.max(jnp.maximum(c0, c1)) > SB_DEAD_LOG


def _sb_far_blocks(q2, block, nblk, u, carry, tk, hd):
    def cond(state):
        return jnp.logical_and(state[0] < nblk, state[1])

    def body(state):
        i, _, c0, c1, acc = state
        kb, vb = block(i)
        c0, c1, acc = _sb_step(q2, kb, vb, u, (c0, c1, acc), tk, hd, None)
        return i + 1, _sb_alive(c0, c1), c0, c1, acc

    state = lax.while_loop(cond, body, (jnp.int32(0), _sb_alive(carry[0], carry[1])) + tuple(carry))
    return state[2:]


def _sb_prompt_kernel(q_ref, k_ref, v_ref, u_ref, o_ref, *, tq, tk, hd):
    q0 = pl.program_id(2) * tq
    q2 = q_ref[0]
    u = u_ref[...]
    row = lax.broadcasted_iota(I32, (tq, 2 * tk), 0)
    koff = lax.broadcasted_iota(I32, (tq, 2 * tk), 1) & (tk - 1)
    carry = (jnp.zeros((tq, 1), F32), jnp.zeros((tq, 1), F32), jnp.zeros((tq, LANES), F32))
    nd = tq // tk
    for d in range(nd):
        off = (nd - 1 - d) * tk
        valid = (koff + off) < row
        ks = pl.multiple_of(q0 + off, tk)
        carry = _sb_step(q2, k_ref[0, pl.ds(ks, tk), :], v_ref[0, pl.ds(ks, tk), :], u, carry, tk, hd, valid)

    def block(i):
        ks = pl.multiple_of(q0 - (i + 1) * tk, tk)
        return k_ref[0, pl.ds(ks, tk), :], v_ref[0, pl.ds(ks, tk), :]

    carry = _sb_far_blocks(q2, block, q0 // tk, u, carry, tk, hd)
    o_ref[0] = carry[2]


def _sb_prompt(q, k, v, hd, tq=256, tk=128):
    b, t, w = q.shape
    return pl.pallas_call(
        functools.partial(_sb_prompt_kernel, tq=tq, tk=tk, hd=hd),
        grid=(b, w // LANES, t // tq),
        in_specs=[pl.BlockSpec((1, tq, LANES), lambda bi, hp, qi: (bi, qi, hp)),
                  pl.BlockSpec((1, t, LANES), lambda bi, hp, qi: (bi, 0, hp)),
                  pl.BlockSpec((1, t, LANES), lambda bi, hp, qi: (bi, 0, hp)),
                  pl.BlockSpec((2 * tk, 2 * tk), lambda bi, hp, qi: (0, 0))],
        out_specs=pl.BlockSpec((1, tq, LANES), lambda bi, hp, qi: (bi, qi, hp)),
        out_shape=jax.ShapeDtypeStruct((b, t, w), F32),
        compiler_params=_cparams("parallel", "parallel", "arbitrary"),
        name="sb_prompt",
    )(q, k, v, _suffix_matrix(tk))


def _sb_sample_kernel(q_ref, kn_ref, vn_ref, kc_ref, vc_ref, un_ref, u_ref, o_ref, *, ts, tk, hd, past):
    q2 = q_ref[0]
    row = lax.broadcasted_iota(I32, (ts, 2 * ts), 0)
    koff = lax.broadcasted_iota(I32, (ts, 2 * ts), 1) & (ts - 1)
    carry = (jnp.zeros((ts, 1), F32), jnp.zeros((ts, 1), F32), jnp.zeros((ts, LANES), F32))
    carry = _sb_step(q2, kn_ref[0], vn_ref[0], un_ref[...], carry, ts, hd, koff < row)
    u = u_ref[...]

    def block(i):
        ks = pl.multiple_of(past - (i + 1) * tk, tk)
        return kc_ref[0, pl.ds(ks, tk), :].astype(BF16), vc_ref[0, pl.ds(ks, tk), :].astype(BF16)

    carry = _sb_far_blocks(q2, block, past // tk, u, carry, tk, hd)
    o_ref[0] = carry[2]


def _sb_sample(q, kn, vn, kc, vc, hd, tk=128):
    b, ts, w = q.shape
    past = kc.shape[1]
    new_spec = pl.BlockSpec((1, ts, LANES), lambda bi, hp: (bi, 0, hp))
    cache_spec = pl.BlockSpec((1, past, LANES), lambda bi, hp: (bi, 0, hp))
    return pl.pallas_call(
        functools.partial(_sb_sample_kernel, ts=ts, tk=tk, hd=hd, past=past),
        grid=(b, w // LANES),
        in_specs=[new_spec, new_spec, new_spec, cache_spec, cache_spec,
                  pl.BlockSpec((2 * ts, 2 * ts), lambda bi, hp: (0, 0)),
                  pl.BlockSpec((2 * tk, 2 * tk), lambda bi, hp: (0, 0))],
        out_specs=new_spec,
        out_shape=jax.ShapeDtypeStruct((b, ts, w), F32),
        compiler_params=_cparams("parallel", "parallel"),
        name="sb_sample",
    )(q, kn, vn, kc, vc, _suffix_matrix(ts), _suffix_matrix(tk))


def _ret_kernel(q_ref, k_ref, v_ref, cos_ref, sin_ref, dec_ref, qd_ref, kd_ref, gl_ref, bd_ref, s0_ref,
                o_ref, st_ref, s_scr, *, tc, nh, hd, k_scale):
    t = pl.program_id(1)
    w = nh * hd
    L = RET_CHUNK

    @pl.when(t == 0)
    def _():
        s_scr[...] = s0_ref[0]

    lane = lax.broadcasted_iota(I32, (L, w), 1)
    first_half = (lane & (hd - 1)) < hd // 2
    head = lane // hd

    def rope(x, cos, sin):
        swapped = jnp.where(first_half, pltpu.roll(x, w - hd // 2, 1), pltpu.roll(x, hd // 2, 1))
        return x * cos + swapped * sin

    s = s_scr[...]
    for ci in range(tc // L):
        sl = slice(ci * L, (ci + 1) * L)
        cos = cos_ref[sl, :]
        sin = sin_ref[sl, :]
        q = rope(q_ref[0, sl, :], cos, sin)
        k = rope(k_ref[0, sl, :], cos, sin) * k_scale
        qb = q.astype(BF16)
        kb = k.astype(BF16)
        vb = v_ref[0, sl, :].astype(BF16)
        zero = jnp.zeros_like(qb)
        out = _dot(qb, s.astype(BF16)) * qd_ref[...]
        for h in range(nh):
            mh = head == h
            sc = _dot_nt(jnp.where(mh, qb, zero), kb)
            p = (sc * dec_ref[h]).astype(BF16)
            out = out + _dot(p, jnp.where(mh, vb, zero))
        o_ref[0, sl, :] = out
        kd = (k * kd_ref[...]).astype(BF16)
        outer = lax.dot_general(kd, vb, (((0,), (0,)), ((), ())), preferred_element_type=F32)
        s = gl_ref[...] * s + bd_ref[...] * outer
    s_scr[...] = s

    @pl.when(t == pl.num_programs(1) - 1)
    def _():
        st_ref[0] = s


def _retention(ret, pos, s0, nh, hd, tc):
    b, t, _ = ret.shape
    w = nh * hd
    L = RET_CHUNK
    half = hd // 2
    freqs = ROPE_BASE ** (-jnp.arange(half, dtype=F32) / half)
    ang = pos.astype(F32)[:, None] * freqs[None, :]
    cos = jnp.tile(jnp.cos(ang), (1, 2 * nh))
    sin = jnp.tile(jnp.concatenate([-jnp.sin(ang), jnp.sin(ang)], axis=1), (1, nh))
    lg = jnp.log1p(-jnp.exp2(-5.0 - jnp.arange(nh, dtype=F32)))
    idx = jnp.arange(L, dtype=F32)
    diff = idx[:, None] - idx[None, :]
    causal = diff >= 0
    dec = jnp.where(causal[None], jnp.exp(jnp.where(causal, diff, 0.0)[None] * lg[:, None, None]), 0.0)
    lg_lane = jnp.repeat(lg, hd)[None, :]
    qd = jnp.exp((idx + 1.0)[:, None] * lg_lane)
    kd = jnp.exp((L - 1.0 - idx)[:, None] * lg_lane)
    gl = jnp.exp(L * lg_lane)
    hid = jnp.arange(w) // hd
    bd = (hid[:, None] == hid[None, :]).astype(F32)
    tok = lambda j: pl.BlockSpec((1, tc, w), lambda bi, ti: (bi, ti, j))
    tab = pl.BlockSpec((tc, w), lambda bi, ti: (ti, 0))
    const = lambda shape: pl.BlockSpec(shape, lambda bi, ti: (0,) * len(shape))
    state = pl.BlockSpec((1, w, w), lambda bi, ti: (bi, 0, 0))
    return pl.pallas_call(
        functools.partial(_ret_kernel, tc=tc, nh=nh, hd=hd, k_scale=hd ** -0.5),
        grid=(b, t // tc),
        in_specs=[tok(0), tok(1), tok(2), tab, tab, const((nh, L, L)), const((L, w)), const((L, w)),
                  const((1, w)), const((w, w)), state],
        out_specs=[pl.BlockSpec((1, tc, w), lambda bi, ti: (bi, ti, 0)), state],
        out_shape=[jax.ShapeDtypeStruct((b, t, w), F32), jax.ShapeDtypeStruct((b, w, w), F32)],
        scratch_shapes=[pltpu.VMEM((w, w), F32)],
        compiler_params=_cparams("parallel", "arbitrary"),
        name="retention",
    )(ret, ret, ret, cos, sin, dec, qd, kd, gl, bd, s0)


def _block_diag_state(st, nh, hd):
    b = st.shape[0]
    eye = jnp.eye(nh, dtype=st.dtype)
    return jnp.einsum('bhde,hg->bhdge', st, eye).reshape(b, nh * hd, nh * hd)


def _diag_blocks(s, nh, hd):
    b = s.shape[0]
    s5 = s.reshape(b, nh, hd, nh, hd)
    return jnp.stack([s5[:, h, :, h, :] for h in range(nh)], axis=1)


def _head_mean_matrix(w, hd):
    hid = np.arange(w) // hd
    return jnp.asarray((hid[:, None] == hid[None, :]).astype(np.float32) / hd, dtype=BF16)


def _headnorm(x, g_mat, gain):
    ms = _dot_split(x * x, g_mat)
    return x * lax.rsqrt(ms + EPS) * gain


def _merge_kernel(x_ref, sb_ref, ret_ref, gate_ref, qm_ref, mk_ref, mv_ref, gsb_ref, gret_ref, gmo_ref,
                  msb_ref, mrt_ref, wo_ref, xlo_ref, xhi_ref, *, nh_mem, hd):
    tm = x_ref.shape[1]
    wsb = sb_ref.shape[2]
    wrt = ret_ref.shape[2]
    qm = qm_ref[0]
    mkb = mk_ref[0].astype(BF16)
    mvb = mv_ref[0].astype(BF16)
    head_q = lax.broadcasted_iota(I32, qm.shape, 1) // hd
    head_m = lax.broadcasted_iota(I32, mkb.shape, 1) // hd
    mem_o = jnp.zeros(qm.shape, F32)
    for h in range(nh_mem):
        s = _dot_nt(jnp.where(head_q == h, qm, jnp.zeros_like(qm)), mkb)
        e = jnp.exp(s - jnp.max(s, axis=-1, keepdims=True))
        p = e / jnp.sum(e, axis=-1, keepdims=True)
        mem_o = mem_o + _dot(p.astype(BF16), jnp.where(head_m == h, mvb, jnp.zeros_like(mvb)))
    sb_n = _headnorm(sb_ref[0], msb_ref[...], gsb_ref[...])
    gate = gate_ref[0]
    ret_n = _headnorm(ret_ref[0], mrt_ref[...], gret_ref[...]) * (gate / (1.0 + jnp.exp(-gate)))
    mem_n = _headnorm(mem_o, mrt_ref[...], gmo_ref[...])
    y = (_dot(sb_n.astype(BF16), wo_ref[0:wsb, :])
         + _dot(ret_n.astype(BF16), wo_ref[wsb:wsb + wrt, :])
         + _dot(mem_n.astype(BF16), wo_ref[wsb + wrt:, :]))
    x1 = x_ref[0] + y
    half = x1.shape[1] // 2
    xlo_ref[0] = x1[:, :half]
    xhi_ref[0] = x1[:, half:]


def _merge(x, sb_o, ret_o, ret, qm, mk, mv, g_sb, g_ret, g_mo, wo_bf16, hd, tm):
    b, t, d = x.shape
    wsb = sb_o.shape[2]
    wrt = ret_o.shape[2]
    nmem = mk.shape[1]
    half = d // 2
    tok = lambda wd, j=0: pl.BlockSpec((1, tm, wd), lambda bi, ti: (bi, ti, j))
    const = lambda shape: pl.BlockSpec(shape, lambda bi, ti: (0,) * len(shape))
    memspec = pl.BlockSpec((1, nmem, wrt), lambda bi, ti: (bi, 0, 0))
    return pl.pallas_call(
        functools.partial(_merge_kernel, nh_mem=wrt // hd, hd=hd),
        grid=(b, t // tm),
        in_specs=[tok(d), tok(wsb), tok(wrt), tok(wrt, 3), tok(wrt), memspec, memspec,
                  const((1, wsb)), const((1, wrt)), const((1, wrt)),
                  const((wsb, wsb)), const((wrt, wrt)), const((d, d))],
        out_specs=[tok(half), tok(half)],
        out_shape=[jax.ShapeDtypeStruct((b, t, half), F32)] * 2,
        compiler_params=_cparams("parallel", "parallel"),
        name="merge",
    )(x, sb_o, ret_o, ret, qm, mk, mv, g_sb.reshape(1, wsb), g_ret.reshape(1, wrt), g_mo.reshape(1, wrt),
      _head_mean_matrix(wsb, hd), _head_mean_matrix(wrt, hd), wo_bf16)


def _topk_rows(s, k):
    r = s.shape[0]
    rid = lax.broadcasted_iota(I32, s.shape, 0)
    vals, idxs = [], []
    for _ in range(k):
        m = jnp.max(s, axis=0, keepdims=True)
        i = jnp.min(jnp.where(s == m, rid, r), axis=0, keepdims=True)
        vals.append(m)
        idxs.append(i)
        s = jnp.where(rid == i, -jnp.inf, s)
    return jnp.concatenate(vals, axis=0), jnp.concatenate(idxs, axis=0)


def _take_rows(tbl, idx):
    out = jnp.zeros(idx.shape, tbl.dtype)
    for a in range(tbl.shape[0]):
        out = out + jnp.where(idx == a, tbl[a:a + 1, :], 0)
    return out


def _peer_score_kernel(xlo_ref, xhi_ref, g_ref, wq_ref, sk_ref, hlo_ref, hhi_ref, eid_ref, gw_ref, hb_scr,
                       eid_scr, gw_scr, *, topk, nkeys):
    head = pl.program_id(1)

    @pl.when(head == 0)
    def _():
        x = jnp.concatenate([xlo_ref[...], xhi_ref[...]], axis=1)
        ms = jnp.mean(x * x, axis=-1, keepdims=True)
        hn = x * lax.rsqrt(ms + EPS) * g_ref[...]
        half = hn.shape[1] // 2
        hlo_ref[...] = hn[:, :half]
        hhi_ref[...] = hn[:, half:]
        hb_scr[...] = hn.astype(BF16)

    qb = _dot(hb_scr[...], wq_ref[...]).astype(BF16)
    hq = qb.shape[1] // 2
    s0 = _dot_nt(sk_ref[0], qb[:, :hq])
    s1 = _dot_nt(sk_ref[1], qb[:, hq:])
    sv0, si0 = _topk_rows(s0, topk)
    sv1, si1 = _topk_rows(s1, topk)
    cand = jnp.concatenate([sv0[a:a + 1, :] + sv1 for a in range(topk)], axis=0)
    cv, ci = _topk_rows(cand, topk)
    e1 = _take_rows(si0, ci // topk)
    e2 = _take_rows(si1, ci % topk)
    rows = pl.ds(pl.multiple_of(head * topk, topk), topk)
    eid_scr[rows, :] = ((e1 * nkeys + e2) * HALF_ROWS).astype(F32)
    e = jnp.exp(cv - cv[0:1, :])
    gw_scr[rows, :] = e / jnp.sum(e, axis=0, keepdims=True)

    @pl.when(head == pl.num_programs(1) - 1)
    def _():
        eid_ref[...] = eid_scr[...].T.astype(I32)
        gw_ref[...] = gw_scr[...].T


def _peer_score(xlo, xhi, g, wq_bf16, sk_bf16, tm):
    n, half = xlo.shape
    d = 2 * half
    nkeys, khalf = sk_bf16.shape[1], sk_bf16.shape[2]
    nheads = wq_bf16.shape[1] // (2 * khalf)
    k = PEER_TOPK
    xs = pl.BlockSpec((tm, half), lambda i, h: (i, 0))
    return pl.pallas_call(
        functools.partial(_peer_score_kernel, topk=k, nkeys=nkeys),
        grid=(n // tm, nheads),
        in_specs=[xs, xs, pl.BlockSpec((1, d), lambda i, h: (0, 0)),
                  pl.BlockSpec((d, 2 * khalf), lambda i, h: (0, h)),
                  pl.BlockSpec((2, nkeys, khalf), lambda i, h: (0, 0, 0))],
        out_specs=[xs, xs, pl.BlockSpec((tm, nheads * k), lambda i, h: (i, 0)),
                   pl.BlockSpec((tm, nheads * k), lambda i, h: (i, 0))],
        out_shape=[jax.ShapeDtypeStruct((n, half), F32), jax.ShapeDtypeStruct((n, half), F32),
                   jax.ShapeDtypeStruct((n, nheads * k), I32), jax.ShapeDtypeStruct((n, nheads * k), F32)],
        scratch_shapes=[pltpu.VMEM((tm, d), BF16), pltpu.VMEM((nheads * k, tm), F32),
                        pltpu.VMEM((nheads * k, tm), F32)],
        compiler_params=_cparams("parallel", "arbitrary"),
        name="peer_score",
    )(xlo, xhi, g.reshape(1, d), wq_bf16, sk_bf16)


def _pack_table(tab):
    e, d = tab.shape
    bits = lax.bitcast_convert_type(tab.astype(BF16), jnp.uint16).astype(jnp.uint32)
    word = bits[:, :d // 2] | (bits[:, d // 2:] << 16)
    return lax.bitcast_convert_type(word, I32).reshape(e * HALF_ROWS, LANES)


def _unpack(wd):
    lo = lax.bitcast_convert_type(wd << 16, F32)
    hi = lax.bitcast_convert_type(wd & (-65536), F32)
    return lo, hi


def _erf(x):
    return lax.erf(x)


PEER_REDUCE_TOKENS = 8


def _peer_u_kernel(eid_ref, hlo_ref, hhi_ref, gw_ref, tab_ref, w_ref, p_scr, qh_scr, ql_scr, act_scr, *, tg, npairs):
    def token(t, carry):
        r4 = pl.multiple_of(t * HALF_ROWS, HALF_ROWS)
        hl = hlo_ref[pl.ds(r4, HALF_ROWS), :]
        hh = hhi_ref[pl.ds(r4, HALF_ROWS), :]
        for j in range(npairs):
            e4 = pl.multiple_of(eid_ref[t, j], HALF_ROWS)
            lo, hi = _unpack(tab_ref[pl.ds(e4, HALF_ROWS), :])
            p_scr[j * HALF_ROWS:(j + 1) * HALF_ROWS, :] = lo * hl + hi * hh
        q = p_scr[pl.ds(0, npairs, stride=HALF_ROWS), :]
        for s in range(1, HALF_ROWS):
            q = q + p_scr[pl.ds(s, npairs, stride=HALF_ROWS), :]
        qh, ql = _split_bf16(q)
        rows = pl.ds(pl.multiple_of(t * npairs, npairs), npairs)
        qh_scr[rows, :] = qh
        ql_scr[rows, :] = ql
        return carry

    lax.fori_loop(0, tg, token, 0)

    ones = jnp.ones((LANES, LANES), BF16)
    nt = PEER_REDUCE_TOKENS
    eye = (lax.broadcasted_iota(I32, (npairs, LANES), 0) == lax.broadcasted_iota(I32, (npairs, LANES), 1)).astype(F32)

    def reduce_chunk(ci, carry):
        rows = pl.ds(pl.multiple_of(ci * (nt * npairs), nt * npairs), nt * npairs)
        r = _dot(qh_scr[rows, :], ones) + _dot(ql_scr[rows, :], ones)
        out = [jnp.sum(r[k * npairs:(k + 1) * npairs, :] * eye, axis=0, keepdims=True) for k in range(nt)]
        act_scr[pl.ds(pl.multiple_of(ci * nt, nt), nt), :] = jnp.concatenate(out, axis=0)
        return carry

    lax.fori_loop(0, tg // nt, reduce_chunk, 0)
    act = act_scr[...]
    gelu = 0.5 * act * (1.0 + _erf(act * (2.0 ** -0.5)))
    w_ref[...] = gw_ref[...] * gelu


def _peer_u(eid, hlo, hhi, gw, tab_packed, tg):
    n, npairs = eid.shape
    assert npairs == LANES and tg % PEER_REDUCE_TOKENS == 0
    rows = tg * HALF_ROWS
    hspec = pl.BlockSpec((rows, LANES), lambda i: (i, 0))
    tspec = pl.BlockSpec((tg, npairs), lambda i: (i, 0))
    return pl.pallas_call(
        functools.partial(_peer_u_kernel, tg=tg, npairs=npairs),
        grid=(n // tg,),
        in_specs=[pl.BlockSpec((tg, npairs), lambda i: (i, 0), memory_space=pltpu.SMEM),
                  hspec, hspec, tspec,
                  pl.BlockSpec(memory_space=pltpu.VMEM)],
        out_specs=tspec,
        out_shape=jax.ShapeDtypeStruct((n, npairs), F32),
        scratch_shapes=[pltpu.VMEM((npairs * HALF_ROWS, LANES), F32),
                        pltpu.VMEM((tg * npairs, LANES), BF16), pltpu.VMEM((tg * npairs, LANES), BF16),
                        pltpu.VMEM((tg, npairs), F32)],
        compiler_params=_cparams("arbitrary"),
        name="peer_u",
    )(eid, hlo.reshape(n * HALF_ROWS, LANES), hhi.reshape(n * HALF_ROWS, LANES), gw, tab_packed)


def _peer_v_kernel(eid_ref, w_ref, tab_ref, olo_ref, ohi_ref, *, tg, npairs, nacc):
    def token(t, carry):
        acc_lo = [jnp.zeros((HALF_ROWS, LANES), F32) for _ in range(nacc)]
        acc_hi = [jnp.zeros((HALF_ROWS, LANES), F32) for _ in range(nacc)]
        for j in range(npairs):
            e4 = pl.multiple_of(eid_ref[t, j], HALF_ROWS)
            lo, hi = _unpack(tab_ref[pl.ds(e4, HALF_ROWS), :])
            wt = w_ref[t, j]
            acc_lo[j % nacc] = acc_lo[j % nacc] + wt * lo
            acc_hi[j % nacc] = acc_hi[j % nacc] + wt * hi
        r4 = pl.multiple_of(t * HALF_ROWS, HALF_ROWS)
        olo_ref[pl.ds(r4, HALF_ROWS), :] = functools.reduce(lambda a, b: a + b, acc_lo)
        ohi_ref[pl.ds(r4, HALF_ROWS), :] = functools.reduce(lambda a, b: a + b, acc_hi)
        return carry

    lax.fori_loop(0, tg, token, 0)


def _peer_v(eid, w, tab_packed, tg):
    n, npairs = eid.shape
    rows = tg * HALF_ROWS
    ospec = pl.BlockSpec((rows, LANES), lambda i: (i, 0))
    sspec = pl.BlockSpec((tg, npairs), lambda i: (i, 0), memory_space=pltpu.SMEM)
    olo, ohi = pl.pallas_call(
        functools.partial(_peer_v_kernel, tg=tg, npairs=npairs, nacc=2),
        grid=(n // tg,),
        in_specs=[sspec, sspec, pl.BlockSpec(memory_space=pltpu.VMEM)],
        out_specs=[ospec, ospec],
        out_shape=[jax.ShapeDtypeStruct((n * HALF_ROWS, LANES), F32)] * 2,
        compiler_params=_cparams("arbitrary"),
        name="peer_v",
    )(eid, w, tab_packed)
    return olo.reshape(n, HALF_ROWS * LANES), ohi.reshape(n, HALF_ROWS * LANES)


def _final_kernel(xlo_ref, xhi_ref, plo_ref, phi_ref, g_ref, y_ref):
    x = jnp.concatenate([xlo_ref[...] + plo_ref[...], xhi_ref[...] + phi_ref[...]], axis=1)
    ms = jnp.mean(x * x, axis=-1, keepdims=True)
    y_ref[...] = x * lax.rsqrt(ms + EPS) * g_ref[...]


def _final(xlo, xhi, plo, phi, g, tm):
    n, half = xlo.shape
    hs = pl.BlockSpec((tm, half), lambda i: (i, 0))
    return pl.pallas_call(
        _final_kernel,
        grid=(n // tm,),
        in_specs=[hs, hs, hs, hs, pl.BlockSpec((1, 2 * half), lambda i: (0, 0))],
        out_specs=pl.BlockSpec((tm, 2 * half), lambda i: (i, 0)),
        out_shape=jax.ShapeDtypeStruct((n, 2 * half), F32),
        compiler_params=_cparams("parallel"),
        name="final_norm",
    )(xlo, xhi, plo, phi, g.reshape(1, 2 * half))


def _row_tile(n, pref):
    t = min(pref, n)
    assert n % t == 0, (n, t)
    return t


def _stream(x, w_in_bf, g_mix, dims):
    b, t, d = x.shape
    wsb, wrt, wmm = dims
    scale = 64 ** -0.5
    o = 3 * wsb
    outs = [(0, wsb, scale, BF16),
            (wsb, 2 * wsb, 1.0, F32), (2 * wsb, o, 1.0, F32),
            (wsb, 2 * wsb, 1.0, BF16), (2 * wsb, o, 1.0, BF16),
            (o, o + 4 * wrt, 1.0, F32),
            (o + 4 * wrt, o + 4 * wrt + wmm, scale, BF16)]
    res = _norm_proj(x.reshape(b * t, d), g_mix, w_in_bf, outs, _row_tile(b * t, 512))
    return [r.reshape(b, t, r.shape[1]) for r in res]


def _peer_and_final(xlo, xhi, g_ffn, wq_bf, sk_bf, u_packed, v_packed, g_final):
    n = xlo.shape[0]
    hlo, hhi, eid, gw = _peer_score(xlo, xhi, g_ffn, wq_bf, sk_bf, _row_tile(n, 256))
    tg = _row_tile(n, 128)
    w = _peer_u(eid, hlo, hhi, gw, u_packed, tg)
    plo, phi = _peer_v(eid, w, v_packed, tg)
    return _final(xlo, xhi, plo, phi, g_final, _row_tile(n, 512))


def kernel(x_prompt, x_sample, mem_prompt, cache_sb_k, cache_sb_v, state_ret, cache_mem_k, cache_mem_v,
           g_mix, w_in, w_out, g_sb_out, g_ret_out, g_mem_out, g_mem, w_mem_kv,
           g_ffn, w_peer_q, peer_sub_keys, peer_u, peer_v, g_final):
    depth = w_in.shape[0]
    assert depth == 1, "single-layer step"
    bp, tp, d = x_prompt.shape
    bs, ts, _ = x_sample.shape
    _, _, past, sb_heads, hd = cache_sb_k.shape
    ret_heads = state_ret.shape[2]
    mem_heads = cache_mem_k.shape[3]
    nmem = mem_prompt.shape[1]
    wsb, wrt, wmm = sb_heads * hd, ret_heads * hd, mem_heads * hd
    assert hd == 64 and wrt == wmm and d == 2 * HALF_ROWS * LANES
    l = 0
    w_in_bf = w_in[l].astype(BF16)
    wo_bf = w_out[l].astype(BF16)
    dims = (wsb, wrt, wmm)

    q_p, k_p, v_p, kb_p, vb_p, ret_p, qm_p = _stream(x_prompt, w_in_bf, g_mix[l], dims)
    sb_p = _sb_prompt(q_p, kb_p, vb_p, hd)
    zeros_state = jnp.zeros((bp, wrt, wrt), F32)
    ro_p, st_p = _retention(ret_p, jnp.arange(tp, dtype=I32), zeros_state, ret_heads, hd, _row_tile(tp, 512))
    mk_p, mv_p = _norm_proj(mem_prompt.reshape(bp * nmem, d), g_mem[l], w_mem_kv[l].astype(BF16),
                            [(0, wmm, 1.0, F32), (wmm, 2 * wmm, 1.0, F32)], _row_tile(bp * nmem, 512))
    mk_p = mk_p.reshape(bp, nmem, wmm)
    mv_p = mv_p.reshape(bp, nmem, wmm)
    xlo_p, xhi_p = _merge(x_prompt, sb_p, ro_p, ret_p, qm_p, mk_p, mv_p, g_sb_out[l], g_ret_out[l], g_mem_out[l],
                          wo_bf, hd, _row_tile(tp, 256))

    q_s, k_s, v_s, kb_s, vb_s, ret_s, qm_s = _stream(x_sample, w_in_bf, g_mix[l], dims)
    sb_s = _sb_sample(q_s, kb_s, vb_s, cache_sb_k[l].reshape(bs, past, wsb), cache_sb_v[l].reshape(bs, past, wsb), hd)
    s0 = _block_diag_state(state_ret[l].astype(F32), ret_heads, hd)
    ro_s, st_s = _retention(ret_s, past + jnp.arange(ts, dtype=I32), s0, ret_heads, hd, _row_tile(ts, 512))
    xlo_s, xhi_s = _merge(x_sample, sb_s, ro_s, ret_s, qm_s, cache_mem_k[l].reshape(bs, nmem, wmm),
                          cache_mem_v[l].reshape(bs, nmem, wmm), g_sb_out[l], g_ret_out[l], g_mem_out[l],
                          wo_bf, hd, _row_tile(ts, 256))

    wq_bf = w_peer_q[l].astype(BF16)
    sk_bf = peer_sub_keys[l].astype(BF16)
    u_packed = _pack_table(peer_u[l])
    v_packed = _pack_table(peer_v[l])
    half = d // 2
    y_p = _peer_and_final(xlo_p.reshape(bp * tp, half), xhi_p.reshape(bp * tp, half), g_ffn[l], wq_bf, sk_bf,
                          u_packed, v_packed, g_final)
    y_s = _peer_and_final(xlo_s.reshape(bs * ts, half), xhi_s.reshape(bs * ts, half), g_ffn[l], wq_bf, sk_bf,
                          u_packed, v_packed, g_final)

    return (y_p.reshape(bp, tp, d), y_s.reshape(bs, ts, d),
            k_p.reshape(1, bp, tp, sb_heads, hd), v_p.reshape(1, bp, tp, sb_heads, hd),
            _diag_blocks(st_p, ret_heads, hd)[None],
            mk_p.reshape(1, bp, nmem, mem_heads, hd), mv_p.reshape(1, bp, nmem, mem_heads, hd),
            k_s.reshape(1, bs, ts, sb_heads, hd), v_s.reshape(1, bs, ts, sb_heads, hd),
            _diag_blocks(st_s, ret_heads, hd)[None].astype(state_ret.dtype))
```

```python
import functools

import numpy as np
import jax
import jax.numpy as jnp
from jax import lax
from jax.experimental import pallas as pl
from jax.experimental.pallas import tpu as pltpu

F32 = jnp.float32
BF16 = jnp.bfloat16
I32 = jnp.int32

EPS = 1e-6
ROPE_BASE = 10000.0
PEER_TOPK = 16
RET_CHUNK = 64
LANES = 128
ROW_TILES = 8
HALF_ROWS = 4
VMEM_LIMIT = 56 * 1024 * 1024
SB_DEAD_LOG = -105.0


def _cparams(*sem):
    return pltpu.CompilerParams(dimension_semantics=sem, vmem_limit_bytes=VMEM_LIMIT)


def _dot(a, b):
    return jnp.dot(a, b, preferred_element_type=F32)


def _dot_nt(a, b):
    return lax.dot_general(a, b, (((1,), (1,)), ((), ())), preferred_element_type=F32)


def _split_bf16(x):
    hi = x.astype(BF16)
    lo = (x - hi.astype(F32)).astype(BF16)
    return hi, lo


def _dot_split(x, m_bf16):
    hi, lo = _split_bf16(x)
    return _dot(hi, m_bf16) + _dot(lo, m_bf16)


def _norm_proj_kernel(x_ref, g_ref, w_ref, *out_refs, cols):
    x = x_ref[...]
    ms = jnp.mean(x * x, axis=-1, keepdims=True)
    h = x * lax.rsqrt(ms + EPS) * g_ref[...]
    p = _dot(h.astype(BF16), w_ref[...])
    for ref, (lo, hi, scale) in zip(out_refs, cols):
        v = p[:, lo:hi]
        if scale != 1.0:
            v = v * scale
        ref[...] = v.astype(ref.dtype)


def _norm_proj(x, g, w_bf16, outs, tm):
    n, d = x.shape
    wcols = w_bf16.shape[1]
    out_shape = [jax.ShapeDtypeStruct((n, hi - lo), dt) for lo, hi, _, dt in outs]
    out_specs = [pl.BlockSpec((tm, hi - lo), lambda i: (i, 0)) for lo, hi, _, _ in outs]
    return pl.pallas_call(
        functools.partial(_norm_proj_kernel, cols=tuple((lo, hi, sc) for lo, hi, sc, _ in outs)),
        grid=(n // tm,),
        in_specs=[pl.BlockSpec((tm, d), lambda i: (i, 0)),
                  pl.BlockSpec((1, d), lambda i: (0, 0)),
                  pl.BlockSpec((d, wcols), lambda i: (0, 0))],
        out_specs=out_specs,
        out_shape=out_shape,
        compiler_params=_cparams("parallel"),
        name="norm_proj",
    )(x, g.reshape(1, d), w_bf16)


def _suffix_matrix(tk):
    j = np.arange(2 * tk)[:, None]
    s = np.arange(2 * tk)[None, :]
    return jnp.asarray(((j // tk == s // tk) & (j >= s)).astype(np.float32), dtype=BF16)


def _sb_step(q2, kb, vb, u, carry, tk, hd, valid):
    c0, c1, acc = carry
    tq = q2.shape[0]
    lane = lax.broadcasted_iota(I32, kb.shape, 1)
    first = lane < hd
    zero = jnp.zeros_like(kb)
    kk = jnp.concatenate([jnp.where(first, kb, zero), jnp.where(first, zero, kb)], axis=0)
    vv = jnp.concatenate([jnp.where(first, vb, zero), jnp.where(first, zero, vb)], axis=0)
    z = _dot_nt(q2, kk)
    sp = jnp.log(1.0 + jnp.exp(-jnp.abs(z)))
    lk = jnp.minimum(-z, 0.0) - sp
    if valid is not None:
        lk = jnp.where(valid, lk, 0.0)
    m = _dot_split(lk, u)
    c = jnp.concatenate([jnp.broadcast_to(c0, (tq, tk)), jnp.broadcast_to(c1, (tq, tk))], axis=1)
    a = jnp.exp(z + c + m)
    if valid is not None:
        a = jnp.where(valid, a, 0.0)
    acc = acc + _dot(a.astype(BF16), vv)
    c0 = c0 + jnp.sum(lk[:, :tk], axis=-1, keepdims=True)
    c1 = c1 + jnp.sum(lk[:, tk:], axis=-1, keepdims=True)
    return c0, c1, acc


def _sb_alive(c0, c1):
    return jnp.max(jnp.maximum(c0, c1)) > SB_DEAD_LOG


def _sb_far_blocks(q2, block, nblk, u, carry, tk, hd):
    def cond(state):
        return jnp.logical_and(state[0] < nblk, state[1])

    def body(state):
        i, _, c0, c1, acc = state
        kb, vb = block(i)
        c0, c1, acc = _sb_step(q2, kb, vb, u, (c0, c1, acc), tk, hd, None)
        return i + 1, _sb_alive(c0, c1), c0, c1, acc

    state = lax.while_loop(cond, body, (jnp.int32(0), _sb_alive(carry[0], carry[1])) + tuple(carry))
    return state[2:]


def _sb_prompt_kernel(q_ref, k_ref, v_ref, u_ref, o_ref, *, tq, tk, hd):
    q0 = pl.program_id(2) * tq
    q2 = q_ref[0]
    u = u_ref[...]
    row = lax.broadcasted_iota(I32, (tq, 2 * tk), 0)
    koff = lax.broadcasted_iota(I32, (tq, 2 * tk), 1) & (tk - 1)
    carry = (jnp.zeros((tq, 1), F32), jnp.zeros((tq, 1), F32), jnp.zeros((tq, LANES), F32))
    nd = tq // tk
    for d in range(nd):
        off = (nd - 1 - d) * tk
        valid = (koff + off) < row
        ks = pl.multiple_of(q0 + off, tk)
        carry = _sb_step(q2, k_ref[0, pl.ds(ks, tk), :], v_ref[0, pl.ds(ks, tk), :], u, carry, tk, hd, valid)

    def block(i):
        ks = pl.multiple_of(q0 - (i + 1) * tk, tk)
        return k_ref[0, pl.ds(ks, tk), :], v_ref[0, pl.ds(ks, tk), :]

    carry = _sb_far_blocks(q2, block, q0 // tk, u, carry, tk, hd)
    o_ref[0] = carry[2]


def _sb_prompt(q, k, v, hd, tq=256, tk=128):
    b, t, w = q.shape
    return pl.pallas_call(
        functools.partial(_sb_prompt_kernel, tq=tq, tk=tk, hd=hd),
        grid=(b, w // LANES, t // tq),
        in_specs=[pl.BlockSpec((1, tq, LANES), lambda bi, hp, qi: (bi, qi, hp)),
                  pl.BlockSpec((1, t, LANES), lambda bi, hp, qi: (bi, 0, hp)),
                  pl.BlockSpec((1, t, LANES), lambda bi, hp, qi: (bi, 0, hp)),
                  pl.BlockSpec((2 * tk, 2 * tk), lambda bi, hp, qi: (0, 0))],
        out_specs=pl.BlockSpec((1, tq, LANES), lambda bi, hp, qi: (bi, qi, hp)),
        out_shape=jax.ShapeDtypeStruct((b, t, w), F32),
        compiler_params=_cparams("parallel", "parallel", "arbitrary"),
        name="sb_prompt",
    )(q, k, v, _suffix_matrix(tk))


def _sb_sample_kernel(q_ref, kn_ref, vn_ref, kc_ref, vc_ref, un_ref, u_ref, o_ref, *, ts, tk, hd, past):
    q2 = q_ref[0]
    row = lax.broadcasted_iota(I32, (ts, 2 * ts), 0)
    koff = lax.broadcasted_iota(I32, (ts, 2 * ts), 1) & (ts - 1)
    carry = (jnp.zeros((ts, 1), F32), jnp.zeros((ts, 1), F32), jnp.zeros((ts, LANES), F32))
    carry = _sb_step(q2, kn_ref[0], vn_ref[0], un_ref[...], carry, ts, hd, koff < row)
    u = u_ref[...]

    def block(i):
        ks = pl.multiple_of(past - (i + 1) * tk, tk)
        return kc_ref[0, pl.ds(ks, tk), :].astype(BF16), vc_ref[0, pl.ds(ks, tk), :].astype(BF16)

    carry = _sb_far_blocks(q2, block, past // tk, u, carry, tk, hd)
    o_ref[0] = carry[2]


def _sb_sample(q, kn, vn, kc, vc, hd, tk=128):
    b, ts, w = q.shape
    past = kc.shape[1]
    new_spec = pl.BlockSpec((1, ts, LANES), lambda bi, hp: (bi, 0, hp))
    cache_spec = pl.BlockSpec((1, past, LANES), lambda bi, hp: (bi, 0, hp))
    return pl.pallas_call(
        functools.partial(_sb_sample_kernel, ts=ts, tk=tk, hd=hd, past=past),
        grid=(b, w // LANES),
        in_specs=[new_spec, new_spec, new_spec, cache_spec, cache_spec,
                  pl.BlockSpec((2 * ts, 2 * ts), lambda bi, hp: (0, 0)),
                  pl.BlockSpec((2 * tk, 2 * tk), lambda bi, hp: (0, 0))],
        out_specs=new_spec,
        out_shape=jax.ShapeDtypeStruct((b, ts, w), F32),
        compiler_params=_cparams("parallel", "parallel"),
        name="sb_sample",
    )(q, kn, vn, kc, vc, _suffix_matrix(ts), _suffix_matrix(tk))


def _ret_kernel(q_ref, k_ref, v_ref, cos_ref, sin_ref, dec_ref, qd_ref, kd_ref, gl_ref, bd_ref, s0_ref,
                o_ref, st_ref, s_scr, *, tc, nh, hd, k_scale):
    t = pl.program_id(1)
    w = nh * hd
    L = RET_CHUNK

    @pl.when(t == 0)
    def _():
        s_scr[...] = s0_ref[0]

    lane = lax.broadcasted_iota(I32, (L, w), 1)
    first_half = (lane & (hd - 1)) < hd // 2
    head = lane // hd

    def rope(x, cos, sin):
        swapped = jnp.where(first_half, pltpu.roll(x, w - hd // 2, 1), pltpu.roll(x, hd // 2, 1))
        return x * cos + swapped * sin

    s = s_scr[...]
    for ci in range(tc // L):
        sl = slice(ci * L, (ci + 1) * L)
        cos = cos_ref[sl, :]
        sin = sin_ref[sl, :]
        q = rope(q_ref[0, sl, :], cos, sin)
        k = rope(k_ref[0, sl, :], cos, sin) * k_scale
        qb = q.astype(BF16)
        kb = k.astype(BF16)
        vb = v_ref[0, sl, :].astype(BF16)
        zero = jnp.zeros_like(qb)
        out = _dot(qb, s.astype(BF16)) * qd_ref[...]
        for h in range(nh):
            mh = head == h
            sc = _dot_nt(jnp.where(mh, qb, zero), kb)
            p = (sc * dec_ref[h]).astype(BF16)
            out = out + _dot(p, jnp.where(mh, vb, zero))
        o_ref[0, sl, :] = out
        kd = (k * kd_ref[...]).astype(BF16)
        outer = lax.dot_general(kd, vb, (((0,), (0,)), ((), ())), preferred_element_type=F32)
        s = gl_ref[...] * s + bd_ref[...] * outer
    s_scr[...] = s

    @pl.when(t == pl.num_programs(1) - 1)
    def _():
        st_ref[0] = s


def _retention(ret, pos, s0, nh, hd, tc):
    b, t, _ = ret.shape
    w = nh * hd
    L = RET_CHUNK
    half = hd // 2
    freqs = ROPE_BASE ** (-jnp.arange(half, dtype=F32) / half)
    ang = pos.astype(F32)[:, None] * freqs[None, :]
    cos = jnp.tile(jnp.cos(ang), (1, 2 * nh))
    sin = jnp.tile(jnp.concatenate([-jnp.sin(ang), jnp.sin(ang)], axis=1), (1, nh))
    lg = jnp.log1p(-jnp.exp2(-5.0 - jnp.arange(nh, dtype=F32)))
    idx = jnp.arange(L, dtype=F32)
    diff = idx[:, None] - idx[None, :]
    causal = diff >= 0
    dec = jnp.where(causal[None], jnp.exp(jnp.where(causal, diff, 0.0)[None] * lg[:, None, None]), 0.0)
    lg_lane = jnp.repeat(lg, hd)[None, :]
    qd = jnp.exp((idx + 1.0)[:, None] * lg_lane)
    kd = jnp.exp((L - 1.0 - idx)[:, None] * lg_lane)
    gl = jnp.exp(L * lg_lane)
    hid = jnp.arange(w) // hd
    bd = (hid[:, None] == hid[None, :]).astype(F32)
    tok = lambda j: pl.BlockSpec((1, tc, w), lambda bi, ti: (bi, ti, j))
    tab = pl.BlockSpec((tc, w), lambda bi, ti: (ti, 0))
    const = lambda shape: pl.BlockSpec(shape, lambda bi, ti: (0,) * len(shape))
    state = pl.BlockSpec((1, w, w), lambda bi, ti: (bi, 0, 0))
    return pl.pallas_call(
        functools.partial(_ret_kernel, tc=tc, nh=nh, hd=hd, k_scale=hd ** -0.5),
        grid=(b, t // tc),
        in_specs=[tok(0), tok(1), tok(2), tab, tab, const((nh, L, L)), const((L, w)), const((L, w)),
                  const((1, w)), const((w, w)), state],
        out_specs=[pl.BlockSpec((1, tc, w), lambda bi, ti: (bi, ti, 0)), state],
        out_shape=[jax.ShapeDtypeStruct((b, t, w), F32), jax.ShapeDtypeStruct((b, w, w), F32)],
        scratch_shapes=[pltpu.VMEM((w, w), F32)],
        compiler_params=_cparams("parallel", "arbitrary"),
        name="retention",
    )(ret, ret, ret, cos, sin, dec, qd, kd, gl, bd, s0)


def _block_diag_state(st, nh, hd):
    b = st.shape[0]
    eye = jnp.eye(nh, dtype=st.dtype)
    return jnp.einsum('bhde,hg->bhdge', st, eye).reshape(b, nh * hd, nh * hd)


def _diag_blocks(s, nh, hd):
    b = s.shape[0]
    s5 = s.reshape(b, nh, hd, nh, hd)
    return jnp.stack([s5[:, h, :, h, :] for h in range(nh)], axis=1)


def _head_mean_matrix(w, hd):
    hid = np.arange(w) // hd
    return jnp.asarray((hid[:, None] == hid[None, :]).astype(np.float32) / hd, dtype=BF16)


def _headnorm(x, g_mat, gain):
    ms = _dot_split(x * x, g_mat)
    return x * lax.rsqrt(ms + EPS) * gain


def _merge_kernel(x_ref, sb_ref, ret_ref, gate_ref, qm_ref, mk_ref, mv_ref, gsb_ref, gret_ref, gmo_ref,
                  msb_ref, mrt_ref, wo_ref, x1_ref, *, nh_mem, hd):
    wsb = sb_ref.shape[2]
    wrt = ret_ref.shape[2]
    qm = qm_ref[0]
    mkb = mk_ref[0].astype(BF16)
    mvb = mv_ref[0].astype(BF16)
    head_q = lax.broadcasted_iota(I32, qm.shape, 1) // hd
    head_m = lax.broadcasted_iota(I32, mkb.shape, 1) // hd
    mem_o = jnp.zeros(qm.shape, F32)
    for h in range(nh_mem):
        s = _dot_nt(jnp.where(head_q == h, qm, jnp.zeros_like(qm)), mkb)
        e = jnp.exp(s - jnp.max(s, axis=-1, keepdims=True))
        p = e / jnp.sum(e, axis=-1, keepdims=True)
        mem_o = mem_o + _dot(p.astype(BF16), jnp.where(head_m == h, mvb, jnp.zeros_like(mvb)))
    sb_n = _headnorm(sb_ref[0], msb_ref[...], gsb_ref[...])
    gate = gate_ref[0]
    ret_n = _headnorm(ret_ref[0], mrt_ref[...], gret_ref[...]) * (gate / (1.0 + jnp.exp(-gate)))
    mem_n = _headnorm(mem_o, mrt_ref[...], gmo_ref[...])
    y = (_dot(sb_n.astype(BF16), wo_ref[0:wsb, :])
         + _dot(ret_n.astype(BF16), wo_ref[wsb:wsb + wrt, :])
         + _dot(mem_n.astype(BF16), wo_ref[wsb + wrt:, :]))
    x1_ref[0] = x_ref[0] + y


def _merge(x, sb_o, ret_o, ret, qm, mk, mv, g_sb, g_ret, g_mo, wo_bf16, hd, tm):
    b, t, d = x.shape
    wsb = sb_o.shape[2]
    wrt = ret_o.shape[2]
    nmem = mk.shape[1]
    tok = lambda wd, j=0: pl.BlockSpec((1, tm, wd), lambda bi, ti: (bi, ti, j))
    const = lambda shape: pl.BlockSpec(shape, lambda bi, ti: (0,) * len(shape))
    memspec = pl.BlockSpec((1, nmem, wrt), lambda bi, ti: (bi, 0, 0))
    return pl.pallas_call(
        functools.partial(_merge_kernel, nh_mem=wrt // hd, hd=hd),
        grid=(b, t // tm),
        in_specs=[tok(d), tok(wsb), tok(wrt), tok(wrt, 3), tok(wrt), memspec, memspec,
                  const((1, wsb)), const((1, wrt)), const((1, wrt)),
                  const((wsb, wsb)), const((wrt, wrt)), const((d, d))],
        out_specs=tok(d),
        out_shape=jax.ShapeDtypeStruct((b, t, d), F32),
        compiler_params=_cparams("parallel", "parallel"),
        name="merge",
    )(x, sb_o, ret_o, ret, qm, mk, mv, g_sb.reshape(1, wsb), g_ret.reshape(1, wrt), g_mo.reshape(1, wrt),
      _head_mean_matrix(wsb, hd), _head_mean_matrix(wrt, hd), wo_bf16)


def _topk_rows(s, k, rid=None):
    if rid is None:
        rid = lax.broadcasted_iota(I32, s.shape, 0)
    big = jnp.int32(2 ** 30)
    vals, idxs = [], []
    for _ in range(k):
        m = jnp.max(s, axis=0, keepdims=True)
        i = jnp.min(jnp.where(s == m, rid, big), axis=0, keepdims=True)
        vals.append(m)
        idxs.append(i)
        s = jnp.where(rid == i, -jnp.inf, s)
    return jnp.concatenate(vals, axis=0), jnp.concatenate(idxs, axis=0)


def _staircase(k):
    pairs = [(a, b) for a in range(k) for b in range(k) if (a + 1) * (b + 1) <= k]
    pad = -len(pairs) % 8
    a = np.array([p[0] for p in pairs] + [0] * pad, np.int32)[:, None]
    b = np.array([p[1] for p in pairs] + [0] * pad, np.int32)[:, None]
    flat = np.array([p[0] * k + p[1] for p in pairs] + [k * k + i for i in range(pad)], np.int32)[:, None]
    return a, b, flat


def _take_rows(tbl, idx):
    out = jnp.zeros(idx.shape, tbl.dtype)
    for a in range(tbl.shape[0]):
        out = out + jnp.where(idx == a, tbl[a:a + 1, :], 0)
    return out


def _peer_score_kernel(x_ref, g_ref, wq_ref, sk_ref, ca_ref, cb_ref, cf_ref, h_ref, eid_ref, gw_ref, hb_scr,
                       eid_scr, gw_scr, *, topk, nkeys):
    head = pl.program_id(1)

    @pl.when(head == 0)
    def _():
        x = x_ref[...]
        ms = jnp.mean(x * x, axis=-1, keepdims=True)
        hn = x * lax.rsqrt(ms + EPS) * g_ref[...]
        for r in range(ROW_TILES):
            h_ref[pl.ds(r, hn.shape[0], stride=ROW_TILES), :] = hn[:, r * LANES:(r + 1) * LANES]
        hb_scr[...] = hn.astype(BF16)

    qb = _dot(hb_scr[...], wq_ref[...]).astype(BF16)
    hq = qb.shape[1] // 2
    s0 = _dot_nt(sk_ref[0], qb[:, :hq])
    s1 = _dot_nt(sk_ref[1], qb[:, hq:])
    sv0, si0 = _topk_rows(s0, topk)
    sv1, si1 = _topk_rows(s1, topk)
    flat = cf_ref[...]
    cand = jnp.where(flat < topk * topk, _take_rows(sv0, ca_ref[...]) + _take_rows(sv1, cb_ref[...]), -jnp.inf)
    cv, ci = _topk_rows(cand, topk, flat)
    e1 = _take_rows(si0, ci // topk)
    e2 = _take_rows(si1, ci % topk)
    rows = pl.ds(pl.multiple_of(head * topk, topk), topk)
    eid_scr[rows, :] = ((e1 * nkeys + e2) * HALF_ROWS).astype(F32)
    e = jnp.exp(cv - cv[0:1, :])
    gw_scr[rows, :] = e / jnp.sum(e, axis=0, keepdims=True)

    @pl.when(head == pl.num_programs(1) - 1)
    def _():
        eid_ref[...] = eid_scr[...].T.astype(I32)
        gw_ref[...] = gw_scr[...].T


def _peer_score(x, g, wq_bf16, sk_bf16, tm):
    n, d = x.shape
    nkeys, khalf = sk_bf16.shape[1], sk_bf16.shape[2]
    nheads = wq_bf16.shape[1] // (2 * khalf)
    k = PEER_TOPK
    ca, cb, cf = (jnp.asarray(np.broadcast_to(c, (c.shape[0], tm))) for c in _staircase(k))
    nc = ca.shape[0]
    xs = pl.BlockSpec((tm, d), lambda i, h: (i, 0))
    cs = pl.BlockSpec((nc, tm), lambda i, h: (0, 0))
    return pl.pallas_call(
        functools.partial(_peer_score_kernel, topk=k, nkeys=nkeys),
        grid=(n // tm, nheads),
        in_specs=[xs, pl.BlockSpec((1, d), lambda i, h: (0, 0)),
                  pl.BlockSpec((d, 2 * khalf), lambda i, h: (0, h)),
                  pl.BlockSpec((2, nkeys, khalf), lambda i, h: (0, 0, 0)), cs, cs, cs],
        out_specs=[pl.BlockSpec((tm * ROW_TILES, LANES), lambda i, h: (i, 0)),
                   pl.BlockSpec((tm, nheads * k), lambda i, h: (i, 0)),
                   pl.BlockSpec((tm, nheads * k), lambda i, h: (i, 0))],
        out_shape=[jax.ShapeDtypeStruct((n * ROW_TILES, LANES), F32),
                   jax.ShapeDtypeStruct((n, nheads * k), I32), jax.ShapeDtypeStruct((n, nheads * k), F32)],
        scratch_shapes=[pltpu.VMEM((tm, d), BF16), pltpu.VMEM((nheads * k, tm), F32),
                        pltpu.VMEM((nheads * k, tm), F32)],
        compiler_params=_cparams("parallel", "arbitrary"),
        name="peer_score",
    )(x, g.reshape(1, d), wq_bf16, sk_bf16, ca, cb, cf)


def _pack_table(tab):
    e, d = tab.shape
    bits = lax.bitcast_convert_type(tab.astype(BF16), jnp.uint16).astype(jnp.uint32)
    bits = bits.reshape(e, d // (2 * LANES), 2, LANES)
    word = bits[:, :, 0, :] | (bits[:, :, 1, :] << 16)
    return lax.bitcast_convert_type(word, I32).reshape(e * HALF_ROWS, LANES)


def _pair_slot_matrix(npairs):
    c = np.arange(npairs * ROW_TILES)[None, :]
    j = np.arange(npairs)[:, None]
    return (c // ROW_TILES == j).astype(np.float32)


def _erf(x):
    return lax.erf(x)


def _gather_rows(eid_ref, t, tab_ref, g_scr, npairs):
    for j in range(npairs):
        e4 = pl.multiple_of(eid_ref[t, j], HALF_ROWS)
        g_scr[j * HALF_ROWS:(j + 1) * HALF_ROWS, :] = tab_ref[pl.ds(e4, HALF_ROWS), :]


def _gather_pipeline(eid_ref, tab_ref, bufs, tg, npairs, consume):
    a0, a1, b0, b1 = bufs
    _gather_rows(eid_ref, 0, tab_ref, a0, npairs)
    _gather_rows(eid_ref, 1, tab_ref, a1, npairs)

    def body(i, carry):
        t0 = 4 * i
        consume(t0, a0)
        consume(t0 + 1, a1)
        _gather_rows(eid_ref, t0 + 2, tab_ref, b0, npairs)
        _gather_rows(eid_ref, t0 + 3, tab_ref, b1, npairs)
        consume(t0 + 2, b0)
        consume(t0 + 3, b1)
        _gather_rows(eid_ref, jnp.minimum(t0 + 4, tg - 1), tab_ref, a0, npairs)
        _gather_rows(eid_ref, jnp.minimum(t0 + 5, tg - 1), tab_ref, a1, npairs)
        return carry

    lax.fori_loop(0, tg // 4, body, 0)


def _slot_mask(rows):
    shape = (rows, LANES * ROW_TILES)
    return (lax.broadcasted_iota(I32, shape, 1) & (ROW_TILES - 1)) == (lax.broadcasted_iota(I32, shape, 0) & (ROW_TILES - 1))


def _peer_u_kernel(eid_ref, h_ref, gw_ref, tab_ref, col_ref, w_ref, a0, a1, b0, b1, s_scr, *, tg, npairs):
    mask = _slot_mask(2 * ROW_TILES)

    def consume(t, g_scr):
        g = pltpu.bitcast(g_scr[...], BF16)
        h8 = h_ref[pl.ds(pl.multiple_of(t * ROW_TILES, ROW_TILES), ROW_TILES), :]
        hh, hl = _split_bf16(h8)
        r16 = _dot_nt(jnp.concatenate([hh, hl], axis=0), g)
        s_scr[pl.ds(t, 1), :] = jnp.sum(jnp.where(mask, r16, 0.0), axis=0, keepdims=True)

    _gather_pipeline(eid_ref, tab_ref, (a0, a1, b0, b1), tg, npairs, consume)
    act = _dot_split(s_scr[...], col_ref[...])
    gelu = 0.5 * act * (1.0 + _erf(act * (2.0 ** -0.5)))
    w_ref[...] = gw_ref[...] * gelu


def _peer_u(eid, h, gw, tab_packed, tg):
    n, npairs = eid.shape
    assert npairs == LANES and h.shape == (n * ROW_TILES, LANES) and tg % 4 == 0
    tspec = pl.BlockSpec((tg, npairs), lambda i: (i, 0))
    gshape = pltpu.VMEM((npairs * HALF_ROWS, LANES), I32)
    return pl.pallas_call(
        functools.partial(_peer_u_kernel, tg=tg, npairs=npairs),
        grid=(n // tg,),
        in_specs=[pl.BlockSpec((tg, npairs), lambda i: (i, 0), memory_space=pltpu.SMEM),
                  pl.BlockSpec((tg * ROW_TILES, LANES), lambda i: (i, 0)), tspec,
                  pl.BlockSpec(memory_space=pltpu.VMEM),
                  pl.BlockSpec((npairs * ROW_TILES, npairs), lambda i: (0, 0))],
        out_specs=tspec,
        out_shape=jax.ShapeDtypeStruct((n, npairs), F32),
        scratch_shapes=[gshape] * 4 + [pltpu.VMEM((tg, npairs * ROW_TILES), F32)],
        compiler_params=_cparams("arbitrary"),
        name="peer_u",
    )(eid, h, gw, tab_packed, jnp.asarray(_pair_slot_matrix(npairs).T, dtype=BF16))


def _peer_v_kernel(eid_ref, w_ref, tab_ref, exp_ref, o_ref, a0, a1, b0, b1, wx_scr, *, tg, npairs):
    whi, wlo = _split_bf16(w_ref[...])
    wx_scr[0] = _dot(whi, exp_ref[...])
    wx_scr[1] = _dot(wlo, exp_ref[...])
    mask = _slot_mask(ROW_TILES)
    shape = (ROW_TILES, npairs * ROW_TILES)

    def consume(t, g_scr):
        g = pltpu.bitcast(g_scr[...], BF16)
        rows = [jnp.where(mask, jnp.broadcast_to(wx_scr[k, pl.ds(t, 1), :], shape), 0.0) for k in range(2)]
        res = _dot(jnp.concatenate(rows, axis=0).astype(BF16), g)
        out = res[:ROW_TILES, :] + res[ROW_TILES:, :]
        o_ref[pl.ds(pl.multiple_of(t * ROW_TILES, ROW_TILES), ROW_TILES), :] = out

    _gather_pipeline(eid_ref, tab_ref, (a0, a1, b0, b1), tg, npairs, consume)


def _peer_v(eid, w, tab_packed, tg):
    n, npairs = eid.shape
    assert tg % 2 == 0
    gshape = pltpu.VMEM((npairs * HALF_ROWS, LANES), I32)
    return pl.pallas_call(
        functools.partial(_peer_v_kernel, tg=tg, npairs=npairs),
        grid=(n // tg,),
        in_specs=[pl.BlockSpec((tg, npairs), lambda i: (i, 0), memory_space=pltpu.SMEM),
                  pl.BlockSpec((tg, npairs), lambda i: (i, 0)),
                  pl.BlockSpec(memory_space=pltpu.VMEM),
                  pl.BlockSpec((npairs, npairs * ROW_TILES), lambda i: (0, 0))],
        out_specs=pl.BlockSpec((tg * ROW_TILES, LANES), lambda i: (i, 0)),
        out_shape=jax.ShapeDtypeStruct((n * ROW_TILES, LANES), F32),
        scratch_shapes=[gshape] * 4 + [pltpu.VMEM((2, tg, npairs * ROW_TILES), F32)],
        compiler_params=_cparams("arbitrary"),
        name="peer_v",
    )(eid, w, tab_packed, jnp.asarray(_pair_slot_matrix(npairs), dtype=BF16))


def _final_kernel(x_ref, p_ref, g_ref, y_ref):
    tm = x_ref.shape[0]
    p = jnp.concatenate([p_ref[pl.ds(r, tm, stride=ROW_TILES), :] for r in range(ROW_TILES)], axis=1)
    x = x_ref[...] + p
    ms = jnp.mean(x * x, axis=-1, keepdims=True)
    y_ref[...] = x * lax.rsqrt(ms + EPS) * g_ref[...]


def _final(x, p, g, tm):
    n, d = x.shape
    xs = pl.BlockSpec((tm, d), lambda i: (i, 0))
    return pl.pallas_call(
        _final_kernel,
        grid=(n // tm,),
        in_specs=[xs, pl.BlockSpec((tm * ROW_TILES, LANES), lambda i: (i, 0)), pl.BlockSpec((1, d), lambda i: (0, 0))],
        out_specs=xs,
        out_shape=jax.ShapeDtypeStruct((n, d), F32),
        compiler_params=_cparams("parallel"),
        name="final_norm",
    )(x, p, g.reshape(1, d))


def _row_tile(n, pref):
    t = min(pref, n)
    assert n % t == 0, (n, t)
    return t


def _stream(x, w_in_bf, g_mix, dims):
    b, t, d = x.shape
    wsb, wrt, wmm = dims
    scale = 64 ** -0.5
    o = 3 * wsb
    outs = [(0, wsb, scale, BF16),
            (wsb, 2 * wsb, 1.0, F32), (2 * wsb, o, 1.0, F32),
            (wsb, 2 * wsb, 1.0, BF16), (2 * wsb, o, 1.0, BF16),
            (o, o + 4 * wrt, 1.0, F32),
            (o + 4 * wrt, o + 4 * wrt + wmm, scale, BF16)]
    res = _norm_proj(x.reshape(b * t, d), g_mix, w_in_bf, outs, _row_tile(b * t, 512))
    return [r.reshape(b, t, r.shape[1]) for r in res]


def _peer_and_final(x1, g_ffn, wq_bf, sk_bf, u_packed, v_packed, g_final):
    n = x1.shape[0]
    h, eid, gw = _peer_score(x1, g_ffn, wq_bf, sk_bf, _row_tile(n, 512))
    tg = _row_tile(n, 128)
    w = _peer_u(eid, h, gw, u_packed, tg)
    p = _peer_v(eid, w, v_packed, tg)
    return _final(x1, p, g_final, _row_tile(n, 512))


def kernel(x_prompt, x_sample, mem_prompt, cache_sb_k, cache_sb_v, state_ret, cache_mem_k, cache_mem_v,
           g_mix, w_in, w_out, g_sb_out, g_ret_out, g_mem_out, g_mem, w_mem_kv,
           g_ffn, w_peer_q, peer_sub_keys, peer_u, peer_v, g_final):
    depth = w_in.shape[0]
    assert depth == 1, "single-layer step"
    bp, tp, d = x_prompt.shape
    bs, ts, _ = x_sample.shape
    _, _, past, sb_heads, hd = cache_sb_k.shape
    ret_heads = state_ret.shape[2]
    mem_heads = cache_mem_k.shape[3]
    nmem = mem_prompt.shape[1]
    wsb, wrt, wmm = sb_heads * hd, ret_heads * hd, mem_heads * hd
    assert hd == 64 and wrt == wmm and d == 2 * HALF_ROWS * LANES
    l = 0
    w_in_bf = w_in[l].astype(BF16)
    wo_bf = w_out[l].astype(BF16)
    dims = (wsb, wrt, wmm)

    q_p, k_p, v_p, kb_p, vb_p, ret_p, qm_p = _stream(x_prompt, w_in_bf, g_mix[l], dims)
    sb_p = _sb_prompt(q_p, kb_p, vb_p, hd)
    zeros_state = jnp.zeros((bp, wrt, wrt), F32)
    ro_p, st_p = _retention(ret_p, jnp.arange(tp, dtype=I32), zeros_state, ret_heads, hd, _row_tile(tp, 512))
    mk_p, mv_p = _norm_proj(mem_prompt.reshape(bp * nmem, d), g_mem[l], w_mem_kv[l].astype(BF16),
                            [(0, wmm, 1.0, F32), (wmm, 2 * wmm, 1.0, F32)], _row_tile(bp * nmem, 512))
    mk_p = mk_p.reshape(bp, nmem, wmm)
    mv_p = mv_p.reshape(bp, nmem, wmm)
    x1_p = _merge(x_prompt, sb_p, ro_p, ret_p, qm_p, mk_p, mv_p, g_sb_out[l], g_ret_out[l], g_mem_out[l],
                  wo_bf, hd, _row_tile(tp, 256))

    q_s, k_s, v_s, kb_s, vb_s, ret_s, qm_s = _stream(x_sample, w_in_bf, g_mix[l], dims)
    sb_s = _sb_sample(q_s, kb_s, vb_s, cache_sb_k[l].reshape(bs, past, wsb), cache_sb_v[l].reshape(bs, past, wsb), hd)
    s0 = _block_diag_state(state_ret[l].astype(F32), ret_heads, hd)
    ro_s, st_s = _retention(ret_s, past + jnp.arange(ts, dtype=I32), s0, ret_heads, hd, _row_tile(ts, 512))
    x1_s = _merge(x_sample, sb_s, ro_s, ret_s, qm_s, cache_mem_k[l].reshape(bs, nmem, wmm),
                  cache_mem_v[l].reshape(bs, nmem, wmm), g_sb_out[l], g_ret_out[l], g_mem_out[l],
                  wo_bf, hd, _row_tile(ts, 256))

    wq_bf = w_peer_q[l].astype(BF16)
    sk_bf = peer_sub_keys[l].astype(BF16)
    u_packed = _pack_table(peer_u[l])
    v_packed = _pack_table(peer_v[l])
    y_p = _peer_and_final(x1_p.reshape(bp * tp, d), g_ffn[l], wq_bf, sk_bf, u_packed, v_packed, g_final)
    y_s = _peer_and_final(x1_s.reshape(bs * ts, d), g_ffn[l], wq_bf, sk_bf, u_packed, v_packed, g_final)

    return (y_p.reshape(bp, tp, d), y_s.reshape(bs, ts, d),
            k_p.reshape(1, bp, tp, sb_heads, hd), v_p.reshape(1, bp, tp, sb_heads, hd),
            _diag_blocks(st_p, ret_heads, hd)[None],
            mk_p.reshape(1, bp, nmem, mem_heads, hd), mv_p.reshape(1, bp, nmem, mem_heads, hd),
            k_s.reshape(1, bs, ts, sb_heads, hd), v_s.reshape(1, bs, ts, sb_heads, hd),
            _diag_blocks(st_s, ret_heads, hd)[None].astype(state_ret.dtype))
```

```python
import functools

import numpy as np
import jax
import jax.numpy as jnp
from jax import lax
from jax.experimental import pallas as pl
from jax.experimental.pallas import tpu as pltpu

F32 = jnp.float32
BF16 = jnp.bfloat16
I32 = jnp.int32

EPS = 1e-6
ROPE_BASE = 10000.0
PEER_TOPK = 16
RET_CHUNK = 64
LANES = 128
ROW_TILES = 8
HALF_ROWS = 4
VMEM_LIMIT = 56 * 1024 * 1024
SB_DEAD_LOG = -105.0


def _cparams(*sem):
    return pltpu.CompilerParams(dimension_semantics=sem, vmem_limit_bytes=VMEM_LIMIT)


def _dot(a, b):
    return jnp.dot(a, b, preferred_element_type=F32)


def _dot_nt(a, b):
    return lax.dot_general(a, b, (((1,), (1,)), ((), ())), preferred_element_type=F32)


def _split_bf16(x):
    hi = x.astype(BF16)
    lo = (x - hi.astype(F32)).astype(BF16)
    return hi, lo


def _dot_split(x, m_bf16):
    hi, lo = _split_bf16(x)
    return _dot(hi, m_bf16) + _dot(lo, m_bf16)


def _norm_proj_kernel(x_ref, g_ref, w_ref, *out_refs, cols):
    x = x_ref[...]
    ms = jnp.mean(x * x, axis=-1, keepdims=True)
    h = x * lax.rsqrt(ms + EPS) * g_ref[...]
    p = _dot(h.astype(BF16), w_ref[...])
    tm = x.shape[0]
    for ref, (lo, hi, scale, heads) in zip(out_refs, cols):
        v = p[:, lo:hi]
        if scale != 1.0:
            v = v * scale
        if heads:
            hd = (hi - lo) // heads
            for h in range(heads):
                ref[pl.ds(h, tm, stride=heads), :] = v[:, h * hd:(h + 1) * hd].astype(ref.dtype)
        else:
            ref[...] = v.astype(ref.dtype)


def _norm_proj(x, g, w_bf16, outs, tm):
    n, d = x.shape
    wcols = w_bf16.shape[1]
    shape = lambda rows, lo, hi, hds: (rows * hds, (hi - lo) // hds) if hds else (rows, hi - lo)
    out_shape = [jax.ShapeDtypeStruct(shape(n, lo, hi, hds), dt) for lo, hi, _, dt, hds in outs]
    out_specs = [pl.BlockSpec(shape(tm, lo, hi, hds), lambda i: (i, 0)) for lo, hi, _, _, hds in outs]
    return pl.pallas_call(
        functools.partial(_norm_proj_kernel, cols=tuple((lo, hi, sc, hds) for lo, hi, sc, _, hds in outs)),
        grid=(n // tm,),
        in_specs=[pl.BlockSpec((tm, d), lambda i: (i, 0)),
                  pl.BlockSpec((1, d), lambda i: (0, 0)),
                  pl.BlockSpec((d, wcols), lambda i: (0, 0))],
        out_specs=out_specs,
        out_shape=out_shape,
        compiler_params=_cparams("parallel"),
        name="norm_proj",
    )(x, g.reshape(1, d), w_bf16)


def _suffix_matrix(tk):
    j = np.arange(2 * tk)[:, None]
    s = np.arange(2 * tk)[None, :]
    return jnp.asarray(((j // tk == s // tk) & (j >= s)).astype(np.float32), dtype=BF16)


def _sb_step(q2, kb, vb, u, carry, tk, hd, valid):
    c0, c1, acc = carry
    tq = q2.shape[0]
    lane = lax.broadcasted_iota(I32, kb.shape, 1)
    first = lane < hd
    zero = jnp.zeros_like(kb)
    kk = jnp.concatenate([jnp.where(first, kb, zero), jnp.where(first, zero, kb)], axis=0)
    vv = jnp.concatenate([jnp.where(first, vb, zero), jnp.where(first, zero, vb)], axis=0)
    z = _dot_nt(q2, kk)
    sp = jnp.log(1.0 + jnp.exp(-jnp.abs(z)))
    lk = jnp.minimum(-z, 0.0) - sp
    if valid is not None:
        lk = jnp.where(valid, lk, 0.0)
    m = _dot_split(lk, u)
    c = jnp.concatenate([jnp.broadcast_to(c0, (tq, tk)), jnp.broadcast_to(c1, (tq, tk))], axis=1)
    a = jnp.exp(z + c + m)
    if valid is not None:
        a = jnp.where(valid, a, 0.0)
    acc = acc + _dot(a.astype(BF16), vv)
    c0 = c0 + jnp.sum(lk[:, :tk], axis=-1, keepdims=True)
    c1 = c1 + jnp.sum(lk[:, tk:], axis=-1, keepdims=True)
    return c0, c1, acc


def _sb_alive(c0, c1):
    return jnp.max(jnp.maximum(c0, c1)) > SB_DEAD_LOG


def _sb_far_blocks(q2, block, nblk, u, carry, tk, hd):
    def cond(state):
        return jnp.logical_and(state[0] < nblk, state[1])

    def body(state):
        i, _, c0, c1, acc = state
        kb, vb = block(i)
        c0, c1, acc = _sb_step(q2, kb, vb, u, (c0, c1, acc), tk, hd, None)
        return i + 1, _sb_alive(c0, c1), c0, c1, acc

    state = lax.while_loop(cond, body, (jnp.int32(0), _sb_alive(carry[0], carry[1])) + tuple(carry))
    return state[2:]


def _sb_prompt_kernel(q_ref, k_ref, v_ref, u_ref, o_ref, *, tq, tk, hd):
    q0 = pl.program_id(2) * tq
    q2 = q_ref[0]
    u = u_ref[...]
    row = lax.broadcasted_iota(I32, (tq, 2 * tk), 0)
    koff = lax.broadcasted_iota(I32, (tq, 2 * tk), 1) & (tk - 1)
    carry = (jnp.zeros((tq, 1), F32), jnp.zeros((tq, 1), F32), jnp.zeros((tq, LANES), F32))
    nd = tq // tk
    for d in range(nd):
        off = (nd - 1 - d) * tk
        valid = (koff + off) < row
        ks = pl.multiple_of(q0 + off, tk)
        carry = _sb_step(q2, k_ref[0, pl.ds(ks, tk), :], v_ref[0, pl.ds(ks, tk), :], u, carry, tk, hd, valid)

    def block(i):
        ks = pl.multiple_of(q0 - (i + 1) * tk, tk)
        return k_ref[0, pl.ds(ks, tk), :], v_ref[0, pl.ds(ks, tk), :]

    carry = _sb_far_blocks(q2, block, q0 // tk, u, carry, tk, hd)
    o_ref[0] = carry[2]


def _sb_prompt(q, k, v, hd, tq=256, tk=128):
    b, t, w = q.shape
    return pl.pallas_call(
        functools.partial(_sb_prompt_kernel, tq=tq, tk=tk, hd=hd),
        grid=(b, w // LANES, t // tq),
        in_specs=[pl.BlockSpec((1, tq, LANES), lambda bi, hp, qi: (bi, qi, hp)),
                  pl.BlockSpec((1, t, LANES), lambda bi, hp, qi: (bi, 0, hp)),
                  pl.BlockSpec((1, t, LANES), lambda bi, hp, qi: (bi, 0, hp)),
                  pl.BlockSpec((2 * tk, 2 * tk), lambda bi, hp, qi: (0, 0))],
        out_specs=pl.BlockSpec((1, tq, LANES), lambda bi, hp, qi: (bi, qi, hp)),
        out_shape=jax.ShapeDtypeStruct((b, t, w), F32),
        compiler_params=_cparams("parallel", "parallel", "arbitrary"),
        name="sb_prompt",
    )(q, k, v, _suffix_matrix(tk))


def _sb_sample_kernel(q_ref, kn_ref, vn_ref, kc_ref, vc_ref, un_ref, u_ref, o_ref, *, ts, tk, hd, past):
    q2 = q_ref[0]
    row = lax.broadcasted_iota(I32, (ts, 2 * ts), 0)
    koff = lax.broadcasted_iota(I32, (ts, 2 * ts), 1) & (ts - 1)
    carry = (jnp.zeros((ts, 1), F32), jnp.zeros((ts, 1), F32), jnp.zeros((ts, LANES), F32))
    carry = _sb_step(q2, kn_ref[0], vn_ref[0], un_ref[...], carry, ts, hd, koff < row)
    u = u_ref[...]

    def block(i):
        ks = pl.multiple_of(past - (i + 1) * tk, tk)
        return kc_ref[0, pl.ds(ks, tk), :].astype(BF16), vc_ref[0, pl.ds(ks, tk), :].astype(BF16)

    carry = _sb_far_blocks(q2, block, past // tk, u, carry, tk, hd)
    o_ref[0] = carry[2]


def _sb_sample(q, kn, vn, kc, vc, hd, tk=128):
    b, ts, w = q.shape
    past = kc.shape[1]
    new_spec = pl.BlockSpec((1, ts, LANES), lambda bi, hp: (bi, 0, hp))
    cache_spec = pl.BlockSpec((1, past, LANES), lambda bi, hp: (bi, 0, hp))
    return pl.pallas_call(
        functools.partial(_sb_sample_kernel, ts=ts, tk=tk, hd=hd, past=past),
        grid=(b, w // LANES),
        in_specs=[new_spec, new_spec, new_spec, cache_spec, cache_spec,
                  pl.BlockSpec((2 * ts, 2 * ts), lambda bi, hp: (0, 0)),
                  pl.BlockSpec((2 * tk, 2 * tk), lambda bi, hp: (0, 0))],
        out_specs=new_spec,
        out_shape=jax.ShapeDtypeStruct((b, ts, w), F32),
        compiler_params=_cparams("parallel", "parallel"),
        name="sb_sample",
    )(q, kn, vn, kc, vc, _suffix_matrix(ts), _suffix_matrix(tk))


def _ret_kernel(q_ref, k_ref, v_ref, cos_ref, sin_ref, dec_ref, qd_ref, kd_ref, gl_ref, bd_ref, s0_ref,
                o_ref, st_ref, s_scr, *, tc, nh, hd, k_scale):
    t = pl.program_id(1)
    w = nh * hd
    L = RET_CHUNK

    @pl.when(t == 0)
    def _():
        s_scr[...] = s0_ref[0]

    lane = lax.broadcasted_iota(I32, (L, w), 1)
    first_half = (lane & (hd - 1)) < hd // 2
    head = lane // hd

    def rope(x, cos, sin):
        swapped = jnp.where(first_half, pltpu.roll(x, w - hd // 2, 1), pltpu.roll(x, hd // 2, 1))
        return x * cos + swapped * sin

    s = s_scr[...]
    for ci in range(tc // L):
        sl = slice(ci * L, (ci + 1) * L)
        cos = cos_ref[sl, :]
        sin = sin_ref[sl, :]
        q = rope(q_ref[0, sl, :], cos, sin)
        k = rope(k_ref[0, sl, :], cos, sin) * k_scale
        qb = q.astype(BF16)
        kb = k.astype(BF16)
        vb = v_ref[0, sl, :].astype(BF16)
        zero = jnp.zeros_like(qb)
        out = _dot(qb, s.astype(BF16)) * qd_ref[...]
        for h in range(nh):
            mh = head == h
            sc = _dot_nt(jnp.where(mh, qb, zero), kb)
            p = (sc * dec_ref[h]).astype(BF16)
            out = out + _dot(p, jnp.where(mh, vb, zero))
        o_ref[0, sl, :] = out
        kd = (k * kd_ref[...]).astype(BF16)
        outer = lax.dot_general(kd, vb, (((0,), (0,)), ((), ())), preferred_element_type=F32)
        s = gl_ref[...] * s + bd_ref[...] * outer
    s_scr[...] = s

    @pl.when(t == pl.num_programs(1) - 1)
    def _():
        st_ref[0] = s


def _retention(ret, pos, s0, nh, hd, tc):
    b, t, _ = ret.shape
    w = nh * hd
    L = RET_CHUNK
    half = hd // 2
    freqs = ROPE_BASE ** (-jnp.arange(half, dtype=F32) / half)
    ang = pos.astype(F32)[:, None] * freqs[None, :]
    cos = jnp.tile(jnp.cos(ang), (1, 2 * nh))
    sin = jnp.tile(jnp.concatenate([-jnp.sin(ang), jnp.sin(ang)], axis=1), (1, nh))
    lg = jnp.log1p(-jnp.exp2(-5.0 - jnp.arange(nh, dtype=F32)))
    idx = jnp.arange(L, dtype=F32)
    diff = idx[:, None] - idx[None, :]
    causal = diff >= 0
    dec = jnp.where(causal[None], jnp.exp(jnp.where(causal, diff, 0.0)[None] * lg[:, None, None]), 0.0)
    lg_lane = jnp.repeat(lg, hd)[None, :]
    qd = jnp.exp((idx + 1.0)[:, None] * lg_lane)
    kd = jnp.exp((L - 1.0 - idx)[:, None] * lg_lane)
    gl = jnp.exp(L * lg_lane)
    hid = jnp.arange(w) // hd
    bd = (hid[:, None] == hid[None, :]).astype(F32)
    tok = lambda j: pl.BlockSpec((1, tc, w), lambda bi, ti: (bi, ti, j))
    tab = pl.BlockSpec((tc, w), lambda bi, ti: (ti, 0))
    const = lambda shape: pl.BlockSpec(shape, lambda bi, ti: (0,) * len(shape))
    state = pl.BlockSpec((1, w, w), lambda bi, ti: (bi, 0, 0))
    return pl.pallas_call(
        functools.partial(_ret_kernel, tc=tc, nh=nh, hd=hd, k_scale=hd ** -0.5),
        grid=(b, t // tc),
        in_specs=[tok(0), tok(1), tok(2), tab, tab, const((nh, L, L)), const((L, w)), const((L, w)),
                  const((1, w)), const((w, w)), state],
        out_specs=[pl.BlockSpec((1, tc, w), lambda bi, ti: (bi, ti, 0)), state],
        out_shape=[jax.ShapeDtypeStruct((b, t, w), F32), jax.ShapeDtypeStruct((b, w, w), F32)],
        scratch_shapes=[pltpu.VMEM((w, w), F32)],
        compiler_params=_cparams("parallel", "arbitrary"),
        name="retention",
    )(ret, ret, ret, cos, sin, dec, qd, kd, gl, bd, s0)


def _block_diag_state(st, nh, hd):
    b = st.shape[0]
    eye = jnp.eye(nh, dtype=st.dtype)
    return jnp.einsum('bhde,hg->bhdge', st, eye).reshape(b, nh * hd, nh * hd)


def _diag_blocks(s, nh, hd):
    b = s.shape[0]
    s5 = s.reshape(b, nh, hd, nh, hd)
    return jnp.stack([s5[:, h, :, h, :] for h in range(nh)], axis=1)


def _head_mean_matrix(w, hd):
    hid = np.arange(w) // hd
    return jnp.asarray((hid[:, None] == hid[None, :]).astype(np.float32) / hd, dtype=BF16)


def _headnorm(x, g_mat, gain):
    ms = _dot_split(x * x, g_mat)
    return x * lax.rsqrt(ms + EPS) * gain


def _merge_kernel(x_ref, sb_ref, ret_ref, gate_ref, qm_ref, mk_ref, mv_ref, gsb_ref, gret_ref, gmo_ref,
                  msb_ref, mrt_ref, wo_ref, x1_ref, *, nh_mem, hd):
    wsb = sb_ref.shape[2]
    wrt = ret_ref.shape[2]
    qm = qm_ref[0]
    mkb = mk_ref[0].astype(BF16)
    mvb = mv_ref[0].astype(BF16)
    head_q = lax.broadcasted_iota(I32, qm.shape, 1) // hd
    head_m = lax.broadcasted_iota(I32, mkb.shape, 1) // hd
    mem_o = jnp.zeros(qm.shape, F32)
    for h in range(nh_mem):
        s = _dot_nt(jnp.where(head_q == h, qm, jnp.zeros_like(qm)), mkb)
        e = jnp.exp(s - jnp.max(s, axis=-1, keepdims=True))
        p = e / jnp.sum(e, axis=-1, keepdims=True)
        mem_o = mem_o + _dot(p.astype(BF16), jnp.where(head_m == h, mvb, jnp.zeros_like(mvb)))
    sb_n = _headnorm(sb_ref[0], msb_ref[...], gsb_ref[...])
    gate = gate_ref[0]
    ret_n = _headnorm(ret_ref[0], mrt_ref[...], gret_ref[...]) * (gate / (1.0 + jnp.exp(-gate)))
    mem_n = _headnorm(mem_o, mrt_ref[...], gmo_ref[...])
    y = (_dot(sb_n.astype(BF16), wo_ref[0:wsb, :])
         + _dot(ret_n.astype(BF16), wo_ref[wsb:wsb + wrt, :])
         + _dot(mem_n.astype(BF16), wo_ref[wsb + wrt:, :]))
    x1_ref[0] = x_ref[0] + y


def _merge(x, sb_o, ret_o, ret, qm, mk, mv, g_sb, g_ret, g_mo, wo_bf16, hd, tm):
    b, t, d = x.shape
    wsb = sb_o.shape[2]
    wrt = ret_o.shape[2]
    nmem = mk.shape[1]
    tok = lambda wd, j=0: pl.BlockSpec((1, tm, wd), lambda bi, ti: (bi, ti, j))
    const = lambda shape: pl.BlockSpec(shape, lambda bi, ti: (0,) * len(shape))
    memspec = pl.BlockSpec((1, nmem, wrt), lambda bi, ti: (bi, 0, 0))
    return pl.pallas_call(
        functools.partial(_merge_kernel, nh_mem=wrt // hd, hd=hd),
        grid=(b, t // tm),
        in_specs=[tok(d), tok(wsb), tok(wrt), tok(wrt, 3), tok(wrt), memspec, memspec,
                  const((1, wsb)), const((1, wrt)), const((1, wrt)),
                  const((wsb, wsb)), const((wrt, wrt)), const((d, d))],
        out_specs=tok(d),
        out_shape=jax.ShapeDtypeStruct((b, t, d), F32),
        compiler_params=_cparams("parallel", "parallel"),
        name="merge",
    )(x, sb_o, ret_o, ret, qm, mk, mv, g_sb.reshape(1, wsb), g_ret.reshape(1, wrt), g_mo.reshape(1, wrt),
      _head_mean_matrix(wsb, hd), _head_mean_matrix(wrt, hd), wo_bf16)


def _topk_rows(s, k, rid=None):
    if rid is None:
        rid = lax.broadcasted_iota(I32, s.shape, 0)
    big = jnp.int32(2 ** 30)
    vals, idxs = [], []
    for _ in range(k):
        m = jnp.max(s, axis=0, keepdims=True)
        i = jnp.min(jnp.where(s == m, rid, big), axis=0, keepdims=True)
        vals.append(m)
        idxs.append(i)
        s = jnp.where(rid == i, -jnp.inf, s)
    return jnp.concatenate(vals, axis=0), jnp.concatenate(idxs, axis=0)


def _staircase(k):
    pairs = [(a, b) for a in range(k) for b in range(k) if (a + 1) * (b + 1) <= k]
    pad = -len(pairs) % 8
    a = np.array([p[0] for p in pairs] + [0] * pad, np.int32)[:, None]
    b = np.array([p[1] for p in pairs] + [0] * pad, np.int32)[:, None]
    flat = np.array([p[0] * k + p[1] for p in pairs] + [k * k + i for i in range(pad)], np.int32)[:, None]
    return a, b, flat


def _take_rows(tbl, idx):
    out = jnp.zeros(idx.shape, tbl.dtype)
    for a in range(tbl.shape[0]):
        out = out + jnp.where(idx == a, tbl[a:a + 1, :], 0)
    return out


def _peer_score_kernel(x_ref, g_ref, wq_ref, sk_ref, ca_ref, cb_ref, cf_ref, h_ref, eid_ref, gw_ref, hb_scr,
                       eid_scr, gw_scr, *, topk, nkeys):
    head = pl.program_id(1)

    @pl.when(head == 0)
    def _():
        x = x_ref[...]
        ms = jnp.mean(x * x, axis=-1, keepdims=True)
        hn = x * lax.rsqrt(ms + EPS) * g_ref[...]
        for r in range(ROW_TILES):
            h_ref[pl.ds(r, hn.shape[0], stride=ROW_TILES), :] = hn[:, r * LANES:(r + 1) * LANES]
        hb_scr[...] = hn.astype(BF16)

    qb = _dot(hb_scr[...], wq_ref[...]).astype(BF16)
    hq = qb.shape[1] // 2
    s0 = _dot_nt(sk_ref[0], qb[:, :hq])
    s1 = _dot_nt(sk_ref[1], qb[:, hq:])
    sv0, si0 = _topk_rows(s0, topk)
    sv1, si1 = _topk_rows(s1, topk)
    flat = cf_ref[...]
    cand = jnp.where(flat < topk * topk, _take_rows(sv0, ca_ref[...]) + _take_rows(sv1, cb_ref[...]), -jnp.inf)
    cv, ci = _topk_rows(cand, topk, flat)
    e1 = _take_rows(si0, ci // topk)
    e2 = _take_rows(si1, ci % topk)
    rows = pl.ds(pl.multiple_of(head * topk, topk), topk)
    eid_scr[rows, :] = ((e1 * nkeys + e2) * HALF_ROWS).astype(F32)
    e = jnp.exp(cv - cv[0:1, :])
    gw_scr[rows, :] = e / jnp.sum(e, axis=0, keepdims=True)

    @pl.when(head == pl.num_programs(1) - 1)
    def _():
        eid_ref[...] = eid_scr[...].T.astype(I32)
        gw_ref[...] = gw_scr[...].T


def _peer_score(x, g, wq_bf16, sk_bf16, tm):
    n, d = x.shape
    nkeys, khalf = sk_bf16.shape[1], sk_bf16.shape[2]
    nheads = wq_bf16.shape[1] // (2 * khalf)
    k = PEER_TOPK
    ca, cb, cf = (jnp.asarray(np.broadcast_to(c, (c.shape[0], tm))) for c in _staircase(k))
    nc = ca.shape[0]
    xs = pl.BlockSpec((tm, d), lambda i, h: (i, 0))
    cs = pl.BlockSpec((nc, tm), lambda i, h: (0, 0))
    return pl.pallas_call(
        functools.partial(_peer_score_kernel, topk=k, nkeys=nkeys),
        grid=(n // tm, nheads),
        in_specs=[xs, pl.BlockSpec((1, d), lambda i, h: (0, 0)),
                  pl.BlockSpec((d, 2 * khalf), lambda i, h: (0, h)),
                  pl.BlockSpec((2, nkeys, khalf), lambda i, h: (0, 0, 0)), cs, cs, cs],
        out_specs=[pl.BlockSpec((tm * ROW_TILES, LANES), lambda i, h: (i, 0)),
                   pl.BlockSpec((tm, nheads * k), lambda i, h: (i, 0)),
                   pl.BlockSpec((tm, nheads * k), lambda i, h: (i, 0))],
        out_shape=[jax.ShapeDtypeStruct((n * ROW_TILES, LANES), F32),
                   jax.ShapeDtypeStruct((n, nheads * k), I32), jax.ShapeDtypeStruct((n, nheads * k), F32)],
        scratch_shapes=[pltpu.VMEM((tm, d), BF16), pltpu.VMEM((nheads * k, tm), F32),
                        pltpu.VMEM((nheads * k, tm), F32)],
        compiler_params=_cparams("parallel", "arbitrary"),
        name="peer_score",
    )(x, g.reshape(1, d), wq_bf16, sk_bf16, ca, cb, cf)


def _pack_table(tab):
    e, d = tab.shape
    t4 = tab.reshape(e, d // (2 * LANES), 2, LANES)
    low = lax.bitcast_convert_type(t4[:, :, 0, :].astype(BF16), jnp.uint16).astype(jnp.uint32)
    odd = lax.bitcast_convert_type(t4[:, :, 1, :], jnp.uint32)
    sign = odd & jnp.uint32(0x80000000)
    mag = (odd & jnp.uint32(0x7FFFFFFF)) + jnp.uint32(0x8000)
    high = jnp.where(mag >= low, (mag - low) >> 16, jnp.uint32(0))
    word = sign | (high << 16) | low
    return lax.bitcast_convert_type(word, I32).reshape(e * HALF_ROWS, LANES)


def _unpack(wd):
    even = lax.bitcast_convert_type(wd << 16, F32)
    odd = lax.bitcast_convert_type(wd, F32)
    return even, odd


def _erf(x):
    return lax.erf(x)


def _token_tiles(ref, t):
    base = pl.multiple_of(t * ROW_TILES, ROW_TILES)
    return ref[pl.ds(base, HALF_ROWS, stride=2), :], ref[pl.ds(base + 1, HALF_ROWS, stride=2), :]


PEER_U_SET = 4


def _peer_u_kernel(eid_ref, h_ref, gw_ref, tab_ref, w_ref, pa, pb, qh_scr, ql_scr, act_scr, *, tg, npairs):
    ones = jnp.ones((LANES, LANES), BF16)
    row = lax.broadcasted_iota(I32, (npairs, LANES), 0)
    eye = (row == lax.broadcasted_iota(I32, (npairs, LANES), 1)).astype(F32)

    def products(t, p_scr, slot):
        he, ho = _token_tiles(h_ref, t)
        for j in range(npairs):
            e4 = pl.multiple_of(eid_ref[t, j], HALF_ROWS)
            even, odd = _unpack(tab_ref[pl.ds(e4, HALF_ROWS), :])
            p_scr[j * HALF_ROWS:(j + 1) * HALF_ROWS, :] = even * he + odd * ho
        q = p_scr[pl.ds(0, npairs, stride=HALF_ROWS), :]
        for s in range(1, HALF_ROWS):
            q = q + p_scr[pl.ds(s, npairs, stride=HALF_ROWS), :]
        qh_scr[slot], ql_scr[slot] = _split_bf16(q)

    def lane_sums(slot, t):
        r = _dot(qh_scr[slot], ones) + _dot(ql_scr[slot], ones)
        act_scr[pl.ds(t, 1), :] = jnp.sum(r * eye, axis=0, keepdims=True)

    ns = PEER_U_SET
    zeros = jnp.zeros((npairs, LANES), BF16)
    for slot in range(ns, 2 * ns):
        qh_scr[slot] = zeros
        ql_scr[slot] = zeros

    def body(i, carry):
        t0 = 2 * ns * i
        for k in range(ns):
            lane_sums(ns + k, jnp.maximum(t0 - ns + k, 0))
        for k in range(ns):
            products(t0 + k, (pa, pb)[k % 2], k)
        for k in range(ns):
            lane_sums(k, t0 + k)
        for k in range(ns):
            products(t0 + ns + k, (pa, pb)[k % 2], ns + k)
        return carry

    lax.fori_loop(0, tg // (2 * ns), body, 0)
    for k in range(ns):
        lane_sums(ns + k, tg - ns + k)
    act = act_scr[...]
    gelu = 0.5 * act * (1.0 + _erf(act * (2.0 ** -0.5)))
    w_ref[...] = gw_ref[...] * gelu


def _peer_u(eid, h, gw, tab_packed, tg):
    n, npairs = eid.shape
    assert npairs == LANES and h.shape == (n * ROW_TILES, LANES) and tg % (2 * PEER_U_SET) == 0
    tspec = pl.BlockSpec((tg, npairs), lambda i: (i, 0))
    pshape = pltpu.VMEM((npairs * HALF_ROWS, LANES), F32)
    qshape = pltpu.VMEM((2 * PEER_U_SET, npairs, LANES), BF16)
    return pl.pallas_call(
        functools.partial(_peer_u_kernel, tg=tg, npairs=npairs),
        grid=(n // tg,),
        in_specs=[pl.BlockSpec((tg, npairs), lambda i: (i, 0), memory_space=pltpu.SMEM),
                  pl.BlockSpec((tg * ROW_TILES, LANES), lambda i: (i, 0)), tspec,
                  pl.BlockSpec(memory_space=pltpu.VMEM)],
        out_specs=tspec,
        out_shape=jax.ShapeDtypeStruct((n, npairs), F32),
        scratch_shapes=[pshape, pshape, qshape, qshape, pltpu.VMEM((tg, npairs), F32)],
        compiler_params=_cparams("arbitrary"),
        name="peer_u",
    )(eid, h, gw, tab_packed)


PEER_V_ACCUMULATORS = 2


def _peer_v_kernel(eid_ref, w_ref, tab_ref, o_ref, *, tg, npairs):
    nacc = PEER_V_ACCUMULATORS

    def token(t):
        acc_e = [jnp.zeros((HALF_ROWS, LANES), F32) for _ in range(nacc)]
        acc_o = [jnp.zeros((HALF_ROWS, LANES), F32) for _ in range(nacc)]
        for j in range(npairs):
            e4 = pl.multiple_of(eid_ref[t, j], HALF_ROWS)
            even, odd = _unpack(tab_ref[pl.ds(e4, HALF_ROWS), :])
            wt = w_ref[t, j]
            acc_e[j % nacc] = acc_e[j % nacc] + wt * even
            acc_o[j % nacc] = acc_o[j % nacc] + wt * odd
        base = pl.multiple_of(t * ROW_TILES, ROW_TILES)
        o_ref[pl.ds(base, HALF_ROWS, stride=2), :] = functools.reduce(lambda a, b: a + b, acc_e)
        o_ref[pl.ds(base + 1, HALF_ROWS, stride=2), :] = functools.reduce(lambda a, b: a + b, acc_o)

    def body(i, carry):
        token(2 * i)
        token(2 * i + 1)
        return carry

    lax.fori_loop(0, tg // 2, body, 0)


def _peer_v(eid, w, tab_packed, tg):
    n, npairs = eid.shape
    assert tg % 2 == 0
    sspec = pl.BlockSpec((tg, npairs), lambda i: (i, 0), memory_space=pltpu.SMEM)
    return pl.pallas_call(
        functools.partial(_peer_v_kernel, tg=tg, npairs=npairs),
        grid=(n // tg,),
        in_specs=[sspec, sspec, pl.BlockSpec(memory_space=pltpu.VMEM)],
        out_specs=pl.BlockSpec((tg * ROW_TILES, LANES), lambda i: (i, 0)),
        out_shape=jax.ShapeDtypeStruct((n * ROW_TILES, LANES), F32),
        compiler_params=_cparams("arbitrary"),
        name="peer_v",
    )(eid, w, tab_packed)


def _final_kernel(x_ref, p_ref, g_ref, y_ref):
    tm = x_ref.shape[0]
    p = jnp.concatenate([p_ref[pl.ds(r, tm, stride=ROW_TILES), :] for r in range(ROW_TILES)], axis=1)
    x = x_ref[...] + p
    ms = jnp.mean(x * x, axis=-1, keepdims=True)
    y_ref[...] = x * lax.rsqrt(ms + EPS) * g_ref[...]


def _final(x, p, g, tm):
    n, d = x.shape
    xs = pl.BlockSpec((tm, d), lambda i: (i, 0))
    return pl.pallas_call(
        _final_kernel,
        grid=(n // tm,),
        in_specs=[xs, pl.BlockSpec((tm * ROW_TILES, LANES), lambda i: (i, 0)), pl.BlockSpec((1, d), lambda i: (0, 0))],
        out_specs=xs,
        out_shape=jax.ShapeDtypeStruct((n, d), F32),
        compiler_params=_cparams("parallel"),
        name="final_norm",
    )(x, p, g.reshape(1, d))


def _row_tile(n, pref):
    t = min(pref, n)
    assert n % t == 0, (n, t)
    return t


def _stream(x, w_in_bf, g_mix, dims):
    b, t, d = x.shape
    wsb, wrt, wmm, hd = dims
    scale = hd ** -0.5
    o = 3 * wsb
    outs = [(0, wsb, scale, BF16, 0),
            (wsb, 2 * wsb, 1.0, F32, wsb // hd), (2 * wsb, o, 1.0, F32, wsb // hd),
            (wsb, 2 * wsb, 1.0, BF16, 0), (2 * wsb, o, 1.0, BF16, 0),
            (o, o + 4 * wrt, 1.0, F32, 0),
            (o + 4 * wrt, o + 4 * wrt + wmm, scale, BF16, 0)]
    res = _norm_proj(x.reshape(b * t, d), g_mix, w_in_bf, outs, _row_tile(b * t, 512))
    return [r.reshape(b, t, r.shape[1]) if r.shape[0] == b * t else r for r in res]


def _peer_and_final(x1, g_ffn, wq_bf, sk_bf, u_packed, v_packed, g_final):
    n = x1.shape[0]
    h, eid, gw = _peer_score(x1, g_ffn, wq_bf, sk_bf, _row_tile(n, 512))
    tg = _row_tile(n, 128)
    w = _peer_u(eid, h, gw, u_packed, tg)
    p = _peer_v(eid, w, v_packed, tg)
    return _final(x1, p, g_final, _row_tile(n, 512))


def kernel(x_prompt, x_sample, mem_prompt, cache_sb_k, cache_sb_v, state_ret, cache_mem_k, cache_mem_v,
           g_mix, w_in, w_out, g_sb_out, g_ret_out, g_mem_out, g_mem, w_mem_kv,
           g_ffn, w_peer_q, peer_sub_keys, peer_u, peer_v, g_final):
    depth = w_in.shape[0]
    assert depth == 1, "single-layer step"
    bp, tp, d = x_prompt.shape
    bs, ts, _ = x_sample.shape
    _, _, past, sb_heads, hd = cache_sb_k.shape
    ret_heads = state_ret.shape[2]
    mem_heads = cache_mem_k.shape[3]
    nmem = mem_prompt.shape[1]
    wsb, wrt, wmm = sb_heads * hd, ret_heads * hd, mem_heads * hd
    assert hd == 64 and wrt == wmm and d == 2 * HALF_ROWS * LANES
    l = 0
    w_in_bf = w_in[l].astype(BF16)
    wo_bf = w_out[l].astype(BF16)
    dims = (wsb, wrt, wmm, hd)

    q_p, k_p, v_p, kb_p, vb_p, ret_p, qm_p = _stream(x_prompt, w_in_bf, g_mix[l], dims)
    sb_p = _sb_prompt(q_p, kb_p, vb_p, hd)
    zeros_state = jnp.zeros((bp, wrt, wrt), F32)
    ro_p, st_p = _retention(ret_p, jnp.arange(tp, dtype=I32), zeros_state, ret_heads, hd, _row_tile(tp, 512))
    mk_p, mv_p = _norm_proj(mem_prompt.reshape(bp * nmem, d), g_mem[l], w_mem_kv[l].astype(BF16),
                            [(0, wmm, 1.0, F32, 0), (wmm, 2 * wmm, 1.0, F32, 0)], _row_tile(bp * nmem, 512))
    mk_p = mk_p.reshape(bp, nmem, wmm)
    mv_p = mv_p.reshape(bp, nmem, wmm)
    x1_p = _merge(x_prompt, sb_p, ro_p, ret_p, qm_p, mk_p, mv_p, g_sb_out[l], g_ret_out[l], g_mem_out[l],
                  wo_bf, hd, _row_tile(tp, 256))

    q_s, k_s, v_s, kb_s, vb_s, ret_s, qm_s = _stream(x_sample, w_in_bf, g_mix[l], dims)
    sb_s = _sb_sample(q_s, kb_s, vb_s, cache_sb_k[l].reshape(bs, past, wsb), cache_sb_v[l].reshape(bs, past, wsb), hd)
    s0 = _block_diag_state(state_ret[l].astype(F32), ret_heads, hd)
    ro_s, st_s = _retention(ret_s, past + jnp.arange(ts, dtype=I32), s0, ret_heads, hd, _row_tile(ts, 512))
    x1_s = _merge(x_sample, sb_s, ro_s, ret_s, qm_s, cache_mem_k[l].reshape(bs, nmem, wmm),
                  cache_mem_v[l].reshape(bs, nmem, wmm), g_sb_out[l], g_ret_out[l], g_mem_out[l],
                  wo_bf, hd, _row_tile(ts, 256))

    wq_bf = w_peer_q[l].astype(BF16)
    sk_bf = peer_sub_keys[l].astype(BF16)
    u_packed = _pack_table(peer_u[l])
    v_packed = _pack_table(peer_v[l])
    y_p = _peer_and_final(x1_p.reshape(bp * tp, d), g_ffn[l], wq_bf, sk_bf, u_packed, v_packed, g_final)
    y_s = _peer_and_final(x1_s.reshape(bs * ts, d), g_ffn[l], wq_bf, sk_bf, u_packed, v_packed, g_final)

    return (y_p.reshape(bp, tp, d), y_s.reshape(bs, ts, d),
            k_p.reshape(1, bp, tp, sb_heads, hd), v_p.reshape(1, bp, tp, sb_heads, hd),
            _diag_blocks(st_p, ret_heads, hd)[None],
            mk_p.reshape(1, bp, nmem, mem_heads, hd), mv_p.reshape(1, bp, nmem, mem_heads, hd),
            k_s.reshape(1, bs, ts, sb_heads, hd), v_s.reshape(1, bs, ts, sb_heads, hd),
            _diag_blocks(st_s, ret_heads, hd)[None].astype(state_ret.dtype))
```

```python
import functools

import numpy as np
import jax
import jax.numpy as jnp
from jax import lax
from jax.experimental import pallas as pl
from jax.experimental.pallas import tpu as pltpu

F32 = jnp.float32
BF16 = jnp.bfloat16
I32 = jnp.int32

EPS = 1e-6
ROPE_BASE = 10000.0
PEER_TOPK = 16
RET_CHUNK = 64
LANES = 128
ROW_TILES = 8
HALF_ROWS = 4
VMEM_LIMIT = 56 * 1024 * 1024
SB_DEAD_LOG = -105.0


def _cparams(*sem):
    return pltpu.CompilerParams(dimension_semantics=sem, vmem_limit_bytes=VMEM_LIMIT)


def _dot(a, b):
    return jnp.dot(a, b, preferred_element_type=F32)


def _dot_nt(a, b):
    return lax.dot_general(a, b, (((1,), (1,)), ((), ())), preferred_element_type=F32)


def _split_bf16(x):
    hi = x.astype(BF16)
    lo = (x - hi.astype(F32)).astype(BF16)
    return hi, lo


def _dot_split(x, m_bf16):
    hi, lo = _split_bf16(x)
    return _dot(hi, m_bf16) + _dot(lo, m_bf16)


def _norm_proj_kernel(x_ref, g_ref, w_ref, *out_refs, cols):
    x = x_ref[...]
    ms = jnp.mean(x * x, axis=-1, keepdims=True)
    h = x * lax.rsqrt(ms + EPS) * g_ref[...]
    p = _dot(h.astype(BF16), w_ref[...])
    tm = x.shape[0]
    for ref, (lo, hi, scale, heads) in zip(out_refs, cols):
        v = p[:, lo:hi]
        if scale != 1.0:
            v = v * scale
        if heads:
            hd = (hi - lo) // heads
            for h in range(heads):
                ref[pl.ds(h, tm, stride=heads), :] = v[:, h * hd:(h + 1) * hd].astype(ref.dtype)
        else:
            ref[...] = v.astype(ref.dtype)


def _norm_proj(x, g, w_bf16, outs, tm):
    n, d = x.shape
    wcols = w_bf16.shape[1]
    shape = lambda rows, lo, hi, hds: (rows * hds, (hi - lo) // hds) if hds else (rows, hi - lo)
    out_shape = [jax.ShapeDtypeStruct(shape(n, lo, hi, hds), dt) for lo, hi, _, dt, hds in outs]
    out_specs = [pl.BlockSpec(shape(tm, lo, hi, hds), lambda i: (i, 0)) for lo, hi, _, _, hds in outs]
    return pl.pallas_call(
        functools.partial(_norm_proj_kernel, cols=tuple((lo, hi, sc, hds) for lo, hi, sc, _, hds in outs)),
        grid=(n // tm,),
        in_specs=[pl.BlockSpec((tm, d), lambda i: (i, 0)),
                  pl.BlockSpec((1, d), lambda i: (0, 0)),
                  pl.BlockSpec((d, wcols), lambda i: (0, 0))],
        out_specs=out_specs,
        out_shape=out_shape,
        compiler_params=_cparams("parallel"),
        name="norm_proj",
    )(x, g.reshape(1, d), w_bf16)


def _suffix_matrix(tk):
    j = np.arange(2 * tk)[:, None]
    s = np.arange(2 * tk)[None, :]
    return jnp.asarray(((j // tk == s // tk) & (j >= s)).astype(np.float32), dtype=BF16)


def _sb_step(q2, kb, vb, u, carry, tk, hd, valid):
    c0, c1, acc = carry
    tq = q2.shape[0]
    lane = lax.broadcasted_iota(I32, kb.shape, 1)
    first = lane < hd
    zero = jnp.zeros_like(kb)
    kk = jnp.concatenate([jnp.where(first, kb, zero), jnp.where(first, zero, kb)], axis=0)
    vv = jnp.concatenate([jnp.where(first, vb, zero), jnp.where(first, zero, vb)], axis=0)
    z = _dot_nt(q2, kk)
    sp = jnp.log(1.0 + jnp.exp(-jnp.abs(z)))
    lk = jnp.minimum(-z, 0.0) - sp
    if valid is not None:
        lk = jnp.where(valid, lk, 0.0)
    m = _dot_split(lk, u)
    c = jnp.concatenate([jnp.broadcast_to(c0, (tq, tk)), jnp.broadcast_to(c1, (tq, tk))], axis=1)
    a = jnp.exp(z + c + m)
    if valid is not None:
        a = jnp.where(valid, a, 0.0)
    acc = acc + _dot(a.astype(BF16), vv)
    c0 = c0 + jnp.sum(lk[:, :tk], axis=-1, keepdims=True)
    c1 = c1 + jnp.sum(lk[:, tk:], axis=-1, keepdims=True)
    return c0, c1, acc


def _sb_alive(carries):
    m = functools.reduce(jnp.maximum, [jnp.maximum(c0, c1) for c0, c1, _ in carries])
    return jnp.max(m) > SB_DEAD_LOG


def _sb_far_blocks(qs, block, nblk, u, carries, tk, hd):
    n = len(qs)
    flatten = lambda cs: tuple(x for c in cs for x in c)
    unflatten = lambda flat: [tuple(flat[3 * p:3 * p + 3]) for p in range(n)]

    def cond(state):
        return jnp.logical_and(state[0] < nblk, state[1])

    def body(state):
        i = state[0]
        kvs = block(i)
        cs = [_sb_step(qs[p], kvs[p][0], kvs[p][1], u, c, tk, hd, None) for p, c in enumerate(unflatten(state[2:]))]
        return (i + 1, _sb_alive(cs)) + flatten(cs)

    state = lax.while_loop(cond, body, (jnp.int32(0), _sb_alive(carries)) + flatten(carries))
    return unflatten(state[2:])


SB_PAIRS_PER_STEP = 2


def _sb_prompt_kernel(q_ref, k_ref, v_ref, u_ref, o_ref, *, tq, tk, hd, npairs):
    q0 = pl.program_id(2) * tq
    lanes = [slice(p * LANES, (p + 1) * LANES) for p in range(npairs)]
    qs = [q_ref[0, :, ln] for ln in lanes]
    u = u_ref[...]
    row = lax.broadcasted_iota(I32, (tq, 2 * tk), 0)
    koff = lax.broadcasted_iota(I32, (tq, 2 * tk), 1) & (tk - 1)
    carries = [(jnp.zeros((tq, 1), F32), jnp.zeros((tq, 1), F32), jnp.zeros((tq, LANES), F32)) for _ in lanes]
    nd = tq // tk
    for d in range(nd):
        off = (nd - 1 - d) * tk
        valid = (koff + off) < row
        ks = pl.multiple_of(q0 + off, tk)
        carries = [_sb_step(qs[p], k_ref[0, pl.ds(ks, tk), ln], v_ref[0, pl.ds(ks, tk), ln], u, carries[p], tk, hd, valid)
                   for p, ln in enumerate(lanes)]

    def block(i):
        ks = pl.multiple_of(q0 - (i + 1) * tk, tk)
        return [(k_ref[0, pl.ds(ks, tk), ln], v_ref[0, pl.ds(ks, tk), ln]) for ln in lanes]

    carries = _sb_far_blocks(qs, block, q0 // tk, u, carries, tk, hd)
    for p, ln in enumerate(lanes):
        o_ref[0, :, ln] = carries[p][2]


def _sb_prompt(q, k, v, hd, tq=256, tk=128):
    b, t, w = q.shape
    npairs = SB_PAIRS_PER_STEP
    wb = npairs * LANES
    assert w % wb == 0
    return pl.pallas_call(
        functools.partial(_sb_prompt_kernel, tq=tq, tk=tk, hd=hd, npairs=npairs),
        grid=(b, w // wb, t // tq),
        in_specs=[pl.BlockSpec((1, tq, wb), lambda bi, hp, qi: (bi, qi, hp)),
                  pl.BlockSpec((1, t, wb), lambda bi, hp, qi: (bi, 0, hp)),
                  pl.BlockSpec((1, t, wb), lambda bi, hp, qi: (bi, 0, hp)),
                  pl.BlockSpec((2 * tk, 2 * tk), lambda bi, hp, qi: (0, 0))],
        out_specs=pl.BlockSpec((1, tq, wb), lambda bi, hp, qi: (bi, qi, hp)),
        out_shape=jax.ShapeDtypeStruct((b, t, w), F32),
        compiler_params=_cparams("parallel", "parallel", "arbitrary"),
        name="sb_prompt",
    )(q, k, v, _suffix_matrix(tk))


def _sb_sample_kernel(q_ref, kn_ref, vn_ref, kc_ref, vc_ref, un_ref, u_ref, o_ref, *, ts, tk, hd, past):
    q2 = q_ref[0]
    row = lax.broadcasted_iota(I32, (ts, 2 * ts), 0)
    koff = lax.broadcasted_iota(I32, (ts, 2 * ts), 1) & (ts - 1)
    carry = (jnp.zeros((ts, 1), F32), jnp.zeros((ts, 1), F32), jnp.zeros((ts, LANES), F32))
    carry = _sb_step(q2, kn_ref[0], vn_ref[0], un_ref[...], carry, ts, hd, koff < row)
    u = u_ref[...]

    def block(i):
        ks = pl.multiple_of(past - (i + 1) * tk, tk)
        return kc_ref[0, pl.ds(ks, tk), :].astype(BF16), vc_ref[0, pl.ds(ks, tk), :].astype(BF16)

    (carry,) = _sb_far_blocks([q2], lambda i: [block(i)], past // tk, u, [carry], tk, hd)
    o_ref[0] = carry[2]


def _sb_sample(q, kn, vn, kc, vc, hd, tk=128):
    b, ts, w = q.shape
    past = kc.shape[1]
    new_spec = pl.BlockSpec((1, ts, LANES), lambda bi, hp: (bi, 0, hp))
    cache_spec = pl.BlockSpec((1, past, LANES), lambda bi, hp: (bi, 0, hp))
    return pl.pallas_call(
        functools.partial(_sb_sample_kernel, ts=ts, tk=tk, hd=hd, past=past),
        grid=(b, w // LANES),
        in_specs=[new_spec, new_spec, new_spec, cache_spec, cache_spec,
                  pl.BlockSpec((2 * ts, 2 * ts), lambda bi, hp: (0, 0)),
                  pl.BlockSpec((2 * tk, 2 * tk), lambda bi, hp: (0, 0))],
        out_specs=new_spec,
        out_shape=jax.ShapeDtypeStruct((b, ts, w), F32),
        compiler_params=_cparams("parallel", "parallel"),
        name="sb_sample",
    )(q, kn, vn, kc, vc, _suffix_matrix(ts), _suffix_matrix(tk))


def _ret_kernel(q_ref, k_ref, v_ref, cos_ref, sin_ref, dec_ref, qd_ref, kd_ref, gl_ref, bd_ref, s0_ref,
                o_ref, st_ref, s_scr, *, tc, nh, hd, k_scale):
    t = pl.program_id(1)
    w = nh * hd
    L = RET_CHUNK

    @pl.when(t == 0)
    def _():
        s_scr[...] = s0_ref[0]

    lane = lax.broadcasted_iota(I32, (L, w), 1)
    first_half = (lane & (hd - 1)) < hd // 2
    head = lane // hd

    def rope(x, cos, sin):
        swapped = jnp.where(first_half, pltpu.roll(x, w - hd // 2, 1), pltpu.roll(x, hd // 2, 1))
        return x * cos + swapped * sin

    s = s_scr[...]
    for ci in range(tc // L):
        sl = slice(ci * L, (ci + 1) * L)
        cos = cos_ref[sl, :]
        sin = sin_ref[sl, :]
        q = rope(q_ref[0, sl, :], cos, sin)
        k = rope(k_ref[0, sl, :], cos, sin) * k_scale
        qb = q.astype(BF16)
        kb = k.astype(BF16)
        vb = v_ref[0, sl, :].astype(BF16)
        zero = jnp.zeros_like(qb)
        out = _dot(qb, s.astype(BF16)) * qd_ref[...]
        for h in range(nh):
            mh = head == h
            sc = _dot_nt(jnp.where(mh, qb, zero), kb)
            p = (sc * dec_ref[h]).astype(BF16)
            out = out + _dot(p, jnp.where(mh, vb, zero))
        o_ref[0, sl, :] = out
        kd = (k * kd_ref[...]).astype(BF16)
        outer = lax.dot_general(kd, vb, (((0,), (0,)), ((), ())), preferred_element_type=F32)
        s = gl_ref[...] * s + bd_ref[...] * outer
    s_scr[...] = s

    @pl.when(t == pl.num_programs(1) - 1)
    def _():
        st_ref[0] = s


def _retention(ret, pos, s0, nh, hd, tc):
    b, t, _ = ret.shape
    w = nh * hd
    L = RET_CHUNK
    half = hd // 2
    freqs = ROPE_BASE ** (-jnp.arange(half, dtype=F32) / half)
    ang = pos.astype(F32)[:, None] * freqs[None, :]
    cos = jnp.tile(jnp.cos(ang), (1, 2 * nh))
    sin = jnp.tile(jnp.concatenate([-jnp.sin(ang), jnp.sin(ang)], axis=1), (1, nh))
    lg = jnp.log1p(-jnp.exp2(-5.0 - jnp.arange(nh, dtype=F32)))
    idx = jnp.arange(L, dtype=F32)
    diff = idx[:, None] - idx[None, :]
    causal = diff >= 0
    dec = jnp.where(causal[None], jnp.exp(jnp.where(causal, diff, 0.0)[None] * lg[:, None, None]), 0.0)
    lg_lane = jnp.repeat(lg, hd)[None, :]
    qd = jnp.exp((idx + 1.0)[:, None] * lg_lane)
    kd = jnp.exp((L - 1.0 - idx)[:, None] * lg_lane)
    gl = jnp.exp(L * lg_lane)
    hid = jnp.arange(w) // hd
    bd = (hid[:, None] == hid[None, :]).astype(F32)
    tok = lambda j: pl.BlockSpec((1, tc, w), lambda bi, ti: (bi, ti, j))
    tab = pl.BlockSpec((tc, w), lambda bi, ti: (ti, 0))
    const = lambda shape: pl.BlockSpec(shape, lambda bi, ti: (0,) * len(shape))
    state = pl.BlockSpec((1, w, w), lambda bi, ti: (bi, 0, 0))
    return pl.pallas_call(
        functools.partial(_ret_kernel, tc=tc, nh=nh, hd=hd, k_scale=hd ** -0.5),
        grid=(b, t // tc),
        in_specs=[tok(0), tok(1), tok(2), tab, tab, const((nh, L, L)), const((L, w)), const((L, w)),
                  const((1, w)), const((w, w)), state],
        out_specs=[pl.BlockSpec((1, tc, w), lambda bi, ti: (bi, ti, 0)), state],
        out_shape=[jax.ShapeDtypeStruct((b, t, w), F32), jax.ShapeDtypeStruct((b, w, w), F32)],
        scratch_shapes=[pltpu.VMEM((w, w), F32)],
        compiler_params=_cparams("parallel", "arbitrary"),
        name="retention",
    )(ret, ret, ret, cos, sin, dec, qd, kd, gl, bd, s0)


def _block_diag_state(st, nh, hd):
    b = st.shape[0]
    eye = jnp.eye(nh, dtype=st.dtype)
    return jnp.einsum('bhde,hg->bhdge', st, eye).reshape(b, nh * hd, nh * hd)


def _diag_blocks(s, nh, hd):
    b = s.shape[0]
    s5 = s.reshape(b, nh, hd, nh, hd)
    return jnp.stack([s5[:, h, :, h, :] for h in range(nh)], axis=1)


def _head_mean_matrix(w, hd):
    hid = np.arange(w) // hd
    return jnp.asarray((hid[:, None] == hid[None, :]).astype(np.float32) / hd, dtype=BF16)


def _headnorm(x, g_mat, gain):
    ms = _dot_split(x * x, g_mat)
    return x * lax.rsqrt(ms + EPS) * gain


def _merge_kernel(x_ref, sb_ref, ret_ref, gate_ref, qm_ref, mk_ref, mv_ref, gsb_ref, gret_ref, gmo_ref,
                  msb_ref, mrt_ref, wo_ref, x1_ref, *, nh_mem, hd):
    wsb = sb_ref.shape[2]
    wrt = ret_ref.shape[2]
    qm = qm_ref[0]
    mkb = mk_ref[0].astype(BF16)
    mvb = mv_ref[0].astype(BF16)
    head_q = lax.broadcasted_iota(I32, qm.shape, 1) // hd
    head_m = lax.broadcasted_iota(I32, mkb.shape, 1) // hd
    mem_o = jnp.zeros(qm.shape, F32)
    for h in range(nh_mem):
        s = _dot_nt(jnp.where(head_q == h, qm, jnp.zeros_like(qm)), mkb)
        e = jnp.exp(s - jnp.max(s, axis=-1, keepdims=True))
        p = e / jnp.sum(e, axis=-1, keepdims=True)
        mem_o = mem_o + _dot(p.astype(BF16), jnp.where(head_m == h, mvb, jnp.zeros_like(mvb)))
    sb_n = _headnorm(sb_ref[0], msb_ref[...], gsb_ref[...])
    gate = gate_ref[0]
    ret_n = _headnorm(ret_ref[0], mrt_ref[...], gret_ref[...]) * (gate / (1.0 + jnp.exp(-gate)))
    mem_n = _headnorm(mem_o, mrt_ref[...], gmo_ref[...])
    y = (_dot(sb_n.astype(BF16), wo_ref[0:wsb, :])
         + _dot(ret_n.astype(BF16), wo_ref[wsb:wsb + wrt, :])
         + _dot(mem_n.astype(BF16), wo_ref[wsb + wrt:, :]))
    x1_ref[0] = x_ref[0] + y


def _merge(x, sb_o, ret_o, ret, qm, mk, mv, g_sb, g_ret, g_mo, wo_bf16, hd, tm):
    b, t, d = x.shape
    wsb = sb_o.shape[2]
    wrt = ret_o.shape[2]
    nmem = mk.shape[1]
    tok = lambda wd, j=0: pl.BlockSpec((1, tm, wd), lambda bi, ti: (bi, ti, j))
    const = lambda shape: pl.BlockSpec(shape, lambda bi, ti: (0,) * len(shape))
    memspec = pl.BlockSpec((1, nmem, wrt), lambda bi, ti: (bi, 0, 0))
    return pl.pallas_call(
        functools.partial(_merge_kernel, nh_mem=wrt // hd, hd=hd),
        grid=(b, t // tm),
        in_specs=[tok(d), tok(wsb), tok(wrt), tok(wrt, 3), tok(wrt), memspec, memspec,
                  const((1, wsb)), const((1, wrt)), const((1, wrt)),
                  const((wsb, wsb)), const((wrt, wrt)), const((d, d))],
        out_specs=tok(d),
        out_shape=jax.ShapeDtypeStruct((b, t, d), F32),
        compiler_params=_cparams("parallel", "parallel"),
        name="merge",
    )(x, sb_o, ret_o, ret, qm, mk, mv, g_sb.reshape(1, wsb), g_ret.reshape(1, wrt), g_mo.reshape(1, wrt),
      _head_mean_matrix(wsb, hd), _head_mean_matrix(wrt, hd), wo_bf16)


def _topk_rows(s, k, rid=None):
    if rid is None:
        rid = lax.broadcasted_iota(I32, s.shape, 0)
    big = jnp.int32(2 ** 30)
    vals, idxs = [], []
    for _ in range(k):
        m = jnp.max(s, axis=0, keepdims=True)
        i = jnp.min(jnp.where(s == m, rid, big), axis=0, keepdims=True)
        vals.append(m)
        idxs.append(i)
        s = jnp.where(rid == i, -jnp.inf, s)
    return jnp.concatenate(vals, axis=0), jnp.concatenate(idxs, axis=0)


def _staircase(k):
    pairs = [(a, b) for a in range(k) for b in range(k) if (a + 1) * (b + 1) <= k]
    pad = -len(pairs) % 8
    a = np.array([p[0] for p in pairs] + [0] * pad, np.int32)[:, None]
    b = np.array([p[1] for p in pairs] + [0] * pad, np.int32)[:, None]
    flat = np.array([p[0] * k + p[1] for p in pairs] + [k * k + i for i in range(pad)], np.int32)[:, None]
    return a, b, flat


def _take_rows(tbl, idx):
    out = jnp.zeros(idx.shape, tbl.dtype)
    for a in range(tbl.shape[0]):
        out = out + jnp.where(idx == a, tbl[a:a + 1, :], 0)
    return out


def _peer_score_kernel(x_ref, g_ref, wq_ref, sk_ref, ca_ref, cb_ref, cf_ref, h_ref, eid_ref, gw_ref, hb_scr,
                       eid_scr, gw_scr, *, topk, nkeys):
    head = pl.program_id(1)

    @pl.when(head == 0)
    def _():
        x = x_ref[...]
        ms = jnp.mean(x * x, axis=-1, keepdims=True)
        hn = x * lax.rsqrt(ms + EPS) * g_ref[...]
        for r in range(ROW_TILES):
            h_ref[pl.ds(r, hn.shape[0], stride=ROW_TILES), :] = hn[:, r * LANES:(r + 1) * LANES]
        hb_scr[...] = hn.astype(BF16)

    qb = _dot(hb_scr[...], wq_ref[...]).astype(BF16)
    hq = qb.shape[1] // 2
    s0 = _dot_nt(sk_ref[0], qb[:, :hq])
    s1 = _dot_nt(sk_ref[1], qb[:, hq:])
    sv0, si0 = _topk_rows(s0, topk)
    sv1, si1 = _topk_rows(s1, topk)
    flat = cf_ref[...]
    cand = jnp.where(flat < topk * topk, _take_rows(sv0, ca_ref[...]) + _take_rows(sv1, cb_ref[...]), -jnp.inf)
    cv, ci = _topk_rows(cand, topk, flat)
    e1 = _take_rows(si0, ci // topk)
    e2 = _take_rows(si1, ci % topk)
    rows = pl.ds(pl.multiple_of(head * topk, topk), topk)
    eid_scr[rows, :] = ((e1 * nkeys + e2) * HALF_ROWS).astype(F32)
    e = jnp.exp(cv - cv[0:1, :])
    gw_scr[rows, :] = e / jnp.sum(e, axis=0, keepdims=True)

    @pl.when(head == pl.num_programs(1) - 1)
    def _():
        eid_ref[...] = eid_scr[...].T.astype(I32)
        gw_ref[...] = gw_scr[...].T


def _peer_score(x, g, wq_bf16, sk_bf16, tm):
    n, d = x.shape
    nkeys, khalf = sk_bf16.shape[1], sk_bf16.shape[2]
    nheads = wq_bf16.shape[1] // (2 * khalf)
    k = PEER_TOPK
    ca, cb, cf = (jnp.asarray(np.broadcast_to(c, (c.shape[0], tm))) for c in _staircase(k))
    nc = ca.shape[0]
    xs = pl.BlockSpec((tm, d), lambda i, h: (i, 0))
    cs = pl.BlockSpec((nc, tm), lambda i, h: (0, 0))
    return pl.pallas_call(
        functools.partial(_peer_score_kernel, topk=k, nkeys=nkeys),
        grid=(n // tm, nheads),
        in_specs=[xs, pl.BlockSpec((1, d), lambda i, h: (0, 0)),
                  pl.BlockSpec((d, 2 * khalf), lambda i, h: (0, h)),
                  pl.BlockSpec((2, nkeys, khalf), lambda i, h: (0, 0, 0)), cs, cs, cs],
        out_specs=[pl.BlockSpec((tm * ROW_TILES, LANES), lambda i, h: (i, 0)),
                   pl.BlockSpec((tm, nheads * k), lambda i, h: (i, 0)),
                   pl.BlockSpec((tm, nheads * k), lambda i, h: (i, 0))],
        out_shape=[jax.ShapeDtypeStruct((n * ROW_TILES, LANES), F32),
                   jax.ShapeDtypeStruct((n, nheads * k), I32), jax.ShapeDtypeStruct((n, nheads * k), F32)],
        scratch_shapes=[pltpu.VMEM((tm, d), BF16), pltpu.VMEM((nheads * k, tm), F32),
                        pltpu.VMEM((nheads * k, tm), F32)],
        compiler_params=_cparams("parallel", "arbitrary"),
        name="peer_score",
    )(x, g.reshape(1, d), wq_bf16, sk_bf16, ca, cb, cf)


def _pack_table(tab):
    e, d = tab.shape
    assert d == ROW_TILES * LANES
    te = _row_tile(e, 512)
    return pl.pallas_call(
        _pack_kernel,
        grid=(e // te,),
        in_specs=[pl.BlockSpec((te, d), lambda i: (i, 0))],
        out_specs=pl.BlockSpec((te * HALF_ROWS, LANES), lambda i: (i, 0)),
        out_shape=jax.ShapeDtypeStruct((e * HALF_ROWS, LANES), I32),
        compiler_params=_cparams("parallel"),
        name="pack_table",
    )(tab)


def _pack_kernel(t_ref, o_ref):
    te = t_ref.shape[0]
    x = t_ref[...]
    for s in range(HALF_ROWS):
        even = x[:, (2 * s) * LANES:(2 * s + 1) * LANES]
        odd = lax.bitcast_convert_type(x[:, (2 * s + 1) * LANES:(2 * s + 2) * LANES], I32)
        low = lax.shift_right_logical(lax.bitcast_convert_type(even.astype(BF16).astype(F32), I32), 16)
        sign = odd & jnp.int32(-2 ** 31)
        mag = (odd & jnp.int32(2 ** 31 - 1)) + jnp.int32(0x8000)
        high = jnp.where(mag >= low, (mag - low) >> 16, 0)
        o_ref[pl.ds(s, te, stride=HALF_ROWS), :] = sign | (high << 16) | low


def _unpack(wd):
    even = lax.bitcast_convert_type(wd << 16, F32)
    odd = lax.bitcast_convert_type(wd, F32)
    return even, odd


def _erf(x):
    return lax.erf(x)


def _token_tiles(ref, t):
    base = pl.multiple_of(t * ROW_TILES, ROW_TILES)
    return ref[pl.ds(base, HALF_ROWS, stride=2), :], ref[pl.ds(base + 1, HALF_ROWS, stride=2), :]


PEER_U_SET = 4


def _peer_u_kernel(eid_ref, h_ref, gw_ref, tab_ref, w_ref, pa, pb, q_scr, act_scr, *, tg, npairs):
    ones = jnp.ones((LANES, LANES), BF16)
    row = lax.broadcasted_iota(I32, (npairs, LANES), 0)
    eye = (row == lax.broadcasted_iota(I32, (npairs, LANES), 1)).astype(F32)
    ns = PEER_U_SET

    def products(t, p_scr, qset, k):
        he, ho = _token_tiles(h_ref, t)
        for j in range(npairs):
            e4 = pl.multiple_of(eid_ref[t, j], HALF_ROWS)
            even, odd = _unpack(tab_ref[pl.ds(e4, HALF_ROWS), :])
            p_scr[j * HALF_ROWS:(j + 1) * HALF_ROWS, :] = even * he + odd * ho
        q = p_scr[pl.ds(0, npairs, stride=HALF_ROWS), :]
        for s in range(1, HALF_ROWS):
            q = q + p_scr[pl.ds(s, npairs, stride=HALF_ROWS), :]
        hi, lo = _split_bf16(q)
        q_scr[qset, 2 * k * npairs:(2 * k + 1) * npairs, :] = hi
        q_scr[qset, (2 * k + 1) * npairs:(2 * k + 2) * npairs, :] = lo

    def lane_sums(qset, tokens):
        r = _dot(q_scr[qset], ones)
        for k, t in enumerate(tokens):
            rk = r[2 * k * npairs:(2 * k + 1) * npairs, :] + r[(2 * k + 1) * npairs:(2 * k + 2) * npairs, :]
            act_scr[pl.ds(t, 1), :] = jnp.sum(rk * eye, axis=0, keepdims=True)

    q_scr[1] = jnp.zeros(q_scr.shape[1:], BF16)

    def body(i, carry):
        t0 = 2 * ns * i
        lane_sums(1, [jnp.maximum(t0 - ns + k, 0) for k in range(ns)])
        for k in range(ns):
            products(t0 + k, (pa, pb)[k % 2], 0, k)
        lane_sums(0, [t0 + k for k in range(ns)])
        for k in range(ns):
            products(t0 + ns + k, (pa, pb)[k % 2], 1, k)
        return carry

    lax.fori_loop(0, tg // (2 * ns), body, 0)
    lane_sums(1, [tg - ns + k for k in range(ns)])
    act = act_scr[...]
    gelu = 0.5 * act * (1.0 + _erf(act * (2.0 ** -0.5)))
    w_ref[...] = gw_ref[...] * gelu


def _peer_u(eid, h, gw, tab_packed, tg):
    n, npairs = eid.shape
    assert npairs == LANES and h.shape == (n * ROW_TILES, LANES) and tg % (2 * PEER_U_SET) == 0
    tspec = pl.BlockSpec((tg, npairs), lambda i: (i, 0))
    pshape = pltpu.VMEM((npairs * HALF_ROWS, LANES), F32)
    qshape = pltpu.VMEM((2, 2 * PEER_U_SET * npairs, LANES), BF16)
    return pl.pallas_call(
        functools.partial(_peer_u_kernel, tg=tg, npairs=npairs),
        grid=(n // tg,),
        in_specs=[pl.BlockSpec((tg, npairs), lambda i: (i, 0), memory_space=pltpu.SMEM),
                  pl.BlockSpec((tg * ROW_TILES, LANES), lambda i: (i, 0)), tspec,
                  pl.BlockSpec(memory_space=pltpu.VMEM)],
        out_specs=tspec,
        out_shape=jax.ShapeDtypeStruct((n, npairs), F32),
        scratch_shapes=[pshape, pshape, qshape, pltpu.VMEM((tg, npairs), F32)],
        compiler_params=_cparams("arbitrary"),
        name="peer_u",
    )(eid, h, gw, tab_packed)


PEER_V_ACCUMULATORS = 2


def _peer_v_kernel(eid_ref, w_ref, tab_ref, o_ref, *, tg, npairs):
    nacc = PEER_V_ACCUMULATORS

    def token(t):
        acc_e = [jnp.zeros((HALF_ROWS, LANES), F32) for _ in range(nacc)]
        acc_o = [jnp.zeros((HALF_ROWS, LANES), F32) for _ in range(nacc)]
        for j in range(npairs):
            e4 = pl.multiple_of(eid_ref[t, j], HALF_ROWS)
            even, odd = _unpack(tab_ref[pl.ds(e4, HALF_ROWS), :])
            wt = w_ref[t, j]
            acc_e[j % nacc] = acc_e[j % nacc] + wt * even
            acc_o[j % nacc] = acc_o[j % nacc] + wt * odd
        base = pl.multiple_of(t * ROW_TILES, ROW_TILES)
        o_ref[pl.ds(base, HALF_ROWS, stride=2), :] = functools.reduce(lambda a, b: a + b, acc_e)
        o_ref[pl.ds(base + 1, HALF_ROWS, stride=2), :] = functools.reduce(lambda a, b: a + b, acc_o)

    def body(i, carry):
        token(2 * i)
        token(2 * i + 1)
        return carry

    lax.fori_loop(0, tg // 2, body, 0)


def _peer_v(eid, w, tab_packed, tg):
    n, npairs = eid.shape
    assert tg % 2 == 0
    sspec = pl.BlockSpec((tg, npairs), lambda i: (i, 0), memory_space=pltpu.SMEM)
    return pl.pallas_call(
        functools.partial(_peer_v_kernel, tg=tg, npairs=npairs),
        grid=(n // tg,),
        in_specs=[sspec, sspec, pl.BlockSpec(memory_space=pltpu.VMEM)],
        out_specs=pl.BlockSpec((tg * ROW_TILES, LANES), lambda i: (i, 0)),
        out_shape=jax.ShapeDtypeStruct((n * ROW_TILES, LANES), F32),
        compiler_params=_cparams("arbitrary"),
        name="peer_v",
    )(eid, w, tab_packed)


def _final_kernel(x_ref, p_ref, g_ref, y_ref):
    tm = x_ref.shape[0]
    p = jnp.concatenate([p_ref[pl.ds(r, tm, stride=ROW_TILES), :] for r in range(ROW_TILES)], axis=1)
    x = x_ref[...] + p
    ms = jnp.mean(x * x, axis=-1, keepdims=True)
    y_ref[...] = x * lax.rsqrt(ms + EPS) * g_ref[...]


def _final(x, p, g, tm):
    n, d = x.shape
    xs = pl.BlockSpec((tm, d), lambda i: (i, 0))
    return pl.pallas_call(
        _final_kernel,
        grid=(n // tm,),
        in_specs=[xs, pl.BlockSpec((tm * ROW_TILES, LANES), lambda i: (i, 0)), pl.BlockSpec((1, d), lambda i: (0, 0))],
        out_specs=xs,
        out_shape=jax.ShapeDtypeStruct((n, d), F32),
        compiler_params=_cparams("parallel"),
        name="final_norm",
    )(x, p, g.reshape(1, d))


def _row_tile(n, pref):
    t = min(pref, n)
    assert n % t == 0, (n, t)
    return t


def _stream(x, w_in_bf, g_mix, dims):
    b, t, d = x.shape
    wsb, wrt, wmm, hd = dims
    scale = hd ** -0.5
    o = 3 * wsb
    outs = [(0, wsb, scale, BF16, 0),
            (wsb, 2 * wsb, 1.0, F32, wsb // hd), (2 * wsb, o, 1.0, F32, wsb // hd),
            (wsb, 2 * wsb, 1.0, BF16, 0), (2 * wsb, o, 1.0, BF16, 0),
            (o, o + 4 * wrt, 1.0, F32, 0),
            (o + 4 * wrt, o + 4 * wrt + wmm, scale, BF16, 0)]
    res = _norm_proj(x.reshape(b * t, d), g_mix, w_in_bf, outs, _row_tile(b * t, 512))
    return [r.reshape(b, t, r.shape[1]) if r.shape[0] == b * t else r for r in res]


def _peer_and_final(x1, g_ffn, wq_bf, sk_bf, u_packed, v_packed, g_final):
    n = x1.shape[0]
    h, eid, gw = _peer_score(x1, g_ffn, wq_bf, sk_bf, _row_tile(n, 512))
    tg = _row_tile(n, 128)
    w = _peer_u(eid, h, gw, u_packed, tg)
    p = _peer_v(eid, w, v_packed, tg)
    return _final(x1, p, g_final, _row_tile(n, 512))


def kernel(x_prompt, x_sample, mem_prompt, cache_sb_k, cache_sb_v, state_ret, cache_mem_k, cache_mem_v,
           g_mix, w_in, w_out, g_sb_out, g_ret_out, g_mem_out, g_mem, w_mem_kv,
           g_ffn, w_peer_q, peer_sub_keys, peer_u, peer_v, g_final):
    depth = w_in.shape[0]
    assert depth == 1, "single-layer step"
    bp, tp, d = x_prompt.shape
    bs, ts, _ = x_sample.shape
    _, _, past, sb_heads, hd = cache_sb_k.shape
    ret_heads = state_ret.shape[2]
    mem_heads = cache_mem_k.shape[3]
    nmem = mem_prompt.shape[1]
    wsb, wrt, wmm = sb_heads * hd, ret_heads * hd, mem_heads * hd
    assert hd == 64 and wrt == wmm and d == 2 * HALF_ROWS * LANES
    l = 0
    w_in_bf = w_in[l].astype(BF16)
    wo_bf = w_out[l].astype(BF16)
    dims = (wsb, wrt, wmm, hd)

    q_p, k_p, v_p, kb_p, vb_p, ret_p, qm_p = _stream(x_prompt, w_in_bf, g_mix[l], dims)
    sb_p = _sb_prompt(q_p, kb_p, vb_p, hd)
    zeros_state = jnp.zeros((bp, wrt, wrt), F32)
    ro_p, st_p = _retention(ret_p, jnp.arange(tp, dtype=I32), zeros_state, ret_heads, hd, _row_tile(tp, 512))
    mk_p, mv_p = _norm_proj(mem_prompt.reshape(bp * nmem, d), g_mem[l], w_mem_kv[l].astype(BF16),
                            [(0, wmm, 1.0, F32, 0), (wmm, 2 * wmm, 1.0, F32, 0)], _row_tile(bp * nmem, 512))
    mk_p = mk_p.reshape(bp, nmem, wmm)
    mv_p = mv_p.reshape(bp, nmem, wmm)
    x1_p = _merge(x_prompt, sb_p, ro_p, ret_p, qm_p, mk_p, mv_p, g_sb_out[l], g_ret_out[l], g_mem_out[l],
                  wo_bf, hd, _row_tile(tp, 256))

    q_s, k_s, v_s, kb_s, vb_s, ret_s, qm_s = _stream(x_sample, w_in_bf, g_mix[l], dims)
    sb_s = _sb_sample(q_s, kb_s, vb_s, cache_sb_k[l].reshape(bs, past, wsb), cache_sb_v[l].reshape(bs, past, wsb), hd)
    s0 = _block_diag_state(state_ret[l].astype(F32), ret_heads, hd)
    ro_s, st_s = _retention(ret_s, past + jnp.arange(ts, dtype=I32), s0, ret_heads, hd, _row_tile(ts, 512))
    x1_s = _merge(x_sample, sb_s, ro_s, ret_s, qm_s, cache_mem_k[l].reshape(bs, nmem, wmm),
                  cache_mem_v[l].reshape(bs, nmem, wmm), g_sb_out[l], g_ret_out[l], g_mem_out[l],
                  wo_bf, hd, _row_tile(ts, 256))

    wq_bf = w_peer_q[l].astype(BF16)
    sk_bf = peer_sub_keys[l].astype(BF16)
    u_packed = _pack_table(peer_u[l])
    v_packed = _pack_table(peer_v[l])
    y_p = _peer_and_final(x1_p.reshape(bp * tp, d), g_ffn[l], wq_bf, sk_bf, u_packed, v_packed, g_final)
    y_s = _peer_and_final(x1_s.reshape(bs * ts, d), g_ffn[l], wq_bf, sk_bf, u_packed, v_packed, g_final)

    return (y_p.reshape(bp, tp, d), y_s.reshape(bs, ts, d),
            k_p.reshape(1, bp, tp, sb_heads, hd), v_p.reshape(1, bp, tp, sb_heads, hd),
            _diag_blocks(st_p, ret_heads, hd)[None],
            mk_p.reshape(1, bp, nmem, mem_heads, hd), mv_p.reshape(1, bp, nmem, mem_heads, hd),
            k_s.reshape(1, bs, ts, sb_heads, hd), v_s.reshape(1, bs, ts, sb_heads, hd),
            _diag_blocks(st_s, ret_heads, hd)[None].astype(state_ret.dtype))
```

```python
import functools

import numpy as np
import jax
import jax.numpy as jnp
from jax import lax
from jax.experimental import pallas as pl
from jax.experimental.pallas import tpu as pltpu

F32 = jnp.float32
BF16 = jnp.bfloat16
I32 = jnp.int32

EPS = 1e-6
ROPE_BASE = 10000.0
PEER_TOPK = 16
RET_CHUNK = 64
LANES = 128
ROW_TILES = 8
HALF_ROWS = 4
VMEM_LIMIT = 56 * 1024 * 1024
SB_DEAD_LOG = -105.0


def _cparams(*sem):
    return pltpu.CompilerParams(dimension_semantics=sem, vmem_limit_bytes=VMEM_LIMIT)


def _dot(a, b):
    return jnp.dot(a, b, preferred_element_type=F32)


def _dot_nt(a, b):
    return lax.dot_general(a, b, (((1,), (1,)), ((), ())), preferred_element_type=F32)


def _split_bf16(x):
    hi = x.astype(BF16)
    lo = (x - hi.astype(F32)).astype(BF16)
    return hi, lo


def _dot_split(x, m_bf16):
    hi, lo = _split_bf16(x)
    return _dot(hi, m_bf16) + _dot(lo, m_bf16)


def _norm_proj_kernel(x_ref, g_ref, w_ref, *out_refs, cols):
    x = x_ref[...]
    ms = jnp.mean(x * x, axis=-1, keepdims=True)
    h = x * lax.rsqrt(ms + EPS) * g_ref[...]
    p = _dot(h.astype(BF16), w_ref[...])
    tm = x.shape[0]
    for ref, (lo, hi, scale, heads) in zip(out_refs, cols):
        v = p[:, lo:hi]
        if scale != 1.0:
            v = v * scale
        if heads:
            hd = (hi - lo) // heads
            for h in range(heads):
                ref[pl.ds(h, tm, stride=heads), :] = v[:, h * hd:(h + 1) * hd].astype(ref.dtype)
        else:
            ref[...] = v.astype(ref.dtype)


def _norm_proj(x, g, w_bf16, outs, tm):
    n, d = x.shape
    wcols = w_bf16.shape[1]
    shape = lambda rows, lo, hi, hds: (rows * hds, (hi - lo) // hds) if hds else (rows, hi - lo)
    out_shape = [jax.ShapeDtypeStruct(shape(n, lo, hi, hds), dt) for lo, hi, _, dt, hds in outs]
    out_specs = [pl.BlockSpec(shape(tm, lo, hi, hds), lambda i: (i, 0)) for lo, hi, _, _, hds in outs]
    return pl.pallas_call(
        functools.partial(_norm_proj_kernel, cols=tuple((lo, hi, sc, hds) for lo, hi, sc, _, hds in outs)),
        grid=(n // tm,),
        in_specs=[pl.BlockSpec((tm, d), lambda i: (i, 0)),
                  pl.BlockSpec((1, d), lambda i: (0, 0)),
                  pl.BlockSpec((d, wcols), lambda i: (0, 0))],
        out_specs=out_specs,
        out_shape=out_shape,
        compiler_params=_cparams("parallel"),
        name="norm_proj",
    )(x, g.reshape(1, d), w_bf16)


def _suffix_matrix(tk):
    j = np.arange(2 * tk)[:, None]
    s = np.arange(2 * tk)[None, :]
    return jnp.asarray(((j // tk == s // tk) & (j >= s)).astype(np.float32), dtype=BF16)


def _sb_step(q2, kb, vb, u, carry, tk, hd, valid):
    c0, c1, acc = carry
    tq = q2.shape[0]
    lane = lax.broadcasted_iota(I32, kb.shape, 1)
    first = lane < hd
    zero = jnp.zeros_like(kb)
    kk = jnp.concatenate([jnp.where(first, kb, zero), jnp.where(first, zero, kb)], axis=0)
    vv = jnp.concatenate([jnp.where(first, vb, zero), jnp.where(first, zero, vb)], axis=0)
    z = _dot_nt(q2, kk)
    sp = jnp.log(1.0 + jnp.exp(-jnp.abs(z)))
    lk = jnp.minimum(-z, 0.0) - sp
    if valid is not None:
        lk = jnp.where(valid, lk, 0.0)
    m = _dot_split(lk, u)
    c = jnp.concatenate([jnp.broadcast_to(c0, (tq, tk)), jnp.broadcast_to(c1, (tq, tk))], axis=1)
    a = jnp.exp(z + c + m)
    if valid is not None:
        a = jnp.where(valid, a, 0.0)
    acc = acc + _dot(a.astype(BF16), vv)
    c0 = c0 + jnp.sum(lk[:, :tk], axis=-1, keepdims=True)
    c1 = c1 + jnp.sum(lk[:, tk:], axis=-1, keepdims=True)
    return c0, c1, acc


def _sb_alive(carries):
    m = functools.reduce(jnp.maximum, [jnp.maximum(c0, c1) for c0, c1, _ in carries])
    return jnp.max(m) > SB_DEAD_LOG


def _sb_far_blocks(qs, block, nblk, u, carries, tk, hd):
    n = len(qs)
    flatten = lambda cs: tuple(x for c in cs for x in c)
    unflatten = lambda flat: [tuple(flat[3 * p:3 * p + 3]) for p in range(n)]

    def cond(state):
        return jnp.logical_and(state[0] < nblk, state[1])

    def body(state):
        i = state[0]
        kvs = block(i)
        cs = [_sb_step(qs[p], kvs[p][0], kvs[p][1], u, c, tk, hd, None) for p, c in enumerate(unflatten(state[2:]))]
        return (i + 1, _sb_alive(cs)) + flatten(cs)

    state = lax.while_loop(cond, body, (jnp.int32(0), _sb_alive(carries)) + flatten(carries))
    return unflatten(state[2:])


SB_PAIRS_PER_STEP = 2


def _sb_prompt_kernel(q_ref, k_ref, v_ref, u_ref, o_ref, *, tq, tk, hd, npairs):
    q0 = pl.program_id(2) * tq
    lanes = [slice(p * LANES, (p + 1) * LANES) for p in range(npairs)]
    qs = [q_ref[0, :, ln] for ln in lanes]
    u = u_ref[...]
    row = lax.broadcasted_iota(I32, (tq, 2 * tk), 0)
    koff = lax.broadcasted_iota(I32, (tq, 2 * tk), 1) & (tk - 1)
    carries = [(jnp.zeros((tq, 1), F32), jnp.zeros((tq, 1), F32), jnp.zeros((tq, LANES), F32)) for _ in lanes]
    nd = tq // tk
    for d in range(nd):
        off = (nd - 1 - d) * tk
        valid = (koff + off) < row
        ks = pl.multiple_of(q0 + off, tk)
        carries = [_sb_step(qs[p], k_ref[0, pl.ds(ks, tk), ln], v_ref[0, pl.ds(ks, tk), ln], u, carries[p], tk, hd, valid)
                   for p, ln in enumerate(lanes)]

    def block(i):
        ks = pl.multiple_of(q0 - (i + 1) * tk, tk)
        return [(k_ref[0, pl.ds(ks, tk), ln], v_ref[0, pl.ds(ks, tk), ln]) for ln in lanes]

    carries = _sb_far_blocks(qs, block, q0 // tk, u, carries, tk, hd)
    for p, ln in enumerate(lanes):
        o_ref[0, :, ln] = carries[p][2]


def _sb_prompt(q, k, v, hd, tq=256, tk=128):
    b, t, w = q.shape
    npairs = SB_PAIRS_PER_STEP
    wb = npairs * LANES
    assert w % wb == 0
    return pl.pallas_call(
        functools.partial(_sb_prompt_kernel, tq=tq, tk=tk, hd=hd, npairs=npairs),
        grid=(b, w // wb, t // tq),
        in_specs=[pl.BlockSpec((1, tq, wb), lambda bi, hp, qi: (bi, qi, hp)),
                  pl.BlockSpec((1, t, wb), lambda bi, hp, qi: (bi, 0, hp)),
                  pl.BlockSpec((1, t, wb), lambda bi, hp, qi: (bi, 0, hp)),
                  pl.BlockSpec((2 * tk, 2 * tk), lambda bi, hp, qi: (0, 0))],
        out_specs=pl.BlockSpec((1, tq, wb), lambda bi, hp, qi: (bi, qi, hp)),
        out_shape=jax.ShapeDtypeStruct((b, t, w), F32),
        compiler_params=_cparams("parallel", "parallel", "arbitrary"),
        name="sb_prompt",
    )(q, k, v, _suffix_matrix(tk))


def _sb_sample_kernel(q_ref, kn_ref, vn_ref, kc_ref, vc_ref, un_ref, u_ref, o_ref, *, ts, tk, hd, past):
    q2 = q_ref[0]
    row = lax.broadcasted_iota(I32, (ts, 2 * ts), 0)
    koff = lax.broadcasted_iota(I32, (ts, 2 * ts), 1) & (ts - 1)
    carry = (jnp.zeros((ts, 1), F32), jnp.zeros((ts, 1), F32), jnp.zeros((ts, LANES), F32))
    carry = _sb_step(q2, kn_ref[0], vn_ref[0], un_ref[...], carry, ts, hd, koff < row)
    u = u_ref[...]

    def block(i):
        ks = pl.multiple_of(past - (i + 1) * tk, tk)
        return kc_ref[0, pl.ds(ks, tk), :].astype(BF16), vc_ref[0, pl.ds(ks, tk), :].astype(BF16)

    (carry,) = _sb_far_blocks([q2], lambda i: [block(i)], past // tk, u, [carry], tk, hd)
    o_ref[0] = carry[2]


def _sb_sample(q, kn, vn, kc, vc, hd, tk=128):
    b, ts, w = q.shape
    past = kc.shape[1]
    new_spec = pl.BlockSpec((1, ts, LANES), lambda bi, hp: (bi, 0, hp))
    cache_spec = pl.BlockSpec((1, past, LANES), lambda bi, hp: (bi, 0, hp))
    return pl.pallas_call(
        functools.partial(_sb_sample_kernel, ts=ts, tk=tk, hd=hd, past=past),
        grid=(b, w // LANES),
        in_specs=[new_spec, new_spec, new_spec, cache_spec, cache_spec,
                  pl.BlockSpec((2 * ts, 2 * ts), lambda bi, hp: (0, 0)),
                  pl.BlockSpec((2 * tk, 2 * tk), lambda bi, hp: (0, 0))],
        out_specs=new_spec,
        out_shape=jax.ShapeDtypeStruct((b, ts, w), F32),
        compiler_params=_cparams("parallel", "parallel"),
        name="sb_sample",
    )(q, kn, vn, kc, vc, _suffix_matrix(ts), _suffix_matrix(tk))


def _ret_kernel(q_ref, k_ref, v_ref, cos_ref, sin_ref, dec_ref, qd_ref, kd_ref, gl_ref, bd_ref, s0_ref,
                o_ref, st_ref, s_scr, *, tc, nh, hd, k_scale):
    t = pl.program_id(1)
    w = nh * hd
    L = RET_CHUNK

    @pl.when(t == 0)
    def _():
        s_scr[...] = s0_ref[0]

    lane = lax.broadcasted_iota(I32, (L, w), 1)
    first_half = (lane & (hd - 1)) < hd // 2
    head = lane // hd

    def rope(x, cos, sin):
        swapped = jnp.where(first_half, pltpu.roll(x, w - hd // 2, 1), pltpu.roll(x, hd // 2, 1))
        return x * cos + swapped * sin

    s = s_scr[...]
    for ci in range(tc // L):
        sl = slice(ci * L, (ci + 1) * L)
        cos = cos_ref[sl, :]
        sin = sin_ref[sl, :]
        q = rope(q_ref[0, sl, :], cos, sin)
        k = rope(k_ref[0, sl, :], cos, sin) * k_scale
        qb = q.astype(BF16)
        kb = k.astype(BF16)
        vb = v_ref[0, sl, :].astype(BF16)
        zero = jnp.zeros_like(qb)
        out = _dot(qb, s.astype(BF16)) * qd_ref[...]
        for h in range(nh):
            mh = head == h
            sc = _dot_nt(jnp.where(mh, qb, zero), kb)
            p = (sc * dec_ref[h]).astype(BF16)
            out = out + _dot(p, jnp.where(mh, vb, zero))
        o_ref[0, sl, :] = out
        kd = (k * kd_ref[...]).astype(BF16)
        outer = lax.dot_general(kd, vb, (((0,), (0,)), ((), ())), preferred_element_type=F32)
        s = gl_ref[...] * s + bd_ref[...] * outer
    s_scr[...] = s

    @pl.when(t == pl.num_programs(1) - 1)
    def _():
        st_ref[0] = s


def _retention(ret, pos, s0, nh, hd, tc):
    b, t, _ = ret.shape
    w = nh * hd
    L = RET_CHUNK
    half = hd // 2
    freqs = ROPE_BASE ** (-jnp.arange(half, dtype=F32) / half)
    ang = pos.astype(F32)[:, None] * freqs[None, :]
    cos = jnp.tile(jnp.cos(ang), (1, 2 * nh))
    sin = jnp.tile(jnp.concatenate([-jnp.sin(ang), jnp.sin(ang)], axis=1), (1, nh))
    lg = jnp.log1p(-jnp.exp2(-5.0 - jnp.arange(nh, dtype=F32)))
    idx = jnp.arange(L, dtype=F32)
    diff = idx[:, None] - idx[None, :]
    causal = diff >= 0
    dec = jnp.where(causal[None], jnp.exp(jnp.where(causal, diff, 0.0)[None] * lg[:, None, None]), 0.0)
    lg_lane = jnp.repeat(lg, hd)[None, :]
    qd = jnp.exp((idx + 1.0)[:, None] * lg_lane)
    kd = jnp.exp((L - 1.0 - idx)[:, None] * lg_lane)
    gl = jnp.exp(L * lg_lane)
    hid = jnp.arange(w) // hd
    bd = (hid[:, None] == hid[None, :]).astype(F32)
    tok = lambda j: pl.BlockSpec((1, tc, w), lambda bi, ti: (bi, ti, j))
    tab = pl.BlockSpec((tc, w), lambda bi, ti: (ti, 0))
    const = lambda shape: pl.BlockSpec(shape, lambda bi, ti: (0,) * len(shape))
    state = pl.BlockSpec((1, w, w), lambda bi, ti: (bi, 0, 0))
    return pl.pallas_call(
        functools.partial(_ret_kernel, tc=tc, nh=nh, hd=hd, k_scale=hd ** -0.5),
        grid=(b, t // tc),
        in_specs=[tok(0), tok(1), tok(2), tab, tab, const((nh, L, L)), const((L, w)), const((L, w)),
                  const((1, w)), const((w, w)), state],
        out_specs=[pl.BlockSpec((1, tc, w), lambda bi, ti: (bi, ti, 0)), state],
        out_shape=[jax.ShapeDtypeStruct((b, t, w), F32), jax.ShapeDtypeStruct((b, w, w), F32)],
        scratch_shapes=[pltpu.VMEM((w, w), F32)],
        compiler_params=_cparams("parallel", "arbitrary"),
        name="retention",
    )(ret, ret, ret, cos, sin, dec, qd, kd, gl, bd, s0)


def _block_diag_state(st, nh, hd):
    b = st.shape[0]
    eye = jnp.eye(nh, dtype=st.dtype)
    return jnp.einsum('bhde,hg->bhdge', st, eye).reshape(b, nh * hd, nh * hd)


def _diag_blocks(s, nh, hd):
    b = s.shape[0]
    s5 = s.reshape(b, nh, hd, nh, hd)
    return jnp.stack([s5[:, h, :, h, :] for h in range(nh)], axis=1)


def _head_mean_matrix(w, hd):
    hid = np.arange(w) // hd
    return jnp.asarray((hid[:, None] == hid[None, :]).astype(np.float32) / hd, dtype=BF16)


def _headnorm(x, g_mat, gain):
    ms = _dot_split(x * x, g_mat)
    return x * lax.rsqrt(ms + EPS) * gain


def _merge_kernel(x_ref, sb_ref, ret_ref, gate_ref, qm_ref, mk_ref, mv_ref, gsb_ref, gret_ref, gmo_ref,
                  msb_ref, mrt_ref, wo_ref, x1_ref, *, nh_mem, hd):
    wsb = sb_ref.shape[2]
    wrt = ret_ref.shape[2]
    qm = qm_ref[0]
    mkb = mk_ref[0].astype(BF16)
    mvb = mv_ref[0].astype(BF16)
    head_q = lax.broadcasted_iota(I32, qm.shape, 1) // hd
    head_m = lax.broadcasted_iota(I32, mkb.shape, 1) // hd
    mem_o = jnp.zeros(qm.shape, F32)
    for h in range(nh_mem):
        s = _dot_nt(jnp.where(head_q == h, qm, jnp.zeros_like(qm)), mkb)
        e = jnp.exp(s - jnp.max(s, axis=-1, keepdims=True))
        p = e / jnp.sum(e, axis=-1, keepdims=True)
        mem_o = mem_o + _dot(p.astype(BF16), jnp.where(head_m == h, mvb, jnp.zeros_like(mvb)))
    sb_n = _headnorm(sb_ref[0], msb_ref[...], gsb_ref[...])
    gate = gate_ref[0]
    ret_n = _headnorm(ret_ref[0], mrt_ref[...], gret_ref[...]) * (gate / (1.0 + jnp.exp(-gate)))
    mem_n = _headnorm(mem_o, mrt_ref[...], gmo_ref[...])
    y = (_dot(sb_n.astype(BF16), wo_ref[0:wsb, :])
         + _dot(ret_n.astype(BF16), wo_ref[wsb:wsb + wrt, :])
         + _dot(mem_n.astype(BF16), wo_ref[wsb + wrt:, :]))
    x1_ref[0] = x_ref[0] + y


def _merge(x, sb_o, ret_o, ret, qm, mk, mv, g_sb, g_ret, g_mo, wo_bf16, hd, tm):
    b, t, d = x.shape
    wsb = sb_o.shape[2]
    wrt = ret_o.shape[2]
    nmem = mk.shape[1]
    tok = lambda wd, j=0: pl.BlockSpec((1, tm, wd), lambda bi, ti: (bi, ti, j))
    const = lambda shape: pl.BlockSpec(shape, lambda bi, ti: (0,) * len(shape))
    memspec = pl.BlockSpec((1, nmem, wrt), lambda bi, ti: (bi, 0, 0))
    return pl.pallas_call(
        functools.partial(_merge_kernel, nh_mem=wrt // hd, hd=hd),
        grid=(b, t // tm),
        in_specs=[tok(d), tok(wsb), tok(wrt), tok(wrt, 3), tok(wrt), memspec, memspec,
                  const((1, wsb)), const((1, wrt)), const((1, wrt)),
                  const((wsb, wsb)), const((wrt, wrt)), const((d, d))],
        out_specs=tok(d),
        out_shape=jax.ShapeDtypeStruct((b, t, d), F32),
        compiler_params=_cparams("parallel", "parallel"),
        name="merge",
    )(x, sb_o, ret_o, ret, qm, mk, mv, g_sb.reshape(1, wsb), g_ret.reshape(1, wrt), g_mo.reshape(1, wrt),
      _head_mean_matrix(wsb, hd), _head_mean_matrix(wrt, hd), wo_bf16)


def _topk_rows(s, k, rid=None):
    if rid is None:
        rid = lax.broadcasted_iota(I32, s.shape, 0)
    big = jnp.int32(2 ** 30)
    vals, idxs = [], []
    for _ in range(k):
        m = jnp.max(s, axis=0, keepdims=True)
        i = jnp.min(jnp.where(s == m, rid, big), axis=0, keepdims=True)
        vals.append(m)
        idxs.append(i)
        s = jnp.where(rid == i, -jnp.inf, s)
    return jnp.concatenate(vals, axis=0), jnp.concatenate(idxs, axis=0)


def _staircase_blocks(k, rows=8):
    blocks = []
    singles = [a for a in range(k) if k // (a + 1) == 1]
    for a in range(k):
        nb = k // (a + 1)
        if nb > 1:
            blocks.extend(('row', a, b0) for b0 in range(0, nb, rows))
    assert singles and singles[0] % rows == 0 and len(singles) % rows == 0 and singles[-1] == k - 1
    blocks.extend(('col', a0, 0) for a0 in range(singles[0], k, rows))
    flat = []
    for kind, a, b in blocks:
        flat.extend([a * k + b + i for i in range(rows)] if kind == 'row' else [(a + i) * k + b for i in range(rows)])
    need = {(a, b) for a in range(k) for b in range(k) if (a + 1) * (b + 1) <= k}
    assert need <= {(f // k, f % k) for f in flat} and len(set(flat)) == len(flat)
    return blocks, np.array(flat, np.int32)[:, None]


def _take_rows(tbl, idx):
    out = jnp.zeros(idx.shape, tbl.dtype)
    for a in range(tbl.shape[0]):
        out = out + jnp.where(idx == a, tbl[a:a + 1, :], 0)
    return out


def _peer_score_kernel(x_ref, g_ref, wq_ref, sk_ref, cf_ref, h_ref, eid_ref, gw_ref, hb_scr,
                       eid_scr, gw_scr, sc_scr, *, topk, nkeys, blocks):
    head = pl.program_id(1)

    @pl.when(head == 0)
    def _():
        x = x_ref[...]
        ms = jnp.mean(x * x, axis=-1, keepdims=True)
        hn = x * lax.rsqrt(ms + EPS) * g_ref[...]
        for r in range(ROW_TILES):
            h_ref[pl.ds(r, hn.shape[0], stride=ROW_TILES), :] = hn[:, r * LANES:(r + 1) * LANES]
        hb_scr[...] = hn.astype(BF16)

    qb = _dot(hb_scr[...], wq_ref[...]).astype(BF16)
    hq = qb.shape[1] // 2
    sc_scr[0] = _dot_nt(sk_ref[0], qb[:, :hq])
    sc_scr[1] = _dot_nt(sk_ref[1], qb[:, hq:])
    brows = cf_ref.shape[0] // len(blocks)
    out_rows = pl.ds(pl.multiple_of(head * topk, topk), topk)
    width = min(LANES, qb.shape[0])

    def column_group(c):
        cols = slice(c * width, (c + 1) * width)
        sv0, si0 = _topk_rows(sc_scr[0, :, cols], topk)
        sv1, si1 = _topk_rows(sc_scr[1, :, cols], topk)
        cand = jnp.concatenate(
            [sv0[a:a + 1, :] + sv1[b:b + brows, :] if kind == 'row' else sv0[a:a + brows, :] + sv1[b:b + 1, :]
             for kind, a, b in blocks], axis=0)
        cv, ci = _topk_rows(cand, topk, cf_ref[:, cols])
        e1 = _take_rows(si0, ci // topk)
        e2 = _take_rows(si1, ci % topk)
        eid_scr[out_rows, cols] = ((e1 * nkeys + e2) * HALF_ROWS).astype(F32)
        e = jnp.exp(cv - cv[0:1, :])
        gw_scr[out_rows, cols] = e / jnp.sum(e, axis=0, keepdims=True)

    for c in range(qb.shape[0] // width):
        column_group(c)

    @pl.when(head == pl.num_programs(1) - 1)
    def _():
        eid_ref[...] = eid_scr[...].T.astype(I32)
        gw_ref[...] = gw_scr[...].T


def _peer_score(x, g, wq_bf16, sk_bf16, tm):
    n, d = x.shape
    nkeys, khalf = sk_bf16.shape[1], sk_bf16.shape[2]
    nheads = wq_bf16.shape[1] // (2 * khalf)
    k = PEER_TOPK
    blocks, flat = _staircase_blocks(k)
    cf = jnp.asarray(np.broadcast_to(flat, (flat.shape[0], tm)))
    xs = pl.BlockSpec((tm, d), lambda i, h: (i, 0))
    cs = pl.BlockSpec(cf.shape, lambda i, h: (0, 0))
    return pl.pallas_call(
        functools.partial(_peer_score_kernel, topk=k, nkeys=nkeys, blocks=tuple(blocks)),
        grid=(n // tm, nheads),
        in_specs=[xs, pl.BlockSpec((1, d), lambda i, h: (0, 0)),
                  pl.BlockSpec((d, 2 * khalf), lambda i, h: (0, h)),
                  pl.BlockSpec((2, nkeys, khalf), lambda i, h: (0, 0, 0)), cs],
        out_specs=[pl.BlockSpec((tm * ROW_TILES, LANES), lambda i, h: (i, 0)),
                   pl.BlockSpec((tm, nheads * k), lambda i, h: (i, 0)),
                   pl.BlockSpec((tm, nheads * k), lambda i, h: (i, 0))],
        out_shape=[jax.ShapeDtypeStruct((n * ROW_TILES, LANES), F32),
                   jax.ShapeDtypeStruct((n, nheads * k), I32), jax.ShapeDtypeStruct((n, nheads * k), F32)],
        scratch_shapes=[pltpu.VMEM((tm, d), BF16), pltpu.VMEM((nheads * k, tm), F32),
                        pltpu.VMEM((nheads * k, tm), F32), pltpu.VMEM((2, nkeys, tm), F32)],
        compiler_params=_cparams("parallel", "arbitrary"),
        name="peer_score",
    )(x, g.reshape(1, d), wq_bf16, sk_bf16, cf)


def _pack_table(tab):
    e, d = tab.shape
    assert d == ROW_TILES * LANES
    te = _row_tile(e, 512)
    return pl.pallas_call(
        _pack_kernel,
        grid=(e // te,),
        in_specs=[pl.BlockSpec((te, d), lambda i: (i, 0))],
        out_specs=pl.BlockSpec((te * HALF_ROWS, LANES), lambda i: (i, 0)),
        out_shape=jax.ShapeDtypeStruct((e * HALF_ROWS, LANES), I32),
        compiler_params=_cparams("parallel"),
        name="pack_table",
    )(tab)


def _pack_kernel(t_ref, o_ref):
    te = t_ref.shape[0]
    x = t_ref[...]
    for s in range(HALF_ROWS):
        even = x[:, (2 * s) * LANES:(2 * s + 1) * LANES]
        odd = lax.bitcast_convert_type(x[:, (2 * s + 1) * LANES:(2 * s + 2) * LANES], I32)
        low = lax.shift_right_logical(lax.bitcast_convert_type(even.astype(BF16).astype(F32), I32), 16)
        sign = odd & jnp.int32(-2 ** 31)
        mag = (odd & jnp.int32(2 ** 31 - 1)) + jnp.int32(0x8000)
        high = jnp.where(mag >= low, (mag - low) >> 16, 0)
        o_ref[pl.ds(s, te, stride=HALF_ROWS), :] = sign | (high << 16) | low


def _unpack(wd):
    even = lax.bitcast_convert_type(wd << 16, F32)
    odd = lax.bitcast_convert_type(wd, F32)
    return even, odd


def _erf(x):
    return lax.erf(x)


def _token_tiles(ref, t):
    base = pl.multiple_of(t * ROW_TILES, ROW_TILES)
    return ref[pl.ds(base, HALF_ROWS, stride=2), :], ref[pl.ds(base + 1, HALF_ROWS, stride=2), :]


PEER_U_SET = 4


def _peer_u_kernel(eid_ref, h_ref, gw_ref, tab_ref, w_ref, pa, pb, q_scr, act_scr, *, tg, npairs):
    ones = jnp.ones((LANES, LANES), BF16)
    row = lax.broadcasted_iota(I32, (npairs, LANES), 0)
    eye = (row == lax.broadcasted_iota(I32, (npairs, LANES), 1)).astype(F32)
    ns = PEER_U_SET

    def products(t, p_scr, qset, k):
        he, ho = _token_tiles(h_ref, t)
        for j in range(npairs):
            e4 = pl.multiple_of(eid_ref[t, j], HALF_ROWS)
            even, odd = _unpack(tab_ref[pl.ds(e4, HALF_ROWS), :])
            p_scr[j * HALF_ROWS:(j + 1) * HALF_ROWS, :] = even * he + odd * ho
        q = p_scr[pl.ds(0, npairs, stride=HALF_ROWS), :]
        for s in range(1, HALF_ROWS):
            q = q + p_scr[pl.ds(s, npairs, stride=HALF_ROWS), :]
        hi, lo = _split_bf16(q)
        q_scr[qset, 2 * k * npairs:(2 * k + 1) * npairs, :] = hi
        q_scr[qset, (2 * k + 1) * npairs:(2 * k + 2) * npairs, :] = lo

    def lane_sums(qset, tokens):
        r = _dot(q_scr[qset], ones)
        for k, t in enumerate(tokens):
            rk = r[2 * k * npairs:(2 * k + 1) * npairs, :] + r[(2 * k + 1) * npairs:(2 * k + 2) * npairs, :]
            act_scr[pl.ds(t, 1), :] = jnp.sum(rk * eye, axis=0, keepdims=True)

    q_scr[1] = jnp.zeros(q_scr.shape[1:], BF16)

    def body(i, carry):
        t0 = 2 * ns * i
        lane_sums(1, [jnp.maximum(t0 - ns + k, 0) for k in range(ns)])
        for k in range(ns):
            products(t0 + k, (pa, pb)[k % 2], 0, k)
        lane_sums(0, [t0 + k for k in range(ns)])
        for k in range(ns):
            products(t0 + ns + k, (pa, pb)[k % 2], 1, k)
        return carry

    lax.fori_loop(0, tg // (2 * ns), body, 0)
    lane_sums(1, [tg - ns + k for k in range(ns)])
    act = act_scr[...]
    gelu = 0.5 * act * (1.0 + _erf(act * (2.0 ** -0.5)))
    w_ref[...] = gw_ref[...] * gelu


def _peer_u(eid, h, gw, tab_packed, tg):
    n, npairs = eid.shape
    assert npairs == LANES and h.shape == (n * ROW_TILES, LANES) and tg % (2 * PEER_U_SET) == 0
    tspec = pl.BlockSpec((tg, npairs), lambda i: (i, 0))
    pshape = pltpu.VMEM((npairs * HALF_ROWS, LANES), F32)
    qshape = pltpu.VMEM((2, 2 * PEER_U_SET * npairs, LANES), BF16)
    return pl.pallas_call(
        functools.partial(_peer_u_kernel, tg=tg, npairs=npairs),
        grid=(n // tg,),
        in_specs=[pl.BlockSpec((tg, npairs), lambda i: (i, 0), memory_space=pltpu.SMEM),
                  pl.BlockSpec((tg * ROW_TILES, LANES), lambda i: (i, 0)), tspec,
                  pl.BlockSpec(memory_space=pltpu.VMEM)],
        out_specs=tspec,
        out_shape=jax.ShapeDtypeStruct((n, npairs), F32),
        scratch_shapes=[pshape, pshape, qshape, pltpu.VMEM((tg, npairs), F32)],
        compiler_params=_cparams("arbitrary"),
        name="peer_u",
    )(eid, h, gw, tab_packed)


PEER_V_ACCUMULATORS = 2


def _peer_v_kernel(eid_ref, w_ref, tab_ref, o_ref, *, tg, npairs):
    nacc = PEER_V_ACCUMULATORS

    def token(t):
        acc_e = [jnp.zeros((HALF_ROWS, LANES), F32) for _ in range(nacc)]
        acc_o = [jnp.zeros((HALF_ROWS, LANES), F32) for _ in range(nacc)]
        for j in range(npairs):
            e4 = pl.multiple_of(eid_ref[t, j], HALF_ROWS)
            even, odd = _unpack(tab_ref[pl.ds(e4, HALF_ROWS), :])
            wt = w_ref[t, j]
            acc_e[j % nacc] = acc_e[j % nacc] + wt * even
            acc_o[j % nacc] = acc_o[j % nacc] + wt * odd
        base = pl.multiple_of(t * ROW_TILES, ROW_TILES)
        o_ref[pl.ds(base, HALF_ROWS, stride=2), :] = functools.reduce(lambda a, b: a + b, acc_e)
        o_ref[pl.ds(base + 1, HALF_ROWS, stride=2), :] = functools.reduce(lambda a, b: a + b, acc_o)

    def body(i, carry):
        token(2 * i)
        token(2 * i + 1)
        return carry

    lax.fori_loop(0, tg // 2, body, 0)


def _peer_v(eid, w, tab_packed, tg):
    n, npairs = eid.shape
    assert tg % 2 == 0
    sspec = pl.BlockSpec((tg, npairs), lambda i: (i, 0), memory_space=pltpu.SMEM)
    return pl.pallas_call(
        functools.partial(_peer_v_kernel, tg=tg, npairs=npairs),
        grid=(n // tg,),
        in_specs=[sspec, sspec, pl.BlockSpec(memory_space=pltpu.VMEM)],
        out_specs=pl.BlockSpec((tg * ROW_TILES, LANES), lambda i: (i, 0)),
        out_shape=jax.ShapeDtypeStruct((n * ROW_TILES, LANES), F32),
        compiler_params=_cparams("arbitrary"),
        name="peer_v",
    )(eid, w, tab_packed)


def _final_kernel(x_ref, p_ref, g_ref, y_ref):
    tm = x_ref.shape[0]
    p = jnp.concatenate([p_ref[pl.ds(r, tm, stride=ROW_TILES), :] for r in range(ROW_TILES)], axis=1)
    x = x_ref[...] + p
    ms = jnp.mean(x * x, axis=-1, keepdims=True)
    y_ref[...] = x * lax.rsqrt(ms + EPS) * g_ref[...]


def _final(x, p, g, tm):
    n, d = x.shape
    xs = pl.BlockSpec((tm, d), lambda i: (i, 0))
    return pl.pallas_call(
        _final_kernel,
        grid=(n // tm,),
        in_specs=[xs, pl.BlockSpec((tm * ROW_TILES, LANES), lambda i: (i, 0)), pl.BlockSpec((1, d), lambda i: (0, 0))],
        out_specs=xs,
        out_shape=jax.ShapeDtypeStruct((n, d), F32),
        compiler_params=_cparams("parallel"),
        name="final_norm",
    )(x, p, g.reshape(1, d))


def _row_tile(n, pref):
    t = min(pref, n)
    assert n % t == 0, (n, t)
    return t


def _stream(x, w_in_bf, g_mix, dims):
    b, t, d = x.shape
    wsb, wrt, wmm, hd = dims
    scale = hd ** -0.5
    o = 3 * wsb
    outs = [(0, wsb, scale, BF16, 0),
            (wsb, 2 * wsb, 1.0, F32, wsb // hd), (2 * wsb, o, 1.0, F32, wsb // hd),
            (wsb, 2 * wsb, 1.0, BF16, 0), (2 * wsb, o, 1.0, BF16, 0),
            (o, o + 4 * wrt, 1.0, F32, 0),
            (o + 4 * wrt, o + 4 * wrt + wmm, scale, BF16, 0)]
    res = _norm_proj(x.reshape(b * t, d), g_mix, w_in_bf, outs, _row_tile(b * t, 512))
    return [r.reshape(b, t, r.shape[1]) if r.shape[0] == b * t else r for r in res]


def _peer_and_final(x1, g_ffn, wq_bf, sk_bf, u_packed, v_packed, g_final):
    n = x1.shape[0]
    h, eid, gw = _peer_score(x1, g_ffn, wq_bf, sk_bf, _row_tile(n, 512))
    tg = _row_tile(n, 128)
    w = _peer_u(eid, h, gw, u_packed, tg)
    p = _peer_v(eid, w, v_packed, tg)
    return _final(x1, p, g_final, _row_tile(n, 512))


def kernel(x_prompt, x_sample, mem_prompt, cache_sb_k, cache_sb_v, state_ret, cache_mem_k, cache_mem_v,
           g_mix, w_in, w_out, g_sb_out, g_ret_out, g_mem_out, g_mem, w_mem_kv,
           g_ffn, w_peer_q, peer_sub_keys, peer_u, peer_v, g_final):
    depth = w_in.shape[0]
    assert depth == 1, "single-layer step"
    bp, tp, d = x_prompt.shape
    bs, ts, _ = x_sample.shape
    _, _, past, sb_heads, hd = cache_sb_k.shape
    ret_heads = state_ret.shape[2]
    mem_heads = cache_mem_k.shape[3]
    nmem = mem_prompt.shape[1]
    wsb, wrt, wmm = sb_heads * hd, ret_heads * hd, mem_heads * hd
    assert hd == 64 and wrt == wmm and d == 2 * HALF_ROWS * LANES
    l = 0
    w_in_bf = w_in[l].astype(BF16)
    wo_bf = w_out[l].astype(BF16)
    dims = (wsb, wrt, wmm, hd)

    q_p, k_p, v_p, kb_p, vb_p, ret_p, qm_p = _stream(x_prompt, w_in_bf, g_mix[l], dims)
    sb_p = _sb_prompt(q_p, kb_p, vb_p, hd)
    zeros_state = jnp.zeros((bp, wrt, wrt), F32)
    ro_p, st_p = _retention(ret_p, jnp.arange(tp, dtype=I32), zeros_state, ret_heads, hd, _row_tile(tp, 512))
    mk_p, mv_p = _norm_proj(mem_prompt.reshape(bp * nmem, d), g_mem[l], w_mem_kv[l].astype(BF16),
                            [(0, wmm, 1.0, F32, 0), (wmm, 2 * wmm, 1.0, F32, 0)], _row_tile(bp * nmem, 512))
    mk_p = mk_p.reshape(bp, nmem, wmm)
    mv_p = mv_p.reshape(bp, nmem, wmm)
    x1_p = _merge(x_prompt, sb_p, ro_p, ret_p, qm_p, mk_p, mv_p, g_sb_out[l], g_ret_out[l], g_mem_out[l],
                  wo_bf, hd, _row_tile(tp, 256))

    q_s, k_s, v_s, kb_s, vb_s, ret_s, qm_s = _stream(x_sample, w_in_bf, g_mix[l], dims)
    sb_s = _sb_sample(q_s, kb_s, vb_s, cache_sb_k[l].reshape(bs, past, wsb), cache_sb_v[l].reshape(bs, past, wsb), hd)
    s0 = _block_diag_state(state_ret[l].astype(F32), ret_heads, hd)
    ro_s, st_s = _retention(ret_s, past + jnp.arange(ts, dtype=I32), s0, ret_heads, hd, _row_tile(ts, 512))
    x1_s = _merge(x_sample, sb_s, ro_s, ret_s, qm_s, cache_mem_k[l].reshape(bs, nmem, wmm),
                  cache_mem_v[l].reshape(bs, nmem, wmm), g_sb_out[l], g_ret_out[l], g_mem_out[l],
                  wo_bf, hd, _row_tile(ts, 256))

    wq_bf = w_peer_q[l].astype(BF16)
    sk_bf = peer_sub_keys[l].astype(BF16)
    u_packed = _pack_table(peer_u[l])
    v_packed = _pack_table(peer_v[l])
    y_p = _peer_and_final(x1_p.reshape(bp * tp, d), g_ffn[l], wq_bf, sk_bf, u_packed, v_packed, g_final)
    y_s = _peer_and_final(x1_s.reshape(bs * ts, d), g_ffn[l], wq_bf, sk_bf, u_packed, v_packed, g_final)

    return (y_p.reshape(bp, tp, d), y_s.reshape(bs, ts, d),
            k_p.reshape(1, bp, tp, sb_heads, hd), v_p.reshape(1, bp, tp, sb_heads, hd),
            _diag_blocks(st_p, ret_heads, hd)[None],
            mk_p.reshape(1, bp, nmem, mem_heads, hd), mv_p.reshape(1, bp, nmem, mem_heads, hd),
            k_s.reshape(1, bs, ts, sb_heads, hd), v_s.reshape(1, bs, ts, sb_heads, hd),
            _diag_blocks(st_s, ret_heads, hd)[None].astype(state_ret.dtype))
```

```python
import functools

import numpy as np
import jax
import jax.numpy as jnp
from jax import lax
from jax.experimental import pallas as pl
from jax.experimental.pallas import tpu as pltpu

F32 = jnp.float32
BF16 = jnp.bfloat16
I32 = jnp.int32

EPS = 1e-6
ROPE_BASE = 10000.0
PEER_TOPK = 16
RET_CHUNK = 64
LANES = 128
ROW_TILES = 8
HALF_ROWS = 4
VMEM_LIMIT = 56 * 1024 * 1024
SB_DEAD_LOG = -105.0


def _cparams(*sem):
    return pltpu.CompilerParams(dimension_semantics=sem, vmem_limit_bytes=VMEM_LIMIT)


def _dot(a, b):
    return jnp.dot(a, b, preferred_element_type=F32)


def _dot_nt(a, b):
    return lax.dot_general(a, b, (((1,), (1,)), ((), ())), preferred_element_type=F32)


def _split_bf16(x):
    hi = x.astype(BF16)
    lo = (x - hi.astype(F32)).astype(BF16)
    return hi, lo


def _dot_split(x, m_bf16):
    hi, lo = _split_bf16(x)
    return _dot(hi, m_bf16) + _dot(lo, m_bf16)


def _norm_proj_kernel(x_ref, g_ref, w_ref, *out_refs, cols):
    x = x_ref[...]
    ms = jnp.mean(x * x, axis=-1, keepdims=True)
    h = x * lax.rsqrt(ms + EPS) * g_ref[...]
    p = _dot(h.astype(BF16), w_ref[...])
    tm = x.shape[0]
    for ref, (lo, hi, scale, heads) in zip(out_refs, cols):
        v = p[:, lo:hi]
        if scale != 1.0:
            v = v * scale
        if heads:
            hd = (hi - lo) // heads
            for h in range(heads):
                ref[pl.ds(h, tm, stride=heads), :] = v[:, h * hd:(h + 1) * hd].astype(ref.dtype)
        else:
            ref[...] = v.astype(ref.dtype)


def _norm_proj(x, g, w_bf16, outs, tm):
    n, d = x.shape
    wcols = w_bf16.shape[1]
    shape = lambda rows, lo, hi, hds: (rows * hds, (hi - lo) // hds) if hds else (rows, hi - lo)
    out_shape = [jax.ShapeDtypeStruct(shape(n, lo, hi, hds), dt) for lo, hi, _, dt, hds in outs]
    out_specs = [pl.BlockSpec(shape(tm, lo, hi, hds), lambda i: (i, 0)) for lo, hi, _, _, hds in outs]
    return pl.pallas_call(
        functools.partial(_norm_proj_kernel, cols=tuple((lo, hi, sc, hds) for lo, hi, sc, _, hds in outs)),
        grid=(n // tm,),
        in_specs=[pl.BlockSpec((tm, d), lambda i: (i, 0)),
                  pl.BlockSpec((1, d), lambda i: (0, 0)),
                  pl.BlockSpec((d, wcols), lambda i: (0, 0))],
        out_specs=out_specs,
        out_shape=out_shape,
        compiler_params=_cparams("parallel"),
        name="norm_proj",
    )(x, g.reshape(1, d), w_bf16)


def _suffix_matrix(tk):
    j = np.arange(2 * tk)[:, None]
    s = np.arange(2 * tk)[None, :]
    return jnp.asarray(((j // tk == s // tk) & (j >= s)).astype(np.float32), dtype=BF16)


def _sb_step(q2, kb, vb, u, carry, tk, hd, valid):
    c0, c1, acc = carry
    tq = q2.shape[0]
    lane = lax.broadcasted_iota(I32, kb.shape, 1)
    first = lane < hd
    zero = jnp.zeros_like(kb)
    kk = jnp.concatenate([jnp.where(first, kb, zero), jnp.where(first, zero, kb)], axis=0)
    vv = jnp.concatenate([jnp.where(first, vb, zero), jnp.where(first, zero, vb)], axis=0)
    z = _dot_nt(q2, kk)
    sp = jnp.log(1.0 + jnp.exp(-jnp.abs(z)))
    lk = jnp.minimum(-z, 0.0) - sp
    if valid is not None:
        lk = jnp.where(valid, lk, 0.0)
    m = _dot_split(lk, u)
    c = jnp.concatenate([jnp.broadcast_to(c0, (tq, tk)), jnp.broadcast_to(c1, (tq, tk))], axis=1)
    a = jnp.exp(z + c + m)
    if valid is not None:
        a = jnp.where(valid, a, 0.0)
    acc = acc + _dot(a.astype(BF16), vv)
    c0 = c0 + jnp.sum(lk[:, :tk], axis=-1, keepdims=True)
    c1 = c1 + jnp.sum(lk[:, tk:], axis=-1, keepdims=True)
    return c0, c1, acc


def _sb_alive(carries):
    m = functools.reduce(jnp.maximum, [jnp.maximum(c0, c1) for c0, c1, _ in carries])
    return jnp.max(m) > SB_DEAD_LOG


def _sb_far_blocks(qs, block, nblk, u, carries, tk, hd, first=0):
    n = len(qs)
    flatten = lambda cs: tuple(x for c in cs for x in c)
    unflatten = lambda flat: [tuple(flat[3 * p:3 * p + 3]) for p in range(n)]

    def cond(state):
        return jnp.logical_and(state[0] < nblk, state[1])

    def body(state):
        i = state[0]
        kvs = block(i)
        cs = [_sb_step(qs[p], kvs[p][0], kvs[p][1], u, c, tk, hd, None) for p, c in enumerate(unflatten(state[2:]))]
        return (i + 1, _sb_alive(cs)) + flatten(cs)

    state = lax.while_loop(cond, body, (jnp.int32(first), _sb_alive(carries)) + flatten(carries))
    return unflatten(state[2:])


SB_PAIRS_PER_STEP = 2


def _sb_prompt_kernel(q_ref, k_ref, v_ref, u_ref, o_ref, *, tq, tk, hd, npairs):
    q0 = pl.program_id(2) * tq
    lanes = [slice(p * LANES, (p + 1) * LANES) for p in range(npairs)]
    qs = [q_ref[0, :, ln] for ln in lanes]
    u = u_ref[...]
    row = lax.broadcasted_iota(I32, (tq, 2 * tk), 0)
    koff = lax.broadcasted_iota(I32, (tq, 2 * tk), 1) & (tk - 1)
    carries = [(jnp.zeros((tq, 1), F32), jnp.zeros((tq, 1), F32), jnp.zeros((tq, LANES), F32)) for _ in lanes]
    nd = tq // tk
    for d in range(nd + 1):
        off = (nd - 1 - d) * tk
        limit = row if d < nd else row + jnp.where(q0 >= tk, 0, -4 * tq)
        valid = (koff + off) < limit
        ks = pl.multiple_of(jnp.maximum(q0 + off, 0), tk)
        carries = [_sb_step(qs[p], k_ref[0, pl.ds(ks, tk), ln], v_ref[0, pl.ds(ks, tk), ln], u, carries[p], tk, hd, valid)
                   for p, ln in enumerate(lanes)]

    def block(i):
        ks = pl.multiple_of(q0 - (i + 1) * tk, tk)
        return [(k_ref[0, pl.ds(ks, tk), ln], v_ref[0, pl.ds(ks, tk), ln]) for ln in lanes]

    carries = _sb_far_blocks(qs, block, q0 // tk, u, carries, tk, hd, first=1)
    for p, ln in enumerate(lanes):
        o_ref[0, :, ln] = carries[p][2]


def _sb_prompt(q, k, v, hd, tq=256, tk=128):
    b, t, w = q.shape
    npairs = SB_PAIRS_PER_STEP
    wb = npairs * LANES
    assert w % wb == 0
    return pl.pallas_call(
        functools.partial(_sb_prompt_kernel, tq=tq, tk=tk, hd=hd, npairs=npairs),
        grid=(b, w // wb, t // tq),
        in_specs=[pl.BlockSpec((1, tq, wb), lambda bi, hp, qi: (bi, qi, hp)),
                  pl.BlockSpec((1, t, wb), lambda bi, hp, qi: (bi, 0, hp)),
                  pl.BlockSpec((1, t, wb), lambda bi, hp, qi: (bi, 0, hp)),
                  pl.BlockSpec((2 * tk, 2 * tk), lambda bi, hp, qi: (0, 0))],
        out_specs=pl.BlockSpec((1, tq, wb), lambda bi, hp, qi: (bi, qi, hp)),
        out_shape=jax.ShapeDtypeStruct((b, t, w), F32),
        compiler_params=_cparams("parallel", "parallel", "arbitrary"),
        name="sb_prompt",
    )(q, k, v, _suffix_matrix(tk))


def _sb_sample_kernel(q_ref, kn_ref, vn_ref, kc_ref, vc_ref, un_ref, u_ref, o_ref, *, ts, tk, hd, past):
    q2 = q_ref[0]
    row = lax.broadcasted_iota(I32, (ts, 2 * ts), 0)
    koff = lax.broadcasted_iota(I32, (ts, 2 * ts), 1) & (ts - 1)
    carry = (jnp.zeros((ts, 1), F32), jnp.zeros((ts, 1), F32), jnp.zeros((ts, LANES), F32))
    carry = _sb_step(q2, kn_ref[0], vn_ref[0], un_ref[...], carry, ts, hd, koff < row)
    u = u_ref[...]

    def block(i):
        ks = pl.multiple_of(past - (i + 1) * tk, tk)
        return kc_ref[0, pl.ds(ks, tk), :].astype(BF16), vc_ref[0, pl.ds(ks, tk), :].astype(BF16)

    (carry,) = _sb_far_blocks([q2], lambda i: [block(i)], past // tk, u, [carry], tk, hd)
    o_ref[0] = carry[2]


def _sb_sample(q, kn, vn, kc, vc, hd, tk=128):
    b, ts, w = q.shape
    past = kc.shape[1]
    new_spec = pl.BlockSpec((1, ts, LANES), lambda bi, hp: (bi, 0, hp))
    cache_spec = pl.BlockSpec((1, past, LANES), lambda bi, hp: (bi, 0, hp))
    return pl.pallas_call(
        functools.partial(_sb_sample_kernel, ts=ts, tk=tk, hd=hd, past=past),
        grid=(b, w // LANES),
        in_specs=[new_spec, new_spec, new_spec, cache_spec, cache_spec,
                  pl.BlockSpec((2 * ts, 2 * ts), lambda bi, hp: (0, 0)),
                  pl.BlockSpec((2 * tk, 2 * tk), lambda bi, hp: (0, 0))],
        out_specs=new_spec,
        out_shape=jax.ShapeDtypeStruct((b, ts, w), F32),
        compiler_params=_cparams("parallel", "parallel"),
        name="sb_sample",
    )(q, kn, vn, kc, vc, _suffix_matrix(ts), _suffix_matrix(tk))


def _ret_kernel(q_ref, k_ref, v_ref, cos_ref, sin_ref, dec_ref, qd_ref, kd_ref, gl_ref, bd_ref, s0_ref,
                o_ref, st_ref, s_scr, *, tc, nh, hd, k_scale):
    t = pl.program_id(1)
    w = nh * hd
    L = RET_CHUNK

    @pl.when(t == 0)
    def _():
        s_scr[...] = s0_ref[0]

    lane = lax.broadcasted_iota(I32, (L, w), 1)
    first_half = (lane & (hd - 1)) < hd // 2
    head = lane // hd

    def rope(x, cos, sin):
        swapped = jnp.where(first_half, pltpu.roll(x, w - hd // 2, 1), pltpu.roll(x, hd // 2, 1))
        return x * cos + swapped * sin

    s = s_scr[...]
    for ci in range(tc // L):
        sl = slice(ci * L, (ci + 1) * L)
        cos = cos_ref[sl, :]
        sin = sin_ref[sl, :]
        q = rope(q_ref[0, sl, :], cos, sin)
        k = rope(k_ref[0, sl, :], cos, sin) * k_scale
        qb = q.astype(BF16)
        kb = k.astype(BF16)
        vb = v_ref[0, sl, :].astype(BF16)
        zero = jnp.zeros_like(qb)
        out = _dot(qb, s.astype(BF16)) * qd_ref[...]
        for h in range(nh):
            mh = head == h
            sc = _dot_nt(jnp.where(mh, qb, zero), kb)
            p = (sc * dec_ref[h]).astype(BF16)
            out = out + _dot(p, jnp.where(mh, vb, zero))
        o_ref[0, sl, :] = out
        kd = (k * kd_ref[...]).astype(BF16)
        outer = lax.dot_general(kd, vb, (((0,), (0,)), ((), ())), preferred_element_type=F32)
        s = gl_ref[...] * s + bd_ref[...] * outer
    s_scr[...] = s

    @pl.when(t == pl.num_programs(1) - 1)
    def _():
        st_ref[0] = s


def _retention(ret, pos, s0, nh, hd, tc):
    b, t, _ = ret.shape
    w = nh * hd
    L = RET_CHUNK
    half = hd // 2
    freqs = ROPE_BASE ** (-jnp.arange(half, dtype=F32) / half)
    ang = pos.astype(F32)[:, None] * freqs[None, :]
    cos = jnp.tile(jnp.cos(ang), (1, 2 * nh))
    sin = jnp.tile(jnp.concatenate([-jnp.sin(ang), jnp.sin(ang)], axis=1), (1, nh))
    lg = jnp.log1p(-jnp.exp2(-5.0 - jnp.arange(nh, dtype=F32)))
    idx = jnp.arange(L, dtype=F32)
    diff = idx[:, None] - idx[None, :]
    causal = diff >= 0
    dec = jnp.where(causal[None], jnp.exp(jnp.where(causal, diff, 0.0)[None] * lg[:, None, None]), 0.0)
    lg_lane = jnp.repeat(lg, hd)[None, :]
    qd = jnp.exp((idx + 1.0)[:, None] * lg_lane)
    kd = jnp.exp((L - 1.0 - idx)[:, None] * lg_lane)
    gl = jnp.exp(L * lg_lane)
    hid = jnp.arange(w) // hd
    bd = (hid[:, None] == hid[None, :]).astype(F32)
    tok = lambda j: pl.BlockSpec((1, tc, w), lambda bi, ti: (bi, ti, j))
    tab = pl.BlockSpec((tc, w), lambda bi, ti: (ti, 0))
    const = lambda shape: pl.BlockSpec(shape, lambda bi, ti: (0,) * len(shape))
    state = pl.BlockSpec((1, w, w), lambda bi, ti: (bi, 0, 0))
    return pl.pallas_call(
        functools.partial(_ret_kernel, tc=tc, nh=nh, hd=hd, k_scale=hd ** -0.5),
        grid=(b, t // tc),
        in_specs=[tok(0), tok(1), tok(2), tab, tab, const((nh, L, L)), const((L, w)), const((L, w)),
                  const((1, w)), const((w, w)), state],
        out_specs=[pl.BlockSpec((1, tc, w), lambda bi, ti: (bi, ti, 0)), state],
        out_shape=[jax.ShapeDtypeStruct((b, t, w), F32), jax.ShapeDtypeStruct((b, w, w), F32)],
        scratch_shapes=[pltpu.VMEM((w, w), F32)],
        compiler_params=_cparams("parallel", "arbitrary"),
        name="retention",
    )(ret, ret, ret, cos, sin, dec, qd, kd, gl, bd, s0)


def _block_diag_state(st, nh, hd):
    b = st.shape[0]
    eye = jnp.eye(nh, dtype=st.dtype)
    return jnp.einsum('bhde,hg->bhdge', st, eye).reshape(b, nh * hd, nh * hd)


def _diag_blocks(s, nh, hd):
    b = s.shape[0]
    s5 = s.reshape(b, nh, hd, nh, hd)
    return jnp.stack([s5[:, h, :, h, :] for h in range(nh)], axis=1)


def _head_mean_matrix(w, hd):
    hid = np.arange(w) // hd
    return jnp.asarray((hid[:, None] == hid[None, :]).astype(np.float32) / hd, dtype=BF16)


def _headnorm(x, g_mat, gain):
    ms = _dot_split(x * x, g_mat)
    return x * lax.rsqrt(ms + EPS) * gain


def _merge_kernel(x_ref, sb_ref, ret_ref, gate_ref, qm_ref, mk_ref, mv_ref, gsb_ref, gret_ref, gmo_ref,
                  msb_ref, mrt_ref, wo_ref, x1_ref, *, nh_mem, hd):
    wsb = sb_ref.shape[2]
    wrt = ret_ref.shape[2]
    qm = qm_ref[0]
    mkb = mk_ref[0].astype(BF16)
    mvb = mv_ref[0].astype(BF16)
    head_q = lax.broadcasted_iota(I32, qm.shape, 1) // hd
    head_m = lax.broadcasted_iota(I32, mkb.shape, 1) // hd
    mem_o = jnp.zeros(qm.shape, F32)
    for h in range(nh_mem):
        s = _dot_nt(jnp.where(head_q == h, qm, jnp.zeros_like(qm)), mkb)
        e = jnp.exp(s - jnp.max(s, axis=-1, keepdims=True))
        p = e / jnp.sum(e, axis=-1, keepdims=True)
        mem_o = mem_o + _dot(p.astype(BF16), jnp.where(head_m == h, mvb, jnp.zeros_like(mvb)))
    sb_n = _headnorm(sb_ref[0], msb_ref[...], gsb_ref[...])
    gate = gate_ref[0]
    ret_n = _headnorm(ret_ref[0], mrt_ref[...], gret_ref[...]) * (gate / (1.0 + jnp.exp(-gate)))
    mem_n = _headnorm(mem_o, mrt_ref[...], gmo_ref[...])
    y = (_dot(sb_n.astype(BF16), wo_ref[0:wsb, :])
         + _dot(ret_n.astype(BF16), wo_ref[wsb:wsb + wrt, :])
         + _dot(mem_n.astype(BF16), wo_ref[wsb + wrt:, :]))
    x1_ref[0] = x_ref[0] + y


def _merge(x, sb_o, ret_o, ret, qm, mk, mv, g_sb, g_ret, g_mo, wo_bf16, hd, tm):
    b, t, d = x.shape
    wsb = sb_o.shape[2]
    wrt = ret_o.shape[2]
    nmem = mk.shape[1]
    tok = lambda wd, j=0: pl.BlockSpec((1, tm, wd), lambda bi, ti: (bi, ti, j))
    const = lambda shape: pl.BlockSpec(shape, lambda bi, ti: (0,) * len(shape))
    memspec = pl.BlockSpec((1, nmem, wrt), lambda bi, ti: (bi, 0, 0))
    return pl.pallas_call(
        functools.partial(_merge_kernel, nh_mem=wrt // hd, hd=hd),
        grid=(b, t // tm),
        in_specs=[tok(d), tok(wsb), tok(wrt), tok(wrt, 3), tok(wrt), memspec, memspec,
                  const((1, wsb)), const((1, wrt)), const((1, wrt)),
                  const((wsb, wsb)), const((wrt, wrt)), const((d, d))],
        out_specs=tok(d),
        out_shape=jax.ShapeDtypeStruct((b, t, d), F32),
        compiler_params=_cparams("parallel", "parallel"),
        name="merge",
    )(x, sb_o, ret_o, ret, qm, mk, mv, g_sb.reshape(1, wsb), g_ret.reshape(1, wrt), g_mo.reshape(1, wrt),
      _head_mean_matrix(wsb, hd), _head_mean_matrix(wrt, hd), wo_bf16)


def _topk_rows(s, k, rid=None):
    if rid is None:
        rid = lax.broadcasted_iota(I32, s.shape, 0)
    big = jnp.int32(2 ** 30)
    vals, idxs = [], []
    for _ in range(k):
        m = jnp.max(s, axis=0, keepdims=True)
        i = jnp.min(jnp.where(s == m, rid, big), axis=0, keepdims=True)
        vals.append(m)
        idxs.append(i)
        s = jnp.where(rid == i, -jnp.inf, s)
    return jnp.concatenate(vals, axis=0), jnp.concatenate(idxs, axis=0)


def _staircase_blocks(k, rows=8):
    blocks = []
    singles = [a for a in range(k) if k // (a + 1) == 1]
    for a in range(k):
        nb = k // (a + 1)
        if nb > 1:
            blocks.extend(('row', a, b0) for b0 in range(0, nb, rows))
    assert singles and singles[0] % rows == 0 and len(singles) % rows == 0 and singles[-1] == k - 1
    blocks.extend(('col', a0, 0) for a0 in range(singles[0], k, rows))
    flat = []
    for kind, a, b in blocks:
        flat.extend([a * k + b + i for i in range(rows)] if kind == 'row' else [(a + i) * k + b for i in range(rows)])
    need = {(a, b) for a in range(k) for b in range(k) if (a + 1) * (b + 1) <= k}
    assert need <= {(f // k, f % k) for f in flat} and len(set(flat)) == len(flat)
    return blocks, np.array(flat, np.int32)[:, None]


def _take_rows(tbl, idx):
    out = jnp.zeros(idx.shape, tbl.dtype)
    for a in range(tbl.shape[0]):
        out = out + jnp.where(idx == a, tbl[a:a + 1, :], 0)
    return out


def _peer_score_kernel(x_ref, g_ref, wq_ref, sk_ref, cf_ref, h_ref, eid_ref, gw_ref, hb_scr,
                       eid_scr, gw_scr, sc_scr, *, topk, nkeys, blocks):
    head = pl.program_id(1)

    @pl.when(head == 0)
    def _():
        x = x_ref[...]
        ms = jnp.mean(x * x, axis=-1, keepdims=True)
        hn = x * lax.rsqrt(ms + EPS) * g_ref[...]
        for r in range(ROW_TILES):
            h_ref[pl.ds(r, hn.shape[0], stride=ROW_TILES), :] = hn[:, r * LANES:(r + 1) * LANES]
        hb_scr[...] = hn.astype(BF16)

    qb = _dot(hb_scr[...], wq_ref[...]).astype(BF16)
    hq = qb.shape[1] // 2
    sc_scr[0] = _dot_nt(sk_ref[0], qb[:, :hq])
    sc_scr[1] = _dot_nt(sk_ref[1], qb[:, hq:])
    brows = cf_ref.shape[0] // len(blocks)
    out_rows = pl.ds(pl.multiple_of(head * topk, topk), topk)
    width = min(LANES, qb.shape[0])

    def column_group(c):
        cols = slice(c * width, (c + 1) * width)
        sv0, si0 = _topk_rows(sc_scr[0, :, cols], topk)
        sv1, si1 = _topk_rows(sc_scr[1, :, cols], topk)
        cand = jnp.concatenate(
            [sv0[a:a + 1, :] + sv1[b:b + brows, :] if kind == 'row' else sv0[a:a + brows, :] + sv1[b:b + 1, :]
             for kind, a, b in blocks], axis=0)
        cv, ci = _topk_rows(cand, topk, cf_ref[:, cols])
        e1 = _take_rows(si0, ci // topk)
        e2 = _take_rows(si1, ci % topk)
        eid_scr[out_rows, cols] = ((e1 * nkeys + e2) * HALF_ROWS).astype(F32)
        e = jnp.exp(cv - cv[0:1, :])
        gw_scr[out_rows, cols] = e / jnp.sum(e, axis=0, keepdims=True)

    for c in range(qb.shape[0] // width):
        column_group(c)

    @pl.when(head == pl.num_programs(1) - 1)
    def _():
        eid_ref[...] = eid_scr[...].T.astype(I32)
        gw_ref[...] = gw_scr[...].T


def _peer_score(x, g, wq_bf16, sk_bf16, tm):
    n, d = x.shape
    nkeys, khalf = sk_bf16.shape[1], sk_bf16.shape[2]
    nheads = wq_bf16.shape[1] // (2 * khalf)
    k = PEER_TOPK
    blocks, flat = _staircase_blocks(k)
    cf = jnp.asarray(np.broadcast_to(flat, (flat.shape[0], tm)))
    xs = pl.BlockSpec((tm, d), lambda i, h: (i, 0))
    cs = pl.BlockSpec(cf.shape, lambda i, h: (0, 0))
    return pl.pallas_call(
        functools.partial(_peer_score_kernel, topk=k, nkeys=nkeys, blocks=tuple(blocks)),
        grid=(n // tm, nheads),
        in_specs=[xs, pl.BlockSpec((1, d), lambda i, h: (0, 0)),
                  pl.BlockSpec((d, 2 * khalf), lambda i, h: (0, h)),
                  pl.BlockSpec((2, nkeys, khalf), lambda i, h: (0, 0, 0)), cs],
        out_specs=[pl.BlockSpec((tm * ROW_TILES, LANES), lambda i, h: (i, 0)),
                   pl.BlockSpec((tm, nheads * k), lambda i, h: (i, 0)),
                   pl.BlockSpec((tm, nheads * k), lambda i, h: (i, 0))],
        out_shape=[jax.ShapeDtypeStruct((n * ROW_TILES, LANES), F32),
                   jax.ShapeDtypeStruct((n, nheads * k), I32), jax.ShapeDtypeStruct((n, nheads * k), F32)],
        scratch_shapes=[pltpu.VMEM((tm, d), BF16), pltpu.VMEM((nheads * k, tm), F32),
                        pltpu.VMEM((nheads * k, tm), F32), pltpu.VMEM((2, nkeys, tm), F32)],
        compiler_params=_cparams("parallel", "arbitrary"),
        name="peer_score",
    )(x, g.reshape(1, d), wq_bf16, sk_bf16, cf)


def _pack_table(tab):
    e, d = tab.shape
    assert d == ROW_TILES * LANES
    te = _row_tile(e, 512)
    return pl.pallas_call(
        _pack_kernel,
        grid=(e // te,),
        in_specs=[pl.BlockSpec((te, d), lambda i: (i, 0))],
        out_specs=pl.BlockSpec((te * HALF_ROWS, LANES), lambda i: (i, 0)),
        out_shape=jax.ShapeDtypeStruct((e * HALF_ROWS, LANES), I32),
        compiler_params=_cparams("parallel"),
        name="pack_table",
    )(tab)


def _pack_kernel(t_ref, o_ref):
    te = t_ref.shape[0]
    x = t_ref[...]
    for s in range(HALF_ROWS):
        even = x[:, (2 * s) * LANES:(2 * s + 1) * LANES]
        odd = lax.bitcast_convert_type(x[:, (2 * s + 1) * LANES:(2 * s + 2) * LANES], I32)
        low = lax.shift_right_logical(lax.bitcast_convert_type(even.astype(BF16).astype(F32), I32), 16)
        sign = odd & jnp.int32(-2 ** 31)
        mag = (odd & jnp.int32(2 ** 31 - 1)) + jnp.int32(0x8000)
        high = jnp.where(mag >= low, (mag - low) >> 16, 0)
        o_ref[pl.ds(s, te, stride=HALF_ROWS), :] = sign | (high << 16) | low


def _unpack(wd):
    even = lax.bitcast_convert_type(wd << 16, F32)
    odd = lax.bitcast_convert_type(wd, F32)
    return even, odd


def _erf(x):
    return lax.erf(x)


def _token_tiles(ref, t):
    base = pl.multiple_of(t * ROW_TILES, ROW_TILES)
    return ref[pl.ds(base, HALF_ROWS, stride=2), :], ref[pl.ds(base + 1, HALF_ROWS, stride=2), :]


PEER_U_SET = 4


def _peer_u_kernel(eid_ref, h_ref, gw_ref, tab_ref, w_ref, pa, pb, q_scr, act_scr, *, tg, npairs):
    ones = jnp.ones((LANES, LANES), BF16)
    row = lax.broadcasted_iota(I32, (npairs, LANES), 0)
    eye = (row == lax.broadcasted_iota(I32, (npairs, LANES), 1)).astype(F32)
    ns = PEER_U_SET

    def products(t, p_scr, qset, k):
        he, ho = _token_tiles(h_ref, t)
        he2 = jnp.concatenate([he, he], axis=0)
        ho2 = jnp.concatenate([ho, ho], axis=0)
        for j in range(0, npairs, 2):
            wa = tab_ref[pl.ds(pl.multiple_of(eid_ref[t, j], HALF_ROWS), HALF_ROWS), :]
            wb = tab_ref[pl.ds(pl.multiple_of(eid_ref[t, j + 1], HALF_ROWS), HALF_ROWS), :]
            even, odd = _unpack(jnp.concatenate([wa, wb], axis=0))
            p_scr[j * HALF_ROWS:(j + 2) * HALF_ROWS, :] = even * he2 + odd * ho2
        q = p_scr[pl.ds(0, npairs, stride=HALF_ROWS), :]
        for s in range(1, HALF_ROWS):
            q = q + p_scr[pl.ds(s, npairs, stride=HALF_ROWS), :]
        hi, lo = _split_bf16(q)
        q_scr[qset, 2 * k * npairs:(2 * k + 1) * npairs, :] = hi
        q_scr[qset, (2 * k + 1) * npairs:(2 * k + 2) * npairs, :] = lo

    def lane_sums(qset, tokens):
        r = _dot(q_scr[qset], ones)
        for k, t in enumerate(tokens):
            rk = r[2 * k * npairs:(2 * k + 1) * npairs, :] + r[(2 * k + 1) * npairs:(2 * k + 2) * npairs, :]
            act_scr[pl.ds(t, 1), :] = jnp.sum(rk * eye, axis=0, keepdims=True)

    q_scr[1] = jnp.zeros(q_scr.shape[1:], BF16)

    def body(i, carry):
        t0 = 2 * ns * i
        lane_sums(1, [jnp.maximum(t0 - ns + k, 0) for k in range(ns)])
        for k in range(ns):
            products(t0 + k, (pa, pb)[k % 2], 0, k)
        lane_sums(0, [t0 + k for k in range(ns)])
        for k in range(ns):
            products(t0 + ns + k, (pa, pb)[k % 2], 1, k)
        return carry

    lax.fori_loop(0, tg // (2 * ns), body, 0)
    lane_sums(1, [tg - ns + k for k in range(ns)])
    act = act_scr[...]
    gelu = 0.5 * act * (1.0 + _erf(act * (2.0 ** -0.5)))
    w_ref[...] = gw_ref[...] * gelu


def _peer_u(eid, h, gw, tab_packed, tg):
    n, npairs = eid.shape
    assert npairs == LANES and h.shape == (n * ROW_TILES, LANES) and tg % (2 * PEER_U_SET) == 0
    tspec = pl.BlockSpec((tg, npairs), lambda i: (i, 0))
    pshape = pltpu.VMEM((npairs * HALF_ROWS, LANES), F32)
    qshape = pltpu.VMEM((2, 2 * PEER_U_SET * npairs, LANES), BF16)
    return pl.pallas_call(
        functools.partial(_peer_u_kernel, tg=tg, npairs=npairs),
        grid=(n // tg,),
        in_specs=[pl.BlockSpec((tg, npairs), lambda i: (i, 0), memory_space=pltpu.SMEM),
                  pl.BlockSpec((tg * ROW_TILES, LANES), lambda i: (i, 0)), tspec,
                  pl.BlockSpec(memory_space=pltpu.VMEM)],
        out_specs=tspec,
        out_shape=jax.ShapeDtypeStruct((n, npairs), F32),
        scratch_shapes=[pshape, pshape, qshape, pltpu.VMEM((tg, npairs), F32)],
        compiler_params=_cparams("arbitrary"),
        name="peer_u",
    )(eid, h, gw, tab_packed)


PEER_V_ACCUMULATORS = 2


def _peer_v_kernel(eid_ref, w_ref, tab_ref, o_ref, *, tg, npairs):
    nacc = PEER_V_ACCUMULATORS

    def token(t):
        acc_e = [jnp.zeros((HALF_ROWS, LANES), F32) for _ in range(nacc)]
        acc_o = [jnp.zeros((HALF_ROWS, LANES), F32) for _ in range(nacc)]
        for j in range(npairs):
            e4 = pl.multiple_of(eid_ref[t, j], HALF_ROWS)
            even, odd = _unpack(tab_ref[pl.ds(e4, HALF_ROWS), :])
            wt = w_ref[t, j]
            acc_e[j % nacc] = acc_e[j % nacc] + wt * even
            acc_o[j % nacc] = acc_o[j % nacc] + wt * odd
        base = pl.multiple_of(t * ROW_TILES, ROW_TILES)
        o_ref[pl.ds(base, HALF_ROWS, stride=2), :] = functools.reduce(lambda a, b: a + b, acc_e)
        o_ref[pl.ds(base + 1, HALF_ROWS, stride=2), :] = functools.reduce(lambda a, b: a + b, acc_o)

    def body(i, carry):
        token(2 * i)
        token(2 * i + 1)
        return carry

    lax.fori_loop(0, tg // 2, body, 0)


def _peer_v(eid, w, tab_packed, tg):
    n, npairs = eid.shape
    assert tg % 2 == 0
    sspec = pl.BlockSpec((tg, npairs), lambda i: (i, 0), memory_space=pltpu.SMEM)
    return pl.pallas_call(
        functools.partial(_peer_v_kernel, tg=tg, npairs=npairs),
        grid=(n // tg,),
        in_specs=[sspec, sspec, pl.BlockSpec(memory_space=pltpu.VMEM)],
        out_specs=pl.BlockSpec((tg * ROW_TILES, LANES), lambda i: (i, 0)),
        out_shape=jax.ShapeDtypeStruct((n * ROW_TILES, LANES), F32),
        compiler_params=_cparams("arbitrary"),
        name="peer_v",
    )(eid, w, tab_packed)


def _final_kernel(x_ref, p_ref, g_ref, y_ref):
    tm = x_ref.shape[0]
    p = jnp.concatenate([p_ref[pl.ds(r, tm, stride=ROW_TILES), :] for r in range(ROW_TILES)], axis=1)
    x = x_ref[...] + p
    ms = jnp.mean(x * x, axis=-1, keepdims=True)
    y_ref[...] = x * lax.rsqrt(ms + EPS) * g_ref[...]


def _final(x, p, g, tm):
    n, d = x.shape
    xs = pl.BlockSpec((tm, d), lambda i: (i, 0))
    return pl.pallas_call(
        _final_kernel,
        grid=(n // tm,),
        in_specs=[xs, pl.BlockSpec((tm * ROW_TILES, LANES), lambda i: (i, 0)), pl.BlockSpec((1, d), lambda i: (0, 0))],
        out_specs=xs,
        out_shape=jax.ShapeDtypeStruct((n, d), F32),
        compiler_params=_cparams("parallel"),
        name="final_norm",
    )(x, p, g.reshape(1, d))


def _row_tile(n, pref):
    t = min(pref, n)
    assert n % t == 0, (n, t)
    return t


def _stream(x, w_in_bf, g_mix, dims):
    b, t, d = x.shape
    wsb, wrt, wmm, hd = dims
    scale = hd ** -0.5
    o = 3 * wsb
    outs = [(0, wsb, scale, BF16, 0),
            (wsb, 2 * wsb, 1.0, F32, wsb // hd), (2 * wsb, o, 1.0, F32, wsb // hd),
            (wsb, 2 * wsb, 1.0, BF16, 0), (2 * wsb, o, 1.0, BF16, 0),
            (o, o + 4 * wrt, 1.0, F32, 0),
            (o + 4 * wrt, o + 4 * wrt + wmm, scale, BF16, 0)]
    res = _norm_proj(x.reshape(b * t, d), g_mix, w_in_bf, outs, _row_tile(b * t, 512))
    return [r.reshape(b, t, r.shape[1]) if r.shape[0] == b * t else r for r in res]


def _peer_and_final(x1, g_ffn, wq_bf, sk_bf, u_packed, v_packed, g_final):
    n = x1.shape[0]
    h, eid, gw = _peer_score(x1, g_ffn, wq_bf, sk_bf, _row_tile(n, 512))
    tg = _row_tile(n, 128)
    w = _peer_u(eid, h, gw, u_packed, tg)
    p = _peer_v(eid, w, v_packed, tg)
    return _final(x1, p, g_final, _row_tile(n, 512))


def kernel(x_prompt, x_sample, mem_prompt, cache_sb_k, cache_sb_v, state_ret, cache_mem_k, cache_mem_v,
           g_mix, w_in, w_out, g_sb_out, g_ret_out, g_mem_out, g_mem, w_mem_kv,
           g_ffn, w_peer_q, peer_sub_keys, peer_u, peer_v, g_final):
    depth = w_in.shape[0]
    assert depth == 1, "single-layer step"
    bp, tp, d = x_prompt.shape
    bs, ts, _ = x_sample.shape
    _, _, past, sb_heads, hd = cache_sb_k.shape
    ret_heads = state_ret.shape[2]
    mem_heads = cache_mem_k.shape[3]
    nmem = mem_prompt.shape[1]
    wsb, wrt, wmm = sb_heads * hd, ret_heads * hd, mem_heads * hd
    assert hd == 64 and wrt == wmm and d == 2 * HALF_ROWS * LANES
    l = 0
    w_in_bf = w_in[l].astype(BF16)
    wo_bf = w_out[l].astype(BF16)
    dims = (wsb, wrt, wmm, hd)

    q_p, k_p, v_p, kb_p, vb_p, ret_p, qm_p = _stream(x_prompt, w_in_bf, g_mix[l], dims)
    sb_p = _sb_prompt(q_p, kb_p, vb_p, hd)
    zeros_state = jnp.zeros((bp, wrt, wrt), F32)
    ro_p, st_p = _retention(ret_p, jnp.arange(tp, dtype=I32), zeros_state, ret_heads, hd, _row_tile(tp, 512))
    mk_p, mv_p = _norm_proj(mem_prompt.reshape(bp * nmem, d), g_mem[l], w_mem_kv[l].astype(BF16),
                            [(0, wmm, 1.0, F32, 0), (wmm, 2 * wmm, 1.0, F32, 0)], _row_tile(bp * nmem, 512))
    mk_p = mk_p.reshape(bp, nmem, wmm)
    mv_p = mv_p.reshape(bp, nmem, wmm)
    x1_p = _merge(x_prompt, sb_p, ro_p, ret_p, qm_p, mk_p, mv_p, g_sb_out[l], g_ret_out[l], g_mem_out[l],
                  wo_bf, hd, _row_tile(tp, 256))

    q_s, k_s, v_s, kb_s, vb_s, ret_s, qm_s = _stream(x_sample, w_in_bf, g_mix[l], dims)
    sb_s = _sb_sample(q_s, kb_s, vb_s, cache_sb_k[l].reshape(bs, past, wsb), cache_sb_v[l].reshape(bs, past, wsb), hd)
    s0 = _block_diag_state(state_ret[l].astype(F32), ret_heads, hd)
    ro_s, st_s = _retention(ret_s, past + jnp.arange(ts, dtype=I32), s0, ret_heads, hd, _row_tile(ts, 512))
    x1_s = _merge(x_sample, sb_s, ro_s, ret_s, qm_s, cache_mem_k[l].reshape(bs, nmem, wmm),
                  cache_mem_v[l].reshape(bs, nmem, wmm), g_sb_out[l], g_ret_out[l], g_mem_out[l],
                  wo_bf, hd, _row_tile(ts, 256))

    wq_bf = w_peer_q[l].astype(BF16)
    sk_bf = peer_sub_keys[l].astype(BF16)
    u_packed = _pack_table(peer_u[l])
    v_packed = _pack_table(peer_v[l])
    y_p = _peer_and_final(x1_p.reshape(bp * tp, d), g_ffn[l], wq_bf, sk_bf, u_packed, v_packed, g_final)
    y_s = _peer_and_final(x1_s.reshape(bs * ts, d), g_ffn[l], wq_bf, sk_bf, u_packed, v_packed, g_final)

    return (y_p.reshape(bp, tp, d), y_s.reshape(bs, ts, d),
            k_p.reshape(1, bp, tp, sb_heads, hd), v_p.reshape(1, bp, tp, sb_heads, hd),
            _diag_blocks(st_p, ret_heads, hd)[None],
            mk_p.reshape(1, bp, nmem, mem_heads, hd), mv_p.reshape(1, bp, nmem, mem_heads, hd),
            k_s.reshape(1, bs, ts, sb_heads, hd), v_s.reshape(1, bs, ts, sb_heads, hd),
            _diag_blocks(st_s, ret_heads, hd)[None].astype(state_ret.dtype))
```

```python
import functools

import numpy as np
import jax
import jax.numpy as jnp
from jax import lax
from jax.experimental import pallas as pl
from jax.experimental.pallas import tpu as pltpu

F32 = jnp.float32
BF16 = jnp.bfloat16
I32 = jnp.int32

EPS = 1e-6
ROPE_BASE = 10000.0
PEER_TOPK = 16
RET_CHUNK = 64
LANES = 128
ROW_TILES = 8
HALF_ROWS = 4
VMEM_LIMIT = 56 * 1024 * 1024
SB_DEAD_LOG = -105.0


def _cparams(*sem):
    return pltpu.CompilerParams(dimension_semantics=sem, vmem_limit_bytes=VMEM_LIMIT)


def _dot(a, b):
    return jnp.dot(a, b, preferred_element_type=F32)


def _dot_nt(a, b):
    return lax.dot_general(a, b, (((1,), (1,)), ((), ())), preferred_element_type=F32)


def _split_bf16(x):
    hi = x.astype(BF16)
    lo = (x - hi.astype(F32)).astype(BF16)
    return hi, lo


def _dot_split(x, m_bf16):
    hi, lo = _split_bf16(x)
    return _dot(hi, m_bf16) + _dot(lo, m_bf16)


def _norm_proj_kernel(x_ref, g_ref, w_ref, *out_refs, cols):
    x = x_ref[...]
    ms = jnp.mean(x * x, axis=-1, keepdims=True)
    h = x * lax.rsqrt(ms + EPS) * g_ref[...]
    p = _dot(h.astype(BF16), w_ref[...])
    tm = x.shape[0]
    for ref, (lo, hi, scale, heads) in zip(out_refs, cols):
        v = p[:, lo:hi]
        if scale != 1.0:
            v = v * scale
        if heads:
            hd = (hi - lo) // heads
            for h in range(heads):
                ref[pl.ds(h, tm, stride=heads), :] = v[:, h * hd:(h + 1) * hd].astype(ref.dtype)
        else:
            ref[...] = v.astype(ref.dtype)


def _norm_proj(x, g, w_bf16, outs, tm):
    n, d = x.shape
    wcols = w_bf16.shape[1]
    shape = lambda rows, lo, hi, hds: (rows * hds, (hi - lo) // hds) if hds else (rows, hi - lo)
    out_shape = [jax.ShapeDtypeStruct(shape(n, lo, hi, hds), dt) for lo, hi, _, dt, hds in outs]
    out_specs = [pl.BlockSpec(shape(tm, lo, hi, hds), lambda i: (i, 0)) for lo, hi, _, _, hds in outs]
    return pl.pallas_call(
        functools.partial(_norm_proj_kernel, cols=tuple((lo, hi, sc, hds) for lo, hi, sc, _, hds in outs)),
        grid=(n // tm,),
        in_specs=[pl.BlockSpec((tm, d), lambda i: (i, 0)),
                  pl.BlockSpec((1, d), lambda i: (0, 0)),
                  pl.BlockSpec((d, wcols), lambda i: (0, 0))],
        out_specs=out_specs,
        out_shape=out_shape,
        compiler_params=_cparams("parallel"),
        name="norm_proj",
    )(x, g.reshape(1, d), w_bf16)


def _suffix_matrix(tk):
    j = np.arange(2 * tk)[:, None]
    s = np.arange(2 * tk)[None, :]
    return jnp.asarray(((j // tk == s // tk) & (j >= s)).astype(np.float32), dtype=BF16)


def _sb_step(q2, kb, vb, u, carry, tk, hd, valid):
    c0, c1, acc = carry
    tq = q2.shape[0]
    lane = lax.broadcasted_iota(I32, kb.shape, 1)
    first = lane < hd
    zero = jnp.zeros_like(kb)
    kk = jnp.concatenate([jnp.where(first, kb, zero), jnp.where(first, zero, kb)], axis=0)
    vv = jnp.concatenate([jnp.where(first, vb, zero), jnp.where(first, zero, vb)], axis=0)
    z = _dot_nt(q2, kk)
    sp = jnp.log(1.0 + jnp.exp(-jnp.abs(z)))
    lk = jnp.minimum(-z, 0.0) - sp
    if valid is not None:
        lk = jnp.where(valid, lk, 0.0)
    m = _dot_split(lk, u)
    c = jnp.concatenate([jnp.broadcast_to(c0, (tq, tk)), jnp.broadcast_to(c1, (tq, tk))], axis=1)
    a = jnp.exp(z + c + m)
    if valid is not None:
        a = jnp.where(valid, a, 0.0)
    acc = acc + _dot(a.astype(BF16), vv)
    c0 = c0 + jnp.sum(lk[:, :tk], axis=-1, keepdims=True)
    c1 = c1 + jnp.sum(lk[:, tk:], axis=-1, keepdims=True)
    return c0, c1, acc


def _sb_alive(carries):
    m = functools.reduce(jnp.maximum, [jnp.maximum(c0, c1) for c0, c1, _ in carries])
    return jnp.max(m) > SB_DEAD_LOG


def _sb_far_blocks(qs, block, nblk, u, carries, tk, hd, first=0):
    n = len(qs)
    flatten = lambda cs: tuple(x for c in cs for x in c)
    unflatten = lambda flat: [tuple(flat[3 * p:3 * p + 3]) for p in range(n)]

    def cond(state):
        return jnp.logical_and(state[0] < nblk, state[1])

    def body(state):
        i = state[0]
        kvs = block(i)
        cs = [_sb_step(qs[p], kvs[p][0], kvs[p][1], u, c, tk, hd, None) for p, c in enumerate(unflatten(state[2:]))]
        return (i + 1, _sb_alive(cs)) + flatten(cs)

    state = lax.while_loop(cond, body, (jnp.int32(first), _sb_alive(carries)) + flatten(carries))
    return unflatten(state[2:])


SB_PAIRS_PER_STEP = 2


def _sb_prompt_kernel(q_ref, k_ref, v_ref, u_ref, o_ref, *, tq, tk, hd, npairs):
    q0 = pl.program_id(2) * tq
    lanes = [slice(p * LANES, (p + 1) * LANES) for p in range(npairs)]
    qs = [q_ref[0, :, ln] for ln in lanes]
    u = u_ref[...]
    row = lax.broadcasted_iota(I32, (tq, 2 * tk), 0)
    koff = lax.broadcasted_iota(I32, (tq, 2 * tk), 1) & (tk - 1)
    carries = [(jnp.zeros((tq, 1), F32), jnp.zeros((tq, 1), F32), jnp.zeros((tq, LANES), F32)) for _ in lanes]
    nd = tq // tk
    for d in range(nd + 1):
        off = (nd - 1 - d) * tk
        limit = row if d < nd else row + jnp.where(q0 >= tk, 0, -4 * tq)
        valid = (koff + off) < limit
        ks = pl.multiple_of(jnp.maximum(q0 + off, 0), tk)
        carries = [_sb_step(qs[p], k_ref[0, pl.ds(ks, tk), ln], v_ref[0, pl.ds(ks, tk), ln], u, carries[p], tk, hd, valid)
                   for p, ln in enumerate(lanes)]

    def block(i):
        ks = pl.multiple_of(q0 - (i + 1) * tk, tk)
        return [(k_ref[0, pl.ds(ks, tk), ln], v_ref[0, pl.ds(ks, tk), ln]) for ln in lanes]

    carries = _sb_far_blocks(qs, block, q0 // tk, u, carries, tk, hd, first=1)
    for p, ln in enumerate(lanes):
        o_ref[0, :, ln] = carries[p][2]


def _sb_prompt(q, k, v, hd, tq=256, tk=128):
    b, t, w = q.shape
    npairs = SB_PAIRS_PER_STEP
    wb = npairs * LANES
    assert w % wb == 0
    return pl.pallas_call(
        functools.partial(_sb_prompt_kernel, tq=tq, tk=tk, hd=hd, npairs=npairs),
        grid=(b, w // wb, t // tq),
        in_specs=[pl.BlockSpec((1, tq, wb), lambda bi, hp, qi: (bi, qi, hp)),
                  pl.BlockSpec((1, t, wb), lambda bi, hp, qi: (bi, 0, hp)),
                  pl.BlockSpec((1, t, wb), lambda bi, hp, qi: (bi, 0, hp)),
                  pl.BlockSpec((2 * tk, 2 * tk), lambda bi, hp, qi: (0, 0))],
        out_specs=pl.BlockSpec((1, tq, wb), lambda bi, hp, qi: (bi, qi, hp)),
        out_shape=jax.ShapeDtypeStruct((b, t, w), F32),
        compiler_params=_cparams("parallel", "parallel", "arbitrary"),
        name="sb_prompt",
    )(q, k, v, _suffix_matrix(tk))


def _sb_sample_kernel(q_ref, kn_ref, vn_ref, kc_ref, vc_ref, un_ref, u_ref, o_ref, *, ts, tk, hd, past):
    q2 = q_ref[0]
    row = lax.broadcasted_iota(I32, (ts, 2 * ts), 0)
    koff = lax.broadcasted_iota(I32, (ts, 2 * ts), 1) & (ts - 1)
    carry = (jnp.zeros((ts, 1), F32), jnp.zeros((ts, 1), F32), jnp.zeros((ts, LANES), F32))
    carry = _sb_step(q2, kn_ref[0], vn_ref[0], un_ref[...], carry, ts, hd, koff < row)
    u = u_ref[...]

    def block(i):
        ks = pl.multiple_of(past - (i + 1) * tk, tk)
        return kc_ref[0, pl.ds(ks, tk), :].astype(BF16), vc_ref[0, pl.ds(ks, tk), :].astype(BF16)

    (carry,) = _sb_far_blocks([q2], lambda i: [block(i)], past // tk, u, [carry], tk, hd)
    o_ref[0] = carry[2]


def _sb_sample(q, kn, vn, kc, vc, hd, tk=128):
    b, ts, w = q.shape
    past = kc.shape[1]
    new_spec = pl.BlockSpec((1, ts, LANES), lambda bi, hp: (bi, 0, hp))
    cache_spec = pl.BlockSpec((1, past, LANES), lambda bi, hp: (bi, 0, hp))
    return pl.pallas_call(
        functools.partial(_sb_sample_kernel, ts=ts, tk=tk, hd=hd, past=past),
        grid=(b, w // LANES),
        in_specs=[new_spec, new_spec, new_spec, cache_spec, cache_spec,
                  pl.BlockSpec((2 * ts, 2 * ts), lambda bi, hp: (0, 0)),
                  pl.BlockSpec((2 * tk, 2 * tk), lambda bi, hp: (0, 0))],
        out_specs=new_spec,
        out_shape=jax.ShapeDtypeStruct((b, ts, w), F32),
        compiler_params=_cparams("parallel", "parallel"),
        name="sb_sample",
    )(q, kn, vn, kc, vc, _suffix_matrix(ts), _suffix_matrix(tk))


def _ret_kernel(q_ref, k_ref, v_ref, cos_ref, sin_ref, dec_ref, qd_ref, kd_ref, gl_ref, bd_ref, s0_ref,
                o_ref, st_ref, s_scr, *, tc, nh, hd, k_scale):
    t = pl.program_id(1)
    w = nh * hd
    L = RET_CHUNK

    @pl.when(t == 0)
    def _():
        s_scr[...] = s0_ref[0]

    lane = lax.broadcasted_iota(I32, (L, w), 1)
    first_half = (lane & (hd - 1)) < hd // 2
    head = lane // hd

    def rope(x, cos, sin):
        swapped = jnp.where(first_half, pltpu.roll(x, w - hd // 2, 1), pltpu.roll(x, hd // 2, 1))
        return x * cos + swapped * sin

    s = s_scr[...]
    for ci in range(tc // L):
        sl = slice(ci * L, (ci + 1) * L)
        cos = cos_ref[sl, :]
        sin = sin_ref[sl, :]
        q = rope(q_ref[0, sl, :], cos, sin)
        k = rope(k_ref[0, sl, :], cos, sin) * k_scale
        qb = q.astype(BF16)
        kb = k.astype(BF16)
        vb = v_ref[0, sl, :].astype(BF16)
        zero = jnp.zeros_like(qb)
        out = _dot(qb, s.astype(BF16)) * qd_ref[...]
        for h in range(nh):
            mh = head == h
            sc = _dot_nt(jnp.where(mh, qb, zero), kb)
            p = (sc * dec_ref[h]).astype(BF16)
            out = out + _dot(p, jnp.where(mh, vb, zero))
        o_ref[0, sl, :] = out
        kd = (k * kd_ref[...]).astype(BF16)
        outer = lax.dot_general(kd, vb, (((0,), (0,)), ((), ())), preferred_element_type=F32)
        s = gl_ref[...] * s + bd_ref[...] * outer
    s_scr[...] = s

    @pl.when(t == pl.num_programs(1) - 1)
    def _():
        st_ref[0] = s


def _retention(ret, pos, s0, nh, hd, tc):
    b, t, _ = ret.shape
    w = nh * hd
    L = RET_CHUNK
    half = hd // 2
    freqs = ROPE_BASE ** (-jnp.arange(half, dtype=F32) / half)
    ang = pos.astype(F32)[:, None] * freqs[None, :]
    cos = jnp.tile(jnp.cos(ang), (1, 2 * nh))
    sin = jnp.tile(jnp.concatenate([-jnp.sin(ang), jnp.sin(ang)], axis=1), (1, nh))
    lg = jnp.log1p(-jnp.exp2(-5.0 - jnp.arange(nh, dtype=F32)))
    idx = jnp.arange(L, dtype=F32)
    diff = idx[:, None] - idx[None, :]
    causal = diff >= 0
    dec = jnp.where(causal[None], jnp.exp(jnp.where(causal, diff, 0.0)[None] * lg[:, None, None]), 0.0)
    lg_lane = jnp.repeat(lg, hd)[None, :]
    qd = jnp.exp((idx + 1.0)[:, None] * lg_lane)
    kd = jnp.exp((L - 1.0 - idx)[:, None] * lg_lane)
    gl = jnp.exp(L * lg_lane)
    hid = jnp.arange(w) // hd
    bd = (hid[:, None] == hid[None, :]).astype(F32)
    tok = lambda j: pl.BlockSpec((1, tc, w), lambda bi, ti: (bi, ti, j))
    tab = pl.BlockSpec((tc, w), lambda bi, ti: (ti, 0))
    const = lambda shape: pl.BlockSpec(shape, lambda bi, ti: (0,) * len(shape))
    state = pl.BlockSpec((1, w, w), lambda bi, ti: (bi, 0, 0))
    return pl.pallas_call(
        functools.partial(_ret_kernel, tc=tc, nh=nh, hd=hd, k_scale=hd ** -0.5),
        grid=(b, t // tc),
        in_specs=[tok(0), tok(1), tok(2), tab, tab, const((nh, L, L)), const((L, w)), const((L, w)),
                  const((1, w)), const((w, w)), state],
        out_specs=[pl.BlockSpec((1, tc, w), lambda bi, ti: (bi, ti, 0)), state],
        out_shape=[jax.ShapeDtypeStruct((b, t, w), F32), jax.ShapeDtypeStruct((b, w, w), F32)],
        scratch_shapes=[pltpu.VMEM((w, w), F32)],
        compiler_params=_cparams("parallel", "arbitrary"),
        name="retention",
    )(ret, ret, ret, cos, sin, dec, qd, kd, gl, bd, s0)


def _block_diag_state(st, nh, hd):
    b = st.shape[0]
    eye = jnp.eye(nh, dtype=st.dtype)
    return jnp.einsum('bhde,hg->bhdge', st, eye).reshape(b, nh * hd, nh * hd)


def _diag_blocks(s, nh, hd):
    b = s.shape[0]
    s5 = s.reshape(b, nh, hd, nh, hd)
    return jnp.stack([s5[:, h, :, h, :] for h in range(nh)], axis=1)


def _head_mean_matrix(w, hd):
    hid = np.arange(w) // hd
    return jnp.asarray((hid[:, None] == hid[None, :]).astype(np.float32) / hd, dtype=BF16)


def _headnorm(x, g_mat, gain):
    ms = _dot_split(x * x, g_mat)
    return x * lax.rsqrt(ms + EPS) * gain


def _merge_kernel(x_ref, sb_ref, ret_ref, gate_ref, qm_ref, mk_ref, mv_ref, gsb_ref, gret_ref, gmo_ref,
                  msb_ref, mrt_ref, wo_ref, x1_ref, *, nh_mem, hd):
    wsb = sb_ref.shape[2]
    wrt = ret_ref.shape[2]
    qm = qm_ref[0]
    mkb = mk_ref[0].astype(BF16)
    mvb = mv_ref[0].astype(BF16)
    head_q = lax.broadcasted_iota(I32, qm.shape, 1) // hd
    head_m = lax.broadcasted_iota(I32, mkb.shape, 1) // hd
    mem_o = jnp.zeros(qm.shape, F32)
    for h in range(nh_mem):
        s = _dot_nt(jnp.where(head_q == h, qm, jnp.zeros_like(qm)), mkb)
        e = jnp.exp(s - jnp.max(s, axis=-1, keepdims=True))
        p = e / jnp.sum(e, axis=-1, keepdims=True)
        mem_o = mem_o + _dot(p.astype(BF16), jnp.where(head_m == h, mvb, jnp.zeros_like(mvb)))
    sb_n = _headnorm(sb_ref[0], msb_ref[...], gsb_ref[...])
    gate = gate_ref[0]
    ret_n = _headnorm(ret_ref[0], mrt_ref[...], gret_ref[...]) * (gate / (1.0 + jnp.exp(-gate)))
    mem_n = _headnorm(mem_o, mrt_ref[...], gmo_ref[...])
    y = (_dot(sb_n.astype(BF16), wo_ref[0:wsb, :])
         + _dot(ret_n.astype(BF16), wo_ref[wsb:wsb + wrt, :])
         + _dot(mem_n.astype(BF16), wo_ref[wsb + wrt:, :]))
    x1_ref[0] = x_ref[0] + y


def _merge(x, sb_o, ret_o, ret, qm, mk, mv, g_sb, g_ret, g_mo, wo_bf16, hd, tm):
    b, t, d = x.shape
    wsb = sb_o.shape[2]
    wrt = ret_o.shape[2]
    nmem = mk.shape[1]
    tok = lambda wd, j=0: pl.BlockSpec((1, tm, wd), lambda bi, ti: (bi, ti, j))
    const = lambda shape: pl.BlockSpec(shape, lambda bi, ti: (0,) * len(shape))
    memspec = pl.BlockSpec((1, nmem, wrt), lambda bi, ti: (bi, 0, 0))
    return pl.pallas_call(
        functools.partial(_merge_kernel, nh_mem=wrt // hd, hd=hd),
        grid=(b, t // tm),
        in_specs=[tok(d), tok(wsb), tok(wrt), tok(wrt, 3), tok(wrt), memspec, memspec,
                  const((1, wsb)), const((1, wrt)), const((1, wrt)),
                  const((wsb, wsb)), const((wrt, wrt)), const((d, d))],
        out_specs=tok(d),
        out_shape=jax.ShapeDtypeStruct((b, t, d), F32),
        compiler_params=_cparams("parallel", "parallel"),
        name="merge",
    )(x, sb_o, ret_o, ret, qm, mk, mv, g_sb.reshape(1, wsb), g_ret.reshape(1, wrt), g_mo.reshape(1, wrt),
      _head_mean_matrix(wsb, hd), _head_mean_matrix(wrt, hd), wo_bf16)


def _topk_rows(s, k, rid=None):
    if rid is None:
        rid = lax.broadcasted_iota(I32, s.shape, 0)
    big = jnp.int32(2 ** 30)
    vals, idxs = [], []
    for _ in range(k):
        m = jnp.max(s, axis=0, keepdims=True)
        i = jnp.min(jnp.where(s == m, rid, big), axis=0, keepdims=True)
        vals.append(m)
        idxs.append(i)
        s = jnp.where(rid == i, -jnp.inf, s)
    return jnp.concatenate(vals, axis=0), jnp.concatenate(idxs, axis=0)


def _staircase_blocks(k, rows=8):
    blocks = []
    singles = [a for a in range(k) if k // (a + 1) == 1]
    for a in range(k):
        nb = k // (a + 1)
        if nb > 1:
            blocks.extend(('row', a, b0) for b0 in range(0, nb, rows))
    assert singles and singles[0] % rows == 0 and len(singles) % rows == 0 and singles[-1] == k - 1
    blocks.extend(('col', a0, 0) for a0 in range(singles[0], k, rows))
    flat = []
    for kind, a, b in blocks:
        flat.extend([a * k + b + i for i in range(rows)] if kind == 'row' else [(a + i) * k + b for i in range(rows)])
    need = {(a, b) for a in range(k) for b in range(k) if (a + 1) * (b + 1) <= k}
    assert need <= {(f // k, f % k) for f in flat} and len(set(flat)) == len(flat)
    return blocks, np.array(flat, np.int32)[:, None]


def _take_rows(tbl, idx):
    out = jnp.zeros(idx.shape, tbl.dtype)
    for a in range(tbl.shape[0]):
        out = out + jnp.where(idx == a, tbl[a:a + 1, :], 0)
    return out


def _peer_score_kernel(x_ref, g_ref, wq_ref, sk_ref, cf_ref, h_ref, eid_ref, gw_ref, hb_scr,
                       eid_scr, gw_scr, sc_scr, *, topk, nkeys, blocks):
    head = pl.program_id(1)

    @pl.when(head == 0)
    def _():
        x = x_ref[...]
        ms = jnp.mean(x * x, axis=-1, keepdims=True)
        hn = x * lax.rsqrt(ms + EPS) * g_ref[...]
        for r in range(ROW_TILES):
            h_ref[pl.ds(r, hn.shape[0], stride=ROW_TILES), :] = hn[:, r * LANES:(r + 1) * LANES]
        hb_scr[...] = hn.astype(BF16)

    qb = _dot(hb_scr[...], wq_ref[...]).astype(BF16)
    hq = qb.shape[1] // 2
    sc_scr[0] = _dot_nt(sk_ref[0], qb[:, :hq])
    sc_scr[1] = _dot_nt(sk_ref[1], qb[:, hq:])
    brows = cf_ref.shape[0] // len(blocks)
    out_rows = pl.ds(pl.multiple_of(head * topk, topk), topk)
    width = min(LANES, qb.shape[0])

    def column_group(c):
        cols = slice(c * width, (c + 1) * width)
        sv0, si0 = _topk_rows(sc_scr[0, :, cols], topk)
        sv1, si1 = _topk_rows(sc_scr[1, :, cols], topk)
        cand = jnp.concatenate(
            [sv0[a:a + 1, :] + sv1[b:b + brows, :] if kind == 'row' else sv0[a:a + brows, :] + sv1[b:b + 1, :]
             for kind, a, b in blocks], axis=0)
        cv, ci = _topk_rows(cand, topk, cf_ref[:, cols])
        e1 = _take_rows(si0, ci // topk)
        e2 = _take_rows(si1, ci % topk)
        eid_scr[out_rows, cols] = ((e1 * nkeys + e2) * HALF_ROWS).astype(F32)
        e = jnp.exp(cv - cv[0:1, :])
        gw_scr[out_rows, cols] = e / jnp.sum(e, axis=0, keepdims=True)

    for c in range(qb.shape[0] // width):
        column_group(c)

    @pl.when(head == pl.num_programs(1) - 1)
    def _():
        eid_ref[...] = eid_scr[...].T.astype(I32)
        gw_ref[...] = gw_scr[...].T


def _peer_score(x, g, wq_bf16, sk_bf16, tm):
    n, d = x.shape
    nkeys, khalf = sk_bf16.shape[1], sk_bf16.shape[2]
    nheads = wq_bf16.shape[1] // (2 * khalf)
    k = PEER_TOPK
    blocks, flat = _staircase_blocks(k)
    cf = jnp.asarray(np.broadcast_to(flat, (flat.shape[0], tm)))
    xs = pl.BlockSpec((tm, d), lambda i, h: (i, 0))
    cs = pl.BlockSpec(cf.shape, lambda i, h: (0, 0))
    return pl.pallas_call(
        functools.partial(_peer_score_kernel, topk=k, nkeys=nkeys, blocks=tuple(blocks)),
        grid=(n // tm, nheads),
        in_specs=[xs, pl.BlockSpec((1, d), lambda i, h: (0, 0)),
                  pl.BlockSpec((d, 2 * khalf), lambda i, h: (0, h)),
                  pl.BlockSpec((2, nkeys, khalf), lambda i, h: (0, 0, 0)), cs],
        out_specs=[pl.BlockSpec((tm * ROW_TILES, LANES), lambda i, h: (i, 0)),
                   pl.BlockSpec((tm, nheads * k), lambda i, h: (i, 0)),
                   pl.BlockSpec((tm, nheads * k), lambda i, h: (i, 0))],
        out_shape=[jax.ShapeDtypeStruct((n * ROW_TILES, LANES), F32),
                   jax.ShapeDtypeStruct((n, nheads * k), I32), jax.ShapeDtypeStruct((n, nheads * k), F32)],
        scratch_shapes=[pltpu.VMEM((tm, d), BF16), pltpu.VMEM((nheads * k, tm), F32),
                        pltpu.VMEM((nheads * k, tm), F32), pltpu.VMEM((2, nkeys, tm), F32)],
        compiler_params=_cparams("parallel", "arbitrary"),
        name="peer_score",
    )(x, g.reshape(1, d), wq_bf16, sk_bf16, cf)


def _pack_table(tab):
    e, d = tab.shape
    assert d == ROW_TILES * LANES
    te = _row_tile(e, 512)
    return pl.pallas_call(
        _pack_kernel,
        grid=(e // te,),
        in_specs=[pl.BlockSpec((te, d), lambda i: (i, 0))],
        out_specs=pl.BlockSpec((te * HALF_ROWS, LANES), lambda i: (i, 0)),
        out_shape=jax.ShapeDtypeStruct((e * HALF_ROWS, LANES), I32),
        compiler_params=_cparams("parallel"),
        name="pack_table",
    )(tab)


def _pack_words(even, odd):
    odd = lax.bitcast_convert_type(odd, I32)
    low = lax.shift_right_logical(lax.bitcast_convert_type(even.astype(BF16).astype(F32), I32), 16)
    sign = odd & jnp.int32(-2 ** 31)
    mag = (odd & jnp.int32(2 ** 31 - 1)) + jnp.int32(0x8000)
    high = jnp.where(mag >= low, (mag - low) >> 16, 0)
    return sign | (high << 16) | low


def _pack_kernel(t_ref, o_ref):
    te = t_ref.shape[0]
    x = t_ref[...]
    for s in range(HALF_ROWS):
        o_ref[pl.ds(s, te, stride=HALF_ROWS), :] = _pack_words(x[:, (2 * s) * LANES:(2 * s + 1) * LANES],
                                                               x[:, (2 * s + 1) * LANES:(2 * s + 2) * LANES])


def _unpack(wd):
    even = lax.bitcast_convert_type(wd << 16, F32)
    odd = lax.bitcast_convert_type(wd, F32)
    return even, odd


def _erf(x):
    return lax.erf(x)


def _token_tiles(ref, t):
    base = pl.multiple_of(t * ROW_TILES, ROW_TILES)
    return ref[pl.ds(base, HALF_ROWS, stride=2), :], ref[pl.ds(base + 1, HALF_ROWS, stride=2), :]


PEER_U_SET = 4


def _peer_u_kernel(eid_ref, h_ref, gw_ref, tab_ref, w_ref, pa, pb, q_scr, act_scr, *, tg, npairs):
    ones = jnp.ones((LANES, LANES), BF16)
    row = lax.broadcasted_iota(I32, (npairs, LANES), 0)
    eye = (row == lax.broadcasted_iota(I32, (npairs, LANES), 1)).astype(F32)
    ns = PEER_U_SET

    def products(t, p_scr, qset, k):
        he, ho = _token_tiles(h_ref, t)
        he2 = jnp.concatenate([he, he], axis=0)
        ho2 = jnp.concatenate([ho, ho], axis=0)
        for j in range(0, npairs, 2):
            wa = tab_ref[pl.ds(pl.multiple_of(eid_ref[t, j], HALF_ROWS), HALF_ROWS), :]
            wb = tab_ref[pl.ds(pl.multiple_of(eid_ref[t, j + 1], HALF_ROWS), HALF_ROWS), :]
            even, odd = _unpack(jnp.concatenate([wa, wb], axis=0))
            p_scr[j * HALF_ROWS:(j + 2) * HALF_ROWS, :] = even * he2 + odd * ho2
        q = p_scr[pl.ds(0, npairs, stride=HALF_ROWS), :]
        for s in range(1, HALF_ROWS):
            q = q + p_scr[pl.ds(s, npairs, stride=HALF_ROWS), :]
        hi, lo = _split_bf16(q)
        q_scr[qset, 2 * k * npairs:(2 * k + 1) * npairs, :] = hi
        q_scr[qset, (2 * k + 1) * npairs:(2 * k + 2) * npairs, :] = lo

    def lane_sums(qset, tokens):
        r = _dot(q_scr[qset], ones)
        for k, t in enumerate(tokens):
            rk = r[2 * k * npairs:(2 * k + 1) * npairs, :] + r[(2 * k + 1) * npairs:(2 * k + 2) * npairs, :]
            act_scr[pl.ds(t, 1), :] = jnp.sum(rk * eye, axis=0, keepdims=True)

    q_scr[1] = jnp.zeros(q_scr.shape[1:], BF16)

    def body(i, carry):
        t0 = 2 * ns * i
        lane_sums(1, [jnp.maximum(t0 - ns + k, 0) for k in range(ns)])
        for k in range(ns):
            products(t0 + k, (pa, pb)[k % 2], 0, k)
        lane_sums(0, [t0 + k for k in range(ns)])
        for k in range(ns):
            products(t0 + ns + k, (pa, pb)[k % 2], 1, k)
        return carry

    lax.fori_loop(0, tg // (2 * ns), body, 0)
    lane_sums(1, [tg - ns + k for k in range(ns)])
    act = act_scr[...]
    gelu = 0.5 * act * (1.0 + _erf(act * (2.0 ** -0.5)))
    w = gw_ref[...] * gelu
    w_ref[...] = _pack_words(w, pltpu.roll(w, npairs - 1, 1))


def _peer_u(eid, h, gw, tab_packed, tg):
    n, npairs = eid.shape
    assert npairs == LANES and h.shape == (n * ROW_TILES, LANES) and tg % (2 * PEER_U_SET) == 0
    tspec = pl.BlockSpec((tg, npairs), lambda i: (i, 0))
    pshape = pltpu.VMEM((npairs * HALF_ROWS, LANES), F32)
    qshape = pltpu.VMEM((2, 2 * PEER_U_SET * npairs, LANES), BF16)
    return pl.pallas_call(
        functools.partial(_peer_u_kernel, tg=tg, npairs=npairs),
        grid=(n // tg,),
        in_specs=[pl.BlockSpec((tg, npairs), lambda i: (i, 0), memory_space=pltpu.SMEM),
                  pl.BlockSpec((tg * ROW_TILES, LANES), lambda i: (i, 0)), tspec,
                  pl.BlockSpec(memory_space=pltpu.VMEM)],
        out_specs=tspec,
        out_shape=jax.ShapeDtypeStruct((n, npairs), I32),
        scratch_shapes=[pshape, pshape, qshape, pltpu.VMEM((tg, npairs), F32)],
        compiler_params=_cparams("arbitrary"),
        name="peer_u",
    )(eid, h, gw, tab_packed)


PEER_V_ACCUMULATORS = 2


def _peer_v_kernel(eid_ref, w_ref, tab_ref, o_ref, *, tg, npairs):
    nacc = PEER_V_ACCUMULATORS

    def token(t):
        acc_e = [jnp.zeros((HALF_ROWS, LANES), F32) for _ in range(nacc)]
        acc_o = [jnp.zeros((HALF_ROWS, LANES), F32) for _ in range(nacc)]
        for j in range(npairs):
            if j % 2 == 0:
                wpair = _unpack(jnp.full((HALF_ROWS, LANES), w_ref[t, j], I32))
            wt = wpair[j % 2]
            e4 = pl.multiple_of(eid_ref[t, j], HALF_ROWS)
            even, odd = _unpack(tab_ref[pl.ds(e4, HALF_ROWS), :])
            acc_e[j % nacc] = acc_e[j % nacc] + wt * even
            acc_o[j % nacc] = acc_o[j % nacc] + wt * odd
        base = pl.multiple_of(t * ROW_TILES, ROW_TILES)
        o_ref[pl.ds(base, HALF_ROWS, stride=2), :] = functools.reduce(lambda a, b: a + b, acc_e)
        o_ref[pl.ds(base + 1, HALF_ROWS, stride=2), :] = functools.reduce(lambda a, b: a + b, acc_o)

    def body(i, carry):
        token(2 * i)
        token(2 * i + 1)
        return carry

    lax.fori_loop(0, tg // 2, body, 0)


def _peer_v(eid, w, tab_packed, tg):
    n, npairs = eid.shape
    assert tg % 2 == 0
    sspec = pl.BlockSpec((tg, npairs), lambda i: (i, 0), memory_space=pltpu.SMEM)
    return pl.pallas_call(
        functools.partial(_peer_v_kernel, tg=tg, npairs=npairs),
        grid=(n // tg,),
        in_specs=[sspec, sspec, pl.BlockSpec(memory_space=pltpu.VMEM)],
        out_specs=pl.BlockSpec((tg * ROW_TILES, LANES), lambda i: (i, 0)),
        out_shape=jax.ShapeDtypeStruct((n * ROW_TILES, LANES), F32),
        compiler_params=_cparams("arbitrary"),
        name="peer_v",
    )(eid, w, tab_packed)


def _final_kernel(x_ref, p_ref, g_ref, y_ref):
    tm = x_ref.shape[0]
    p = jnp.concatenate([p_ref[pl.ds(r, tm, stride=ROW_TILES), :] for r in range(ROW_TILES)], axis=1)
    x = x_ref[...] + p
    ms = jnp.mean(x * x, axis=-1, keepdims=True)
    y_ref[...] = x * lax.rsqrt(ms + EPS) * g_ref[...]


def _final(x, p, g, tm):
    n, d = x.shape
    xs = pl.BlockSpec((tm, d), lambda i: (i, 0))
    return pl.pallas_call(
        _final_kernel,
        grid=(n // tm,),
        in_specs=[xs, pl.BlockSpec((tm * ROW_TILES, LANES), lambda i: (i, 0)), pl.BlockSpec((1, d), lambda i: (0, 0))],
        out_specs=xs,
        out_shape=jax.ShapeDtypeStruct((n, d), F32),
        compiler_params=_cparams("parallel"),
        name="final_norm",
    )(x, p, g.reshape(1, d))


def _row_tile(n, pref):
    t = min(pref, n)
    assert n % t == 0, (n, t)
    return t


def _stream(x, w_in_bf, g_mix, dims):
    b, t, d = x.shape
    wsb, wrt, wmm, hd = dims
    scale = hd ** -0.5
    o = 3 * wsb
    outs = [(0, wsb, scale, BF16, 0),
            (wsb, 2 * wsb, 1.0, F32, wsb // hd), (2 * wsb, o, 1.0, F32, wsb // hd),
            (wsb, 2 * wsb, 1.0, BF16, 0), (2 * wsb, o, 1.0, BF16, 0),
            (o, o + 4 * wrt, 1.0, F32, 0),
            (o + 4 * wrt, o + 4 * wrt + wmm, scale, BF16, 0)]
    res = _norm_proj(x.reshape(b * t, d), g_mix, w_in_bf, outs, _row_tile(b * t, 512))
    return [r.reshape(b, t, r.shape[1]) if r.shape[0] == b * t else r for r in res]


def _peer_and_final(x1, g_ffn, wq_bf, sk_bf, u_packed, v_packed, g_final):
    n = x1.shape[0]
    h, eid, gw = _peer_score(x1, g_ffn, wq_bf, sk_bf, _row_tile(n, 512))
    tg = _row_tile(n, 128)
    w = _peer_u(eid, h, gw, u_packed, tg)
    p = _peer_v(eid, w, v_packed, tg)
    return _final(x1, p, g_final, _row_tile(n, 512))


def kernel(x_prompt, x_sample, mem_prompt, cache_sb_k, cache_sb_v, state_ret, cache_mem_k, cache_mem_v,
           g_mix, w_in, w_out, g_sb_out, g_ret_out, g_mem_out, g_mem, w_mem_kv,
           g_ffn, w_peer_q, peer_sub_keys, peer_u, peer_v, g_final):
    depth = w_in.shape[0]
    assert depth == 1, "single-layer step"
    bp, tp, d = x_prompt.shape
    bs, ts, _ = x_sample.shape
    _, _, past, sb_heads, hd = cache_sb_k.shape
    ret_heads = state_ret.shape[2]
    mem_heads = cache_mem_k.shape[3]
    nmem = mem_prompt.shape[1]
    wsb, wrt, wmm = sb_heads * hd, ret_heads * hd, mem_heads * hd
    assert hd == 64 and wrt == wmm and d == 2 * HALF_ROWS * LANES
    l = 0
    w_in_bf = w_in[l].astype(BF16)
    wo_bf = w_out[l].astype(BF16)
    dims = (wsb, wrt, wmm, hd)

    q_p, k_p, v_p, kb_p, vb_p, ret_p, qm_p = _stream(x_prompt, w_in_bf, g_mix[l], dims)
    sb_p = _sb_prompt(q_p, kb_p, vb_p, hd)
    zeros_state = jnp.zeros((bp, wrt, wrt), F32)
    ro_p, st_p = _retention(ret_p, jnp.arange(tp, dtype=I32), zeros_state, ret_heads, hd, _row_tile(tp, 512))
    mk_p, mv_p = _norm_proj(mem_prompt.reshape(bp * nmem, d), g_mem[l], w_mem_kv[l].astype(BF16),
                            [(0, wmm, 1.0, F32, 0), (wmm, 2 * wmm, 1.0, F32, 0)], _row_tile(bp * nmem, 512))
    mk_p = mk_p.reshape(bp, nmem, wmm)
    mv_p = mv_p.reshape(bp, nmem, wmm)
    x1_p = _merge(x_prompt, sb_p, ro_p, ret_p, qm_p, mk_p, mv_p, g_sb_out[l], g_ret_out[l], g_mem_out[l],
                  wo_bf, hd, _row_tile(tp, 256))

    q_s, k_s, v_s, kb_s, vb_s, ret_s, qm_s = _stream(x_sample, w_in_bf, g_mix[l], dims)
    sb_s = _sb_sample(q_s, kb_s, vb_s, cache_sb_k[l].reshape(bs, past, wsb), cache_sb_v[l].reshape(bs, past, wsb), hd)
    s0 = _block_diag_state(state_ret[l].astype(F32), ret_heads, hd)
    ro_s, st_s = _retention(ret_s, past + jnp.arange(ts, dtype=I32), s0, ret_heads, hd, _row_tile(ts, 512))
    x1_s = _merge(x_sample, sb_s, ro_s, ret_s, qm_s, cache_mem_k[l].reshape(bs, nmem, wmm),
                  cache_mem_v[l].reshape(bs, nmem, wmm), g_sb_out[l], g_ret_out[l], g_mem_out[l],
                  wo_bf, hd, _row_tile(ts, 256))

    wq_bf = w_peer_q[l].astype(BF16)
    sk_bf = peer_sub_keys[l].astype(BF16)
    u_packed = _pack_table(peer_u[l])
    v_packed = _pack_table(peer_v[l])
    y_p = _peer_and_final(x1_p.reshape(bp * tp, d), g_ffn[l], wq_bf, sk_bf, u_packed, v_packed, g_final)
    y_s = _peer_and_final(x1_s.reshape(bs * ts, d), g_ffn[l], wq_bf, sk_bf, u_packed, v_packed, g_final)

    return (y_p.reshape(bp, tp, d), y_s.reshape(bs, ts, d),
            k_p.reshape(1, bp, tp, sb_heads, hd), v_p.reshape(1, bp, tp, sb_heads, hd),
            _diag_blocks(st_p, ret_heads, hd)[None],
            mk_p.reshape(1, bp, nmem, mem_heads, hd), mv_p.reshape(1, bp, nmem, mem_heads, hd),
            k_s.reshape(1, bs, ts, sb_heads, hd), v_s.reshape(1, bs, ts, sb_heads, hd),
            _diag_blocks(st_s, ret_heads, hd)[None].astype(state_ret.dtype))
```

```python
import functools

import numpy as np
import jax
import jax.numpy as jnp
from jax import lax
from jax.experimental import pallas as pl
from jax.experimental.pallas import tpu as pltpu

F32 = jnp.float32
BF16 = jnp.bfloat16
I32 = jnp.int32

EPS = 1e-6
ROPE_BASE = 10000.0
PEER_TOPK = 16
RET_CHUNK = 64
LANES = 128
ROW_TILES = 8
HALF_ROWS = 4
VMEM_LIMIT = 56 * 1024 * 1024
SB_DEAD_LOG = -105.0


def _cparams(*sem):
    return pltpu.CompilerParams(dimension_semantics=sem, vmem_limit_bytes=VMEM_LIMIT)


def _dot(a, b):
    return jnp.dot(a, b, preferred_element_type=F32)


def _dot_nt(a, b):
    return lax.dot_general(a, b, (((1,), (1,)), ((), ())), preferred_element_type=F32)


def _split_bf16(x):
    hi = x.astype(BF16)
    lo = (x - hi.astype(F32)).astype(BF16)
    return hi, lo


def _dot_split(x, m_bf16):
    hi, lo = _split_bf16(x)
    return _dot(hi, m_bf16) + _dot(lo, m_bf16)


def _norm_proj_kernel(x_ref, g_ref, w_ref, *out_refs, cols):
    x = x_ref[...]
    ms = jnp.mean(x * x, axis=-1, keepdims=True)
    h = x * lax.rsqrt(ms + EPS) * g_ref[...]
    p = _dot(h.astype(BF16), w_ref[...])
    tm = x.shape[0]
    for ref, (lo, hi, scale, heads) in zip(out_refs, cols):
        v = p[:, lo:hi]
        if scale != 1.0:
            v = v * scale
        if heads:
            hd = (hi - lo) // heads
            for h in range(heads):
                ref[pl.ds(h, tm, stride=heads), :] = v[:, h * hd:(h + 1) * hd].astype(ref.dtype)
        else:
            ref[...] = v.astype(ref.dtype)


def _norm_proj(x, g, w_bf16, outs, tm):
    n, d = x.shape
    wcols = w_bf16.shape[1]
    shape = lambda rows, lo, hi, hds: (rows * hds, (hi - lo) // hds) if hds else (rows, hi - lo)
    out_shape = [jax.ShapeDtypeStruct(shape(n, lo, hi, hds), dt) for lo, hi, _, dt, hds in outs]
    out_specs = [pl.BlockSpec(shape(tm, lo, hi, hds), lambda i: (i, 0)) for lo, hi, _, _, hds in outs]
    return pl.pallas_call(
        functools.partial(_norm_proj_kernel, cols=tuple((lo, hi, sc, hds) for lo, hi, sc, _, hds in outs)),
        grid=(n // tm,),
        in_specs=[pl.BlockSpec((tm, d), lambda i: (i, 0)),
                  pl.BlockSpec((1, d), lambda i: (0, 0)),
                  pl.BlockSpec((d, wcols), lambda i: (0, 0))],
        out_specs=out_specs,
        out_shape=out_shape,
        compiler_params=_cparams("parallel"),
        name="norm_proj",
    )(x, g.reshape(1, d), w_bf16)


def _suffix_matrix(tk):
    j = np.arange(2 * tk)[:, None]
    s = np.arange(2 * tk)[None, :]
    return jnp.asarray(((j // tk == s // tk) & (j >= s)).astype(np.float32), dtype=BF16)


def _sb_step(q2, kb, vb, u, carry, tk, hd, valid):
    c0, c1, acc = carry
    tq = q2.shape[0]
    lane = lax.broadcasted_iota(I32, kb.shape, 1)
    first = lane < hd
    zero = jnp.zeros_like(kb)
    kk = jnp.concatenate([jnp.where(first, kb, zero), jnp.where(first, zero, kb)], axis=0)
    vv = jnp.concatenate([jnp.where(first, vb, zero), jnp.where(first, zero, vb)], axis=0)
    z = _dot_nt(q2, kk)
    sp = jnp.log(1.0 + jnp.exp(-jnp.abs(z)))
    lk = jnp.minimum(-z, 0.0) - sp
    if valid is not None:
        lk = jnp.where(valid, lk, 0.0)
    m = _dot_split(lk, u)
    c = jnp.concatenate([jnp.broadcast_to(c0, (tq, tk)), jnp.broadcast_to(c1, (tq, tk))], axis=1)
    a = jnp.exp(z + c + m)
    if valid is not None:
        a = jnp.where(valid, a, 0.0)
    acc = acc + _dot(a.astype(BF16), vv)
    c0 = c0 + jnp.sum(lk[:, :tk], axis=-1, keepdims=True)
    c1 = c1 + jnp.sum(lk[:, tk:], axis=-1, keepdims=True)
    return c0, c1, acc


def _sb_alive(carries):
    m = functools.reduce(jnp.maximum, [jnp.maximum(c0, c1) for c0, c1, _ in carries])
    return jnp.max(m) > SB_DEAD_LOG


def _sb_far_blocks(qs, block, nblk, u, carries, tk, hd, first=0):
    n = len(qs)
    flatten = lambda cs: tuple(x for c in cs for x in c)
    unflatten = lambda flat: [tuple(flat[3 * p:3 * p + 3]) for p in range(n)]

    def cond(state):
        return jnp.logical_and(state[0] < nblk, state[1])

    def body(state):
        i = state[0]
        kvs = block(i)
        cs = [_sb_step(qs[p], kvs[p][0], kvs[p][1], u, c, tk, hd, None) for p, c in enumerate(unflatten(state[2:]))]
        return (i + 1, _sb_alive(cs)) + flatten(cs)

    state = lax.while_loop(cond, body, (jnp.int32(first), _sb_alive(carries)) + flatten(carries))
    return unflatten(state[2:])


SB_PAIRS_PER_STEP = 2


def _sb_prompt_kernel(q_ref, k_ref, v_ref, u_ref, o_ref, *, tq, tk, hd, npairs):
    q0 = pl.program_id(2) * tq
    lanes = [slice(p * LANES, (p + 1) * LANES) for p in range(npairs)]
    qs = [q_ref[0, :, ln] for ln in lanes]
    u = u_ref[...]
    row = lax.broadcasted_iota(I32, (tq, 2 * tk), 0)
    koff = lax.broadcasted_iota(I32, (tq, 2 * tk), 1) & (tk - 1)
    carries = [(jnp.zeros((tq, 1), F32), jnp.zeros((tq, 1), F32), jnp.zeros((tq, LANES), F32)) for _ in lanes]
    nd = tq // tk
    for d in range(nd + 1):
        off = (nd - 1 - d) * tk
        limit = row if d < nd else row + jnp.where(q0 >= tk, 0, -4 * tq)
        valid = (koff + off) < limit
        ks = pl.multiple_of(jnp.maximum(q0 + off, 0), tk)
        carries = [_sb_step(qs[p], k_ref[0, pl.ds(ks, tk), ln], v_ref[0, pl.ds(ks, tk), ln], u, carries[p], tk, hd, valid)
                   for p, ln in enumerate(lanes)]

    def block(i):
        ks = pl.multiple_of(q0 - (i + 1) * tk, tk)
        return [(k_ref[0, pl.ds(ks, tk), ln], v_ref[0, pl.ds(ks, tk), ln]) for ln in lanes]

    carries = _sb_far_blocks(qs, block, q0 // tk, u, carries, tk, hd, first=1)
    for p, ln in enumerate(lanes):
        o_ref[0, :, ln] = carries[p][2]


def _sb_prompt(q, k, v, hd, tq=256, tk=128):
    b, t, w = q.shape
    npairs = SB_PAIRS_PER_STEP
    wb = npairs * LANES
    assert w % wb == 0
    return pl.pallas_call(
        functools.partial(_sb_prompt_kernel, tq=tq, tk=tk, hd=hd, npairs=npairs),
        grid=(b, w // wb, t // tq),
        in_specs=[pl.BlockSpec((1, tq, wb), lambda bi, hp, qi: (bi, qi, hp)),
                  pl.BlockSpec((1, t, wb), lambda bi, hp, qi: (bi, 0, hp)),
                  pl.BlockSpec((1, t, wb), lambda bi, hp, qi: (bi, 0, hp)),
                  pl.BlockSpec((2 * tk, 2 * tk), lambda bi, hp, qi: (0, 0))],
        out_specs=pl.BlockSpec((1, tq, wb), lambda bi, hp, qi: (bi, qi, hp)),
        out_shape=jax.ShapeDtypeStruct((b, t, w), F32),
        compiler_params=_cparams("parallel", "parallel", "arbitrary"),
        name="sb_prompt",
    )(q, k, v, _suffix_matrix(tk))


def _sb_sample_kernel(q_ref, kn_ref, vn_ref, kc_ref, vc_ref, un_ref, u_ref, o_ref, *, ts, tk, hd, past):
    q2 = q_ref[0]
    row = lax.broadcasted_iota(I32, (ts, 2 * ts), 0)
    koff = lax.broadcasted_iota(I32, (ts, 2 * ts), 1) & (ts - 1)
    carry = (jnp.zeros((ts, 1), F32), jnp.zeros((ts, 1), F32), jnp.zeros((ts, LANES), F32))
    carry = _sb_step(q2, kn_ref[0], vn_ref[0], un_ref[...], carry, ts, hd, koff < row)
    u = u_ref[...]

    def block(i):
        ks = pl.multiple_of(past - (i + 1) * tk, tk)
        return kc_ref[0, pl.ds(ks, tk), :].astype(BF16), vc_ref[0, pl.ds(ks, tk), :].astype(BF16)

    (carry,) = _sb_far_blocks([q2], lambda i: [block(i)], past // tk, u, [carry], tk, hd)
    o_ref[0] = carry[2]


def _sb_sample(q, kn, vn, kc, vc, hd, tk=128):
    b, ts, w = q.shape
    past = kc.shape[1]
    new_spec = pl.BlockSpec((1, ts, LANES), lambda bi, hp: (bi, 0, hp))
    cache_spec = pl.BlockSpec((1, past, LANES), lambda bi, hp: (bi, 0, hp))
    return pl.pallas_call(
        functools.partial(_sb_sample_kernel, ts=ts, tk=tk, hd=hd, past=past),
        grid=(b, w // LANES),
        in_specs=[new_spec, new_spec, new_spec, cache_spec, cache_spec,
                  pl.BlockSpec((2 * ts, 2 * ts), lambda bi, hp: (0, 0)),
                  pl.BlockSpec((2 * tk, 2 * tk), lambda bi, hp: (0, 0))],
        out_specs=new_spec,
        out_shape=jax.ShapeDtypeStruct((b, ts, w), F32),
        compiler_params=_cparams("parallel", "parallel"),
        name="sb_sample",
    )(q, kn, vn, kc, vc, _suffix_matrix(ts), _suffix_matrix(tk))


def _ret_kernel(q_ref, k_ref, v_ref, cos_ref, sin_ref, dec_ref, qd_ref, kd_ref, gl_ref, bd_ref, s0_ref,
                o_ref, st_ref, s_scr, *, tc, nh, hd, k_scale):
    t = pl.program_id(1)
    w = nh * hd
    L = RET_CHUNK

    @pl.when(t == 0)
    def _():
        s_scr[...] = s0_ref[0]

    lane = lax.broadcasted_iota(I32, (L, w), 1)
    first_half = (lane & (hd - 1)) < hd // 2
    head = lane // hd

    def rope(x, cos, sin):
        swapped = jnp.where(first_half, pltpu.roll(x, w - hd // 2, 1), pltpu.roll(x, hd // 2, 1))
        return x * cos + swapped * sin

    s = s_scr[...]
    for ci in range(tc // L):
        sl = slice(ci * L, (ci + 1) * L)
        cos = cos_ref[sl, :]
        sin = sin_ref[sl, :]
        q = rope(q_ref[0, sl, :], cos, sin)
        k = rope(k_ref[0, sl, :], cos, sin) * k_scale
        qb = q.astype(BF16)
        kb = k.astype(BF16)
        vb = v_ref[0, sl, :].astype(BF16)
        zero = jnp.zeros_like(qb)
        out = _dot(qb, s.astype(BF16)) * qd_ref[...]
        for h in range(nh):
            mh = head == h
            sc = _dot_nt(jnp.where(mh, qb, zero), kb)
            p = (sc * dec_ref[h]).astype(BF16)
            out = out + _dot(p, jnp.where(mh, vb, zero))
        o_ref[0, sl, :] = out
        kd = (k * kd_ref[...]).astype(BF16)
        outer = lax.dot_general(kd, vb, (((0,), (0,)), ((), ())), preferred_element_type=F32)
        s = gl_ref[...] * s + bd_ref[...] * outer
    s_scr[...] = s

    @pl.when(t == pl.num_programs(1) - 1)
    def _():
        st_ref[0] = s


def _retention(ret, pos, s0, nh, hd, tc):
    b, t, _ = ret.shape
    w = nh * hd
    L = RET_CHUNK
    half = hd // 2
    freqs = ROPE_BASE ** (-jnp.arange(half, dtype=F32) / half)
    ang = pos.astype(F32)[:, None] * freqs[None, :]
    cos = jnp.tile(jnp.cos(ang), (1, 2 * nh))
    sin = jnp.tile(jnp.concatenate([-jnp.sin(ang), jnp.sin(ang)], axis=1), (1, nh))
    lg = jnp.log1p(-jnp.exp2(-5.0 - jnp.arange(nh, dtype=F32)))
    idx = jnp.arange(L, dtype=F32)
    diff = idx[:, None] - idx[None, :]
    causal = diff >= 0
    dec = jnp.where(causal[None], jnp.exp(jnp.where(causal, diff, 0.0)[None] * lg[:, None, None]), 0.0)
    lg_lane = jnp.repeat(lg, hd)[None, :]
    qd = jnp.exp((idx + 1.0)[:, None] * lg_lane)
    kd = jnp.exp((L - 1.0 - idx)[:, None] * lg_lane)
    gl = jnp.exp(L * lg_lane)
    hid = jnp.arange(w) // hd
    bd = (hid[:, None] == hid[None, :]).astype(F32)
    tok = lambda j: pl.BlockSpec((1, tc, w), lambda bi, ti: (bi, ti, j))
    tab = pl.BlockSpec((tc, w), lambda bi, ti: (ti, 0))
    const = lambda shape: pl.BlockSpec(shape, lambda bi, ti: (0,) * len(shape))
    state = pl.BlockSpec((1, w, w), lambda bi, ti: (bi, 0, 0))
    return pl.pallas_call(
        functools.partial(_ret_kernel, tc=tc, nh=nh, hd=hd, k_scale=hd ** -0.5),
        grid=(b, t // tc),
        in_specs=[tok(0), tok(1), tok(2), tab, tab, const((nh, L, L)), const((L, w)), const((L, w)),
                  const((1, w)), const((w, w)), state],
        out_specs=[pl.BlockSpec((1, tc, w), lambda bi, ti: (bi, ti, 0)), state],
        out_shape=[jax.ShapeDtypeStruct((b, t, w), F32), jax.ShapeDtypeStruct((b, w, w), F32)],
        scratch_shapes=[pltpu.VMEM((w, w), F32)],
        compiler_params=_cparams("parallel", "arbitrary"),
        name="retention",
    )(ret, ret, ret, cos, sin, dec, qd, kd, gl, bd, s0)


def _block_diag_state(st, nh, hd):
    b = st.shape[0]
    eye = jnp.eye(nh, dtype=st.dtype)
    return jnp.einsum('bhde,hg->bhdge', st, eye).reshape(b, nh * hd, nh * hd)


def _diag_blocks(s, nh, hd):
    b = s.shape[0]
    s5 = s.reshape(b, nh, hd, nh, hd)
    return jnp.stack([s5[:, h, :, h, :] for h in range(nh)], axis=1)


def _head_mean_matrix(w, hd):
    hid = np.arange(w) // hd
    return jnp.asarray((hid[:, None] == hid[None, :]).astype(np.float32) / hd, dtype=BF16)


def _headnorm(x, g_mat, gain):
    ms = _dot_split(x * x, g_mat)
    return x * lax.rsqrt(ms + EPS) * gain


def _merge_kernel(x_ref, sb_ref, ret_ref, gate_ref, qm_ref, mk_ref, mv_ref, gsb_ref, gret_ref, gmo_ref,
                  msb_ref, mrt_ref, wo_ref, x1_ref, *, nh_mem, hd):
    wsb = sb_ref.shape[2]
    wrt = ret_ref.shape[2]
    qm = qm_ref[0]
    mkb = mk_ref[0].astype(BF16)
    mvb = mv_ref[0].astype(BF16)
    head_q = lax.broadcasted_iota(I32, qm.shape, 1) // hd
    head_m = lax.broadcasted_iota(I32, mkb.shape, 1) // hd
    mem_o = jnp.zeros(qm.shape, F32)
    for h in range(nh_mem):
        s = _dot_nt(jnp.where(head_q == h, qm, jnp.zeros_like(qm)), mkb)
        e = jnp.exp(s - jnp.max(s, axis=-1, keepdims=True))
        p = e / jnp.sum(e, axis=-1, keepdims=True)
        mem_o = mem_o + _dot(p.astype(BF16), jnp.where(head_m == h, mvb, jnp.zeros_like(mvb)))
    sb_n = _headnorm(sb_ref[0], msb_ref[...], gsb_ref[...])
    gate = gate_ref[0]
    ret_n = _headnorm(ret_ref[0], mrt_ref[...], gret_ref[...]) * (gate / (1.0 + jnp.exp(-gate)))
    mem_n = _headnorm(mem_o, mrt_ref[...], gmo_ref[...])
    y = (_dot(sb_n.astype(BF16), wo_ref[0:wsb, :])
         + _dot(ret_n.astype(BF16), wo_ref[wsb:wsb + wrt, :])
         + _dot(mem_n.astype(BF16), wo_ref[wsb + wrt:, :]))
    x1_ref[0] = x_ref[0] + y


def _merge(x, sb_o, ret_o, ret, qm, mk, mv, g_sb, g_ret, g_mo, wo_bf16, hd, tm):
    b, t, d = x.shape
    wsb = sb_o.shape[2]
    wrt = ret_o.shape[2]
    nmem = mk.shape[1]
    tok = lambda wd, j=0: pl.BlockSpec((1, tm, wd), lambda bi, ti: (bi, ti, j))
    const = lambda shape: pl.BlockSpec(shape, lambda bi, ti: (0,) * len(shape))
    memspec = pl.BlockSpec((1, nmem, wrt), lambda bi, ti: (bi, 0, 0))
    return pl.pallas_call(
        functools.partial(_merge_kernel, nh_mem=wrt // hd, hd=hd),
        grid=(b, t // tm),
        in_specs=[tok(d), tok(wsb), tok(wrt), tok(wrt, 3), tok(wrt), memspec, memspec,
                  const((1, wsb)), const((1, wrt)), const((1, wrt)),
                  const((wsb, wsb)), const((wrt, wrt)), const((d, d))],
        out_specs=tok(d),
        out_shape=jax.ShapeDtypeStruct((b, t, d), F32),
        compiler_params=_cparams("parallel", "parallel"),
        name="merge",
    )(x, sb_o, ret_o, ret, qm, mk, mv, g_sb.reshape(1, wsb), g_ret.reshape(1, wrt), g_mo.reshape(1, wrt),
      _head_mean_matrix(wsb, hd), _head_mean_matrix(wrt, hd), wo_bf16)


def _topk_rows(s, k, rid=None):
    if rid is None:
        rid = lax.broadcasted_iota(I32, s.shape, 0)
    big = jnp.int32(2 ** 30)
    vals, idxs = [], []
    for _ in range(k):
        m = jnp.max(s, axis=0, keepdims=True)
        i = jnp.min(jnp.where(s == m, rid, big), axis=0, keepdims=True)
        vals.append(m)
        idxs.append(i)
        s = jnp.where(rid == i, -jnp.inf, s)
    return jnp.concatenate(vals, axis=0), jnp.concatenate(idxs, axis=0)


def _staircase_blocks(k, rows=8):
    blocks = []
    singles = [a for a in range(k) if k // (a + 1) == 1]
    for a in range(k):
        nb = k // (a + 1)
        if nb > 1:
            blocks.extend(('row', a, b0) for b0 in range(0, nb, rows))
    assert singles and singles[0] % rows == 0 and len(singles) % rows == 0 and singles[-1] == k - 1
    blocks.extend(('col', a0, 0) for a0 in range(singles[0], k, rows))
    flat = []
    for kind, a, b in blocks:
        flat.extend([a * k + b + i for i in range(rows)] if kind == 'row' else [(a + i) * k + b for i in range(rows)])
    need = {(a, b) for a in range(k) for b in range(k) if (a + 1) * (b + 1) <= k}
    assert need <= {(f // k, f % k) for f in flat} and len(set(flat)) == len(flat)
    return blocks, np.array(flat, np.int32)[:, None]


def _take_rows(tbl, idx):
    out = jnp.zeros(idx.shape, tbl.dtype)
    for a in range(tbl.shape[0]):
        out = out + jnp.where(idx == a, tbl[a:a + 1, :], 0)
    return out


def _peer_score_kernel(x_ref, g_ref, wq_ref, sk_ref, cf_ref, h_ref, eid_ref, gw_ref, hb_scr,
                       eid_scr, gw_scr, sc_scr, *, topk, nkeys, blocks):
    head = pl.program_id(1)

    @pl.when(head == 0)
    def _():
        x = x_ref[...]
        ms = jnp.mean(x * x, axis=-1, keepdims=True)
        hn = x * lax.rsqrt(ms + EPS) * g_ref[...]
        for r in range(ROW_TILES):
            h_ref[pl.ds(r, hn.shape[0], stride=ROW_TILES), :] = hn[:, r * LANES:(r + 1) * LANES]
        hb_scr[...] = hn.astype(BF16)

    qb = _dot(hb_scr[...], wq_ref[...]).astype(BF16)
    hq = qb.shape[1] // 2
    sc_scr[0] = _dot_nt(sk_ref[0], qb[:, :hq])
    sc_scr[1] = _dot_nt(sk_ref[1], qb[:, hq:])
    brows = cf_ref.shape[0] // len(blocks)
    out_rows = pl.ds(pl.multiple_of(head * topk, topk), topk)
    width = min(LANES, qb.shape[0])

    def column_group(c):
        cols = slice(c * width, (c + 1) * width)
        sv0, si0 = _topk_rows(sc_scr[0, :, cols], topk)
        sv1, si1 = _topk_rows(sc_scr[1, :, cols], topk)
        cand = jnp.concatenate(
            [sv0[a:a + 1, :] + sv1[b:b + brows, :] if kind == 'row' else sv0[a:a + brows, :] + sv1[b:b + 1, :]
             for kind, a, b in blocks], axis=0)
        cv, ci = _topk_rows(cand, topk, cf_ref[:, cols])
        e1 = _take_rows(si0, ci // topk)
        e2 = _take_rows(si1, ci % topk)
        eid_scr[out_rows, cols] = ((e1 * nkeys + e2) * HALF_ROWS).astype(F32)
        e = jnp.exp(cv - cv[0:1, :])
        gw_scr[out_rows, cols] = e / jnp.sum(e, axis=0, keepdims=True)

    for c in range(qb.shape[0] // width):
        column_group(c)

    @pl.when(head == pl.num_programs(1) - 1)
    def _():
        eid_ref[...] = eid_scr[...].T.astype(I32)
        gw_ref[...] = gw_scr[...].T


def _peer_score(x, g, wq_bf16, sk_bf16, tm):
    n, d = x.shape
    nkeys, khalf = sk_bf16.shape[1], sk_bf16.shape[2]
    nheads = wq_bf16.shape[1] // (2 * khalf)
    k = PEER_TOPK
    blocks, flat = _staircase_blocks(k)
    cf = jnp.asarray(np.broadcast_to(flat, (flat.shape[0], tm)))
    xs = pl.BlockSpec((tm, d), lambda i, h: (i, 0))
    cs = pl.BlockSpec(cf.shape, lambda i, h: (0, 0))
    return pl.pallas_call(
        functools.partial(_peer_score_kernel, topk=k, nkeys=nkeys, blocks=tuple(blocks)),
        grid=(n // tm, nheads),
        in_specs=[xs, pl.BlockSpec((1, d), lambda i, h: (0, 0)),
                  pl.BlockSpec((d, 2 * khalf), lambda i, h: (0, h)),
                  pl.BlockSpec((2, nkeys, khalf), lambda i, h: (0, 0, 0)), cs],
        out_specs=[pl.BlockSpec((tm * ROW_TILES, LANES), lambda i, h: (i, 0)),
                   pl.BlockSpec((tm, nheads * k), lambda i, h: (i, 0)),
                   pl.BlockSpec((tm, nheads * k), lambda i, h: (i, 0))],
        out_shape=[jax.ShapeDtypeStruct((n * ROW_TILES, LANES), F32),
                   jax.ShapeDtypeStruct((n, nheads * k), I32), jax.ShapeDtypeStruct((n, nheads * k), F32)],
        scratch_shapes=[pltpu.VMEM((tm, d), BF16), pltpu.VMEM((nheads * k, tm), F32),
                        pltpu.VMEM((nheads * k, tm), F32), pltpu.VMEM((2, nkeys, tm), F32)],
        compiler_params=_cparams("parallel", "arbitrary"),
        name="peer_score",
    )(x, g.reshape(1, d), wq_bf16, sk_bf16, cf)


def _pack_table(tab):
    e, d = tab.shape
    assert d == ROW_TILES * LANES
    te = _row_tile(e, 512)
    return pl.pallas_call(
        _pack_kernel,
        grid=(e // te,),
        in_specs=[pl.BlockSpec((te, d), lambda i: (i, 0))],
        out_specs=pl.BlockSpec((te * HALF_ROWS, LANES), lambda i: (i, 0)),
        out_shape=jax.ShapeDtypeStruct((e * HALF_ROWS, LANES), I32),
        compiler_params=_cparams("parallel"),
        name="pack_table",
    )(tab)


def _pack_words(even, odd):
    odd = lax.bitcast_convert_type(odd, I32)
    low = lax.shift_right_logical(lax.bitcast_convert_type(even.astype(BF16).astype(F32), I32), 16)
    sign = odd & jnp.int32(-2 ** 31)
    mag = (odd & jnp.int32(2 ** 31 - 1)) + jnp.int32(0x8000)
    high = jnp.where(mag >= low, (mag - low) >> 16, 0)
    return sign | (high << 16) | low


def _pack_kernel(t_ref, o_ref):
    te = t_ref.shape[0]
    x = t_ref[...]
    for s in range(HALF_ROWS):
        o_ref[pl.ds(s, te, stride=HALF_ROWS), :] = _pack_words(x[:, (2 * s) * LANES:(2 * s + 1) * LANES],
                                                               x[:, (2 * s + 1) * LANES:(2 * s + 2) * LANES])


def _unpack(wd):
    even = lax.bitcast_convert_type(wd << 16, F32)
    odd = lax.bitcast_convert_type(wd, F32)
    return even, odd


def _erf(x):
    return lax.erf(x)


def _token_tiles(ref, t):
    base = pl.multiple_of(t * ROW_TILES, ROW_TILES)
    return ref[pl.ds(base, HALF_ROWS, stride=2), :], ref[pl.ds(base + 1, HALF_ROWS, stride=2), :]


PEER_U_TOKENS = 8


def _peer_u_kernel(eid_ref, h_ref, gw_ref, tab_ref, w_ref, pa, pb, *, tg, npairs):
    token_lane = lax.broadcasted_iota(I32, (npairs, tg), 1)

    def products(t, p_scr):
        he, ho = _token_tiles(h_ref, t)
        he2 = jnp.concatenate([he, he], axis=0)
        ho2 = jnp.concatenate([ho, ho], axis=0)
        for j in range(0, npairs, 2):
            wa = tab_ref[pl.ds(pl.multiple_of(eid_ref[t, j], HALF_ROWS), HALF_ROWS), :]
            wb = tab_ref[pl.ds(pl.multiple_of(eid_ref[t, j + 1], HALF_ROWS), HALF_ROWS), :]
            even, odd = _unpack(jnp.concatenate([wa, wb], axis=0))
            p_scr[j * HALF_ROWS:(j + 2) * HALF_ROWS, :] = even * he2 + odd * ho2

    def reduce_into(act_t, p_scr, t):
        q = p_scr[pl.ds(0, npairs, stride=HALF_ROWS), :]
        for s in range(1, HALF_ROWS):
            q = q + p_scr[pl.ds(s, npairs, stride=HALF_ROWS), :]
        return jnp.where(token_lane == t, jnp.sum(q, axis=-1, keepdims=True), act_t)

    def body(i, act_t):
        t0 = PEER_U_TOKENS * i
        bufs = (pa, pb)
        products(t0, pa)
        for k in range(1, PEER_U_TOKENS):
            products(t0 + k, bufs[k % 2])
            act_t = reduce_into(act_t, bufs[(k - 1) % 2], t0 + k - 1)
        return reduce_into(act_t, bufs[(PEER_U_TOKENS - 1) % 2], t0 + PEER_U_TOKENS - 1)

    act_t = lax.fori_loop(0, tg // PEER_U_TOKENS, body, jnp.zeros((npairs, tg), F32))
    act = act_t.T
    gelu = 0.5 * act * (1.0 + _erf(act * (2.0 ** -0.5)))
    w = gw_ref[...] * gelu
    w_ref[...] = _pack_words(w, pltpu.roll(w, npairs - 1, 1))


def _peer_u(eid, h, gw, tab_packed, tg):
    n, npairs = eid.shape
    assert npairs == LANES and h.shape == (n * ROW_TILES, LANES) and tg % PEER_U_TOKENS == 0
    tspec = pl.BlockSpec((tg, npairs), lambda i: (i, 0))
    pshape = pltpu.VMEM((npairs * HALF_ROWS, LANES), F32)
    return pl.pallas_call(
        functools.partial(_peer_u_kernel, tg=tg, npairs=npairs),
        grid=(n // tg,),
        in_specs=[pl.BlockSpec((tg, npairs), lambda i: (i, 0), memory_space=pltpu.SMEM),
                  pl.BlockSpec((tg * ROW_TILES, LANES), lambda i: (i, 0)), tspec,
                  pl.BlockSpec(memory_space=pltpu.VMEM)],
        out_specs=tspec,
        out_shape=jax.ShapeDtypeStruct((n, npairs), I32),
        scratch_shapes=[pshape, pshape],
        compiler_params=_cparams("arbitrary"),
        name="peer_u",
    )(eid, h, gw, tab_packed)


PEER_V_ACCUMULATORS = 2


def _peer_v_kernel(eid_ref, w_ref, tab_ref, o_ref, *, tg, npairs):
    nacc = PEER_V_ACCUMULATORS

    def token(t):
        acc_e = [jnp.zeros((HALF_ROWS, LANES), F32) for _ in range(nacc)]
        acc_o = [jnp.zeros((HALF_ROWS, LANES), F32) for _ in range(nacc)]
        for j in range(npairs):
            if j % 2 == 0:
                wpair = _unpack(jnp.full((HALF_ROWS, LANES), w_ref[t, j], I32))
            wt = wpair[j % 2]
            e4 = pl.multiple_of(eid_ref[t, j], HALF_ROWS)
            even, odd = _unpack(tab_ref[pl.ds(e4, HALF_ROWS), :])
            acc_e[j % nacc] = acc_e[j % nacc] + wt * even
            acc_o[j % nacc] = acc_o[j % nacc] + wt * odd
        base = pl.multiple_of(t * ROW_TILES, ROW_TILES)
        o_ref[pl.ds(base, HALF_ROWS, stride=2), :] = functools.reduce(lambda a, b: a + b, acc_e)
        o_ref[pl.ds(base + 1, HALF_ROWS, stride=2), :] = functools.reduce(lambda a, b: a + b, acc_o)

    def body(i, carry):
        token(2 * i)
        token(2 * i + 1)
        return carry

    lax.fori_loop(0, tg // 2, body, 0)


def _peer_v(eid, w, tab_packed, tg):
    n, npairs = eid.shape
    assert tg % 2 == 0
    sspec = pl.BlockSpec((tg, npairs), lambda i: (i, 0), memory_space=pltpu.SMEM)
    return pl.pallas_call(
        functools.partial(_peer_v_kernel, tg=tg, npairs=npairs),
        grid=(n // tg,),
        in_specs=[sspec, sspec, pl.BlockSpec(memory_space=pltpu.VMEM)],
        out_specs=pl.BlockSpec((tg * ROW_TILES, LANES), lambda i: (i, 0)),
        out_shape=jax.ShapeDtypeStruct((n * ROW_TILES, LANES), F32),
        compiler_params=_cparams("arbitrary"),
        name="peer_v",
    )(eid, w, tab_packed)


def _final_kernel(x_ref, p_ref, g_ref, y_ref):
    tm = x_ref.shape[0]
    p = jnp.concatenate([p_ref[pl.ds(r, tm, stride=ROW_TILES), :] for r in range(ROW_TILES)], axis=1)
    x = x_ref[...] + p
    ms = jnp.mean(x * x, axis=-1, keepdims=True)
    y_ref[...] = x * lax.rsqrt(ms + EPS) * g_ref[...]


def _final(x, p, g, tm):
    n, d = x.shape
    xs = pl.BlockSpec((tm, d), lambda i: (i, 0))
    return pl.pallas_call(
        _final_kernel,
        grid=(n // tm,),
        in_specs=[xs, pl.BlockSpec((tm * ROW_TILES, LANES), lambda i: (i, 0)), pl.BlockSpec((1, d), lambda i: (0, 0))],
        out_specs=xs,
        out_shape=jax.ShapeDtypeStruct((n, d), F32),
        compiler_params=_cparams("parallel"),
        name="final_norm",
    )(x, p, g.reshape(1, d))


def _row_tile(n, pref):
    t = min(pref, n)
    assert n % t == 0, (n, t)
    return t


def _stream(x, w_in_bf, g_mix, dims):
    b, t, d = x.shape
    wsb, wrt, wmm, hd = dims
    scale = hd ** -0.5
    o = 3 * wsb
    outs = [(0, wsb, scale, BF16, 0),
            (wsb, 2 * wsb, 1.0, F32, wsb // hd), (2 * wsb, o, 1.0, F32, wsb // hd),
            (wsb, 2 * wsb, 1.0, BF16, 0), (2 * wsb, o, 1.0, BF16, 0),
            (o, o + 4 * wrt, 1.0, F32, 0),
            (o + 4 * wrt, o + 4 * wrt + wmm, scale, BF16, 0)]
    res = _norm_proj(x.reshape(b * t, d), g_mix, w_in_bf, outs, _row_tile(b * t, 512))
    return [r.reshape(b, t, r.shape[1]) if r.shape[0] == b * t else r for r in res]


def _peer_and_final(x1, g_ffn, wq_bf, sk_bf, u_packed, v_packed, g_final):
    n = x1.shape[0]
    h, eid, gw = _peer_score(x1, g_ffn, wq_bf, sk_bf, _row_tile(n, 512))
    tg = _row_tile(n, 128)
    w = _peer_u(eid, h, gw, u_packed, tg)
    p = _peer_v(eid, w, v_packed, tg)
    return _final(x1, p, g_final, _row_tile(n, 512))


def kernel(x_prompt, x_sample, mem_prompt, cache_sb_k, cache_sb_v, state_ret, cache_mem_k, cache_mem_v,
           g_mix, w_in, w_out, g_sb_out, g_ret_out, g_mem_out, g_mem, w_mem_kv,
           g_ffn, w_peer_q, peer_sub_keys, peer_u, peer_v, g_final):
    depth = w_in.shape[0]
    assert depth == 1, "single-layer step"
    bp, tp, d = x_prompt.shape
    bs, ts, _ = x_sample.shape
    _, _, past, sb_heads, hd = cache_sb_k.shape
    ret_heads = state_ret.shape[2]
    mem_heads = cache_mem_k.shape[3]
    nmem = mem_prompt.shape[1]
    wsb, wrt, wmm = sb_heads * hd, ret_heads * hd, mem_heads * hd
    assert hd == 64 and wrt == wmm and d == 2 * HALF_ROWS * LANES
    l = 0
    w_in_bf = w_in[l].astype(BF16)
    wo_bf = w_out[l].astype(BF16)
    dims = (wsb, wrt, wmm, hd)

    q_p, k_p, v_p, kb_p, vb_p, ret_p, qm_p = _stream(x_prompt, w_in_bf, g_mix[l], dims)
    sb_p = _sb_prompt(q_p, kb_p, vb_p, hd)
    zeros_state = jnp.zeros((bp, wrt, wrt), F32)
    ro_p, st_p = _retention(ret_p, jnp.arange(tp, dtype=I32), zeros_state, ret_heads, hd, _row_tile(tp, 512))
    mk_p, mv_p = _norm_proj(mem_prompt.reshape(bp * nmem, d), g_mem[l], w_mem_kv[l].astype(BF16),
                            [(0, wmm, 1.0, F32, 0), (wmm, 2 * wmm, 1.0, F32, 0)], _row_tile(bp * nmem, 512))
    mk_p = mk_p.reshape(bp, nmem, wmm)
    mv_p = mv_p.reshape(bp, nmem, wmm)
    x1_p = _merge(x_prompt, sb_p, ro_p, ret_p, qm_p, mk_p, mv_p, g_sb_out[l], g_ret_out[l], g_mem_out[l],
                  wo_bf, hd, _row_tile(tp, 256))

    q_s, k_s, v_s, kb_s, vb_s, ret_s, qm_s = _stream(x_sample, w_in_bf, g_mix[l], dims)
    sb_s = _sb_sample(q_s, kb_s, vb_s, cache_sb_k[l].reshape(bs, past, wsb), cache_sb_v[l].reshape(bs, past, wsb), hd)
    s0 = _block_diag_state(state_ret[l].astype(F32), ret_heads, hd)
    ro_s, st_s = _retention(ret_s, past + jnp.arange(ts, dtype=I32), s0, ret_heads, hd, _row_tile(ts, 512))
    x1_s = _merge(x_sample, sb_s, ro_s, ret_s, qm_s, cache_mem_k[l].reshape(bs, nmem, wmm),
                  cache_mem_v[l].reshape(bs, nmem, wmm), g_sb_out[l], g_ret_out[l], g_mem_out[l],
                  wo_bf, hd, _row_tile(ts, 256))

    wq_bf = w_peer_q[l].astype(BF16)
    sk_bf = peer_sub_keys[l].astype(BF16)
    u_packed = _pack_table(peer_u[l])
    v_packed = _pack_table(peer_v[l])
    y_p = _peer_and_final(x1_p.reshape(bp * tp, d), g_ffn[l], wq_bf, sk_bf, u_packed, v_packed, g_final)
    y_s = _peer_and_final(x1_s.reshape(bs * ts, d), g_ffn[l], wq_bf, sk_bf, u_packed, v_packed, g_final)

    return (y_p.reshape(bp, tp, d), y_s.reshape(bs, ts, d),
            k_p.reshape(1, bp, tp, sb_heads, hd), v_p.reshape(1, bp, tp, sb_heads, hd),
            _diag_blocks(st_p, ret_heads, hd)[None],
            mk_p.reshape(1, bp, nmem, mem_heads, hd), mv_p.reshape(1, bp, nmem, mem_heads, hd),
            k_s.reshape(1, bs, ts, sb_heads, hd), v_s.reshape(1, bs, ts, sb_heads, hd),
            _diag_blocks(st_s, ret_heads, hd)[None].astype(state_ret.dtype))
```

```python
import functools

import numpy as np
import jax
import jax.numpy as jnp
from jax import lax
from jax.experimental import pallas as pl
from jax.experimental.pallas import tpu as pltpu

F32 = jnp.float32
BF16 = jnp.bfloat16
I32 = jnp.int32

EPS = 1e-6
ROPE_BASE = 10000.0
PEER_TOPK = 16
RET_CHUNK = 64
LANES = 128
ROW_TILES = 8
HALF_ROWS = 4
VMEM_LIMIT = 56 * 1024 * 1024
SB_DEAD_LOG = -105.0


def _cparams(*sem):
    return pltpu.CompilerParams(dimension_semantics=sem, vmem_limit_bytes=VMEM_LIMIT)


def _dot(a, b):
    return jnp.dot(a, b, preferred_element_type=F32)


def _dot_nt(a, b):
    return lax.dot_general(a, b, (((1,), (1,)), ((), ())), preferred_element_type=F32)


def _split_bf16(x):
    hi = x.astype(BF16)
    lo = (x - hi.astype(F32)).astype(BF16)
    return hi, lo


def _dot_split(x, m_bf16):
    hi, lo = _split_bf16(x)
    return _dot(hi, m_bf16) + _dot(lo, m_bf16)


def _norm_proj_kernel(x_ref, g_ref, w_ref, *out_refs, cols):
    x = x_ref[...]
    ms = jnp.mean(x * x, axis=-1, keepdims=True)
    h = x * lax.rsqrt(ms + EPS) * g_ref[...]
    p = _dot(h.astype(BF16), w_ref[...])
    tm = x.shape[0]
    for ref, (lo, hi, scale, heads) in zip(out_refs, cols):
        v = p[:, lo:hi]
        if scale != 1.0:
            v = v * scale
        if heads:
            hd = (hi - lo) // heads
            for h in range(heads):
                ref[pl.ds(h, tm, stride=heads), :] = v[:, h * hd:(h + 1) * hd].astype(ref.dtype)
        else:
            ref[...] = v.astype(ref.dtype)


def _norm_proj(x, g, w_bf16, outs, tm):
    n, d = x.shape
    wcols = w_bf16.shape[1]
    shape = lambda rows, lo, hi, hds: (rows * hds, (hi - lo) // hds) if hds else (rows, hi - lo)
    out_shape = [jax.ShapeDtypeStruct(shape(n, lo, hi, hds), dt) for lo, hi, _, dt, hds in outs]
    out_specs = [pl.BlockSpec(shape(tm, lo, hi, hds), lambda i: (i, 0)) for lo, hi, _, _, hds in outs]
    return pl.pallas_call(
        functools.partial(_norm_proj_kernel, cols=tuple((lo, hi, sc, hds) for lo, hi, sc, _, hds in outs)),
        grid=(n // tm,),
        in_specs=[pl.BlockSpec((tm, d), lambda i: (i, 0)),
                  pl.BlockSpec((1, d), lambda i: (0, 0)),
                  pl.BlockSpec((d, wcols), lambda i: (0, 0))],
        out_specs=out_specs,
        out_shape=out_shape,
        compiler_params=_cparams("parallel"),
        name="norm_proj",
    )(x, g.reshape(1, d), w_bf16)


def _suffix_matrix(tk):
    j = np.arange(2 * tk)[:, None]
    s = np.arange(2 * tk)[None, :]
    return jnp.asarray(((j // tk == s // tk) & (j >= s)).astype(np.float32), dtype=BF16)


def _sb_step(q2, kb, vb, u, carry, tk, hd, valid):
    c0, c1, acc = carry
    tq = q2.shape[0]
    lane = lax.broadcasted_iota(I32, kb.shape, 1)
    first = lane < hd
    zero = jnp.zeros_like(kb)
    kk = jnp.concatenate([jnp.where(first, kb, zero), jnp.where(first, zero, kb)], axis=0)
    vv = jnp.concatenate([jnp.where(first, vb, zero), jnp.where(first, zero, vb)], axis=0)
    z = _dot_nt(q2, kk)
    sp = jnp.log(1.0 + jnp.exp(-jnp.abs(z)))
    lk = jnp.minimum(-z, 0.0) - sp
    if valid is not None:
        lk = jnp.where(valid, lk, 0.0)
    m = _dot_split(lk, u)
    c = jnp.concatenate([jnp.broadcast_to(c0, (tq, tk)), jnp.broadcast_to(c1, (tq, tk))], axis=1)
    a = jnp.exp(z + c + m)
    if valid is not None:
        a = jnp.where(valid, a, 0.0)
    acc = acc + _dot(a.astype(BF16), vv)
    c0 = c0 + jnp.sum(lk[:, :tk], axis=-1, keepdims=True)
    c1 = c1 + jnp.sum(lk[:, tk:], axis=-1, keepdims=True)
    return c0, c1, acc


def _sb_alive(carries):
    m = functools.reduce(jnp.maximum, [jnp.maximum(c0, c1) for c0, c1, _ in carries])
    return jnp.max(m) > SB_DEAD_LOG


def _sb_far_blocks(qs, block, nblk, u, carries, tk, hd, first=0):
    n = len(qs)
    flatten = lambda cs: tuple(x for c in cs for x in c)
    unflatten = lambda flat: [tuple(flat[3 * p:3 * p + 3]) for p in range(n)]

    def cond(state):
        return jnp.logical_and(state[0] < nblk, state[1])

    def body(state):
        i = state[0]
        kvs = block(i)
        cs = [_sb_step(qs[p], kvs[p][0], kvs[p][1], u, c, tk, hd, None) for p, c in enumerate(unflatten(state[2:]))]
        return (i + 1, _sb_alive(cs)) + flatten(cs)

    state = lax.while_loop(cond, body, (jnp.int32(first), _sb_alive(carries)) + flatten(carries))
    return unflatten(state[2:])


SB_PAIRS_PER_STEP = 4


def _sb_prompt_kernel(q_ref, k_ref, v_ref, u_ref, o_ref, *, tq, tk, hd, npairs):
    q0 = pl.program_id(2) * tq
    lanes = [slice(p * LANES, (p + 1) * LANES) for p in range(npairs)]
    qs = [q_ref[0, :, ln] for ln in lanes]
    u = u_ref[...]
    row = lax.broadcasted_iota(I32, (tq, 2 * tk), 0)
    koff = lax.broadcasted_iota(I32, (tq, 2 * tk), 1) & (tk - 1)
    carries = [(jnp.zeros((tq, 1), F32), jnp.zeros((tq, 1), F32), jnp.zeros((tq, LANES), F32)) for _ in lanes]
    nd = tq // tk
    for d in range(nd + 1):
        off = (nd - 1 - d) * tk
        limit = row if d < nd else row + jnp.where(q0 >= tk, 0, -4 * tq)
        valid = (koff + off) < limit
        ks = pl.multiple_of(jnp.maximum(q0 + off, 0), tk)
        carries = [_sb_step(qs[p], k_ref[0, pl.ds(ks, tk), ln], v_ref[0, pl.ds(ks, tk), ln], u, carries[p], tk, hd, valid)
                   for p, ln in enumerate(lanes)]

    def block(i):
        ks = pl.multiple_of(q0 - (i + 1) * tk, tk)
        return [(k_ref[0, pl.ds(ks, tk), ln], v_ref[0, pl.ds(ks, tk), ln]) for ln in lanes]

    carries = _sb_far_blocks(qs, block, q0 // tk, u, carries, tk, hd, first=1)
    for p, ln in enumerate(lanes):
        o_ref[0, :, ln] = carries[p][2]


def _sb_prompt(q, k, v, hd, tq=256, tk=128):
    b, t, w = q.shape
    npairs = SB_PAIRS_PER_STEP
    wb = npairs * LANES
    assert w % wb == 0
    return pl.pallas_call(
        functools.partial(_sb_prompt_kernel, tq=tq, tk=tk, hd=hd, npairs=npairs),
        grid=(b, w // wb, t // tq),
        in_specs=[pl.BlockSpec((1, tq, wb), lambda bi, hp, qi: (bi, qi, hp)),
                  pl.BlockSpec((1, t, wb), lambda bi, hp, qi: (bi, 0, hp)),
                  pl.BlockSpec((1, t, wb), lambda bi, hp, qi: (bi, 0, hp)),
                  pl.BlockSpec((2 * tk, 2 * tk), lambda bi, hp, qi: (0, 0))],
        out_specs=pl.BlockSpec((1, tq, wb), lambda bi, hp, qi: (bi, qi, hp)),
        out_shape=jax.ShapeDtypeStruct((b, t, w), F32),
        compiler_params=_cparams("parallel", "parallel", "arbitrary"),
        name="sb_prompt",
    )(q, k, v, _suffix_matrix(tk))


def _sb_sample_kernel(q_ref, kn_ref, vn_ref, kc_ref, vc_ref, un_ref, u_ref, o_ref, *, ts, tk, hd, past):
    q2 = q_ref[0]
    row = lax.broadcasted_iota(I32, (ts, 2 * ts), 0)
    koff = lax.broadcasted_iota(I32, (ts, 2 * ts), 1) & (ts - 1)
    carry = (jnp.zeros((ts, 1), F32), jnp.zeros((ts, 1), F32), jnp.zeros((ts, LANES), F32))
    carry = _sb_step(q2, kn_ref[0], vn_ref[0], un_ref[...], carry, ts, hd, koff < row)
    u = u_ref[...]

    def block(i):
        ks = pl.multiple_of(past - (i + 1) * tk, tk)
        return kc_ref[0, pl.ds(ks, tk), :].astype(BF16), vc_ref[0, pl.ds(ks, tk), :].astype(BF16)

    (carry,) = _sb_far_blocks([q2], lambda i: [block(i)], past // tk, u, [carry], tk, hd)
    o_ref[0] = carry[2]


def _sb_sample(q, kn, vn, kc, vc, hd, tk=128):
    b, ts, w = q.shape
    past = kc.shape[1]
    new_spec = pl.BlockSpec((1, ts, LANES), lambda bi, hp: (bi, 0, hp))
    cache_spec = pl.BlockSpec((1, past, LANES), lambda bi, hp: (bi, 0, hp))
    return pl.pallas_call(
        functools.partial(_sb_sample_kernel, ts=ts, tk=tk, hd=hd, past=past),
        grid=(b, w // LANES),
        in_specs=[new_spec, new_spec, new_spec, cache_spec, cache_spec,
                  pl.BlockSpec((2 * ts, 2 * ts), lambda bi, hp: (0, 0)),
                  pl.BlockSpec((2 * tk, 2 * tk), lambda bi, hp: (0, 0))],
        out_specs=new_spec,
        out_shape=jax.ShapeDtypeStruct((b, ts, w), F32),
        compiler_params=_cparams("parallel", "parallel"),
        name="sb_sample",
    )(q, kn, vn, kc, vc, _suffix_matrix(ts), _suffix_matrix(tk))


def _ret_kernel(q_ref, k_ref, v_ref, cos_ref, sin_ref, dec_ref, qd_ref, kd_ref, gl_ref, bd_ref, s0_ref,
                o_ref, st_ref, s_scr, *, tc, nh, hd, k_scale):
    t = pl.program_id(1)
    w = nh * hd
    L = RET_CHUNK

    @pl.when(t == 0)
    def _():
        s_scr[...] = s0_ref[0]

    lane = lax.broadcasted_iota(I32, (L, w), 1)
    first_half = (lane & (hd - 1)) < hd // 2
    head = lane // hd

    def rope(x, cos, sin):
        swapped = jnp.where(first_half, pltpu.roll(x, w - hd // 2, 1), pltpu.roll(x, hd // 2, 1))
        return x * cos + swapped * sin

    s = s_scr[...]
    for ci in range(tc // L):
        sl = slice(ci * L, (ci + 1) * L)
        cos = cos_ref[sl, :]
        sin = sin_ref[sl, :]
        q = rope(q_ref[0, sl, :], cos, sin)
        k = rope(k_ref[0, sl, :], cos, sin) * k_scale
        qb = q.astype(BF16)
        kb = k.astype(BF16)
        vb = v_ref[0, sl, :].astype(BF16)
        zero = jnp.zeros_like(qb)
        out = _dot(qb, s.astype(BF16)) * qd_ref[...]
        for h in range(nh):
            mh = head == h
            sc = _dot_nt(jnp.where(mh, qb, zero), kb)
            p = (sc * dec_ref[h]).astype(BF16)
            out = out + _dot(p, jnp.where(mh, vb, zero))
        o_ref[0, sl, :] = out
        kd = (k * kd_ref[...]).astype(BF16)
        outer = lax.dot_general(kd, vb, (((0,), (0,)), ((), ())), preferred_element_type=F32)
        s = gl_ref[...] * s + bd_ref[...] * outer
    s_scr[...] = s

    @pl.when(t == pl.num_programs(1) - 1)
    def _():
        st_ref[0] = s


def _retention(ret, pos, s0, nh, hd, tc):
    b, t, _ = ret.shape
    w = nh * hd
    L = RET_CHUNK
    half = hd // 2
    freqs = ROPE_BASE ** (-jnp.arange(half, dtype=F32) / half)
    ang = pos.astype(F32)[:, None] * freqs[None, :]
    cos = jnp.tile(jnp.cos(ang), (1, 2 * nh))
    sin = jnp.tile(jnp.concatenate([-jnp.sin(ang), jnp.sin(ang)], axis=1), (1, nh))
    lg = jnp.log1p(-jnp.exp2(-5.0 - jnp.arange(nh, dtype=F32)))
    idx = jnp.arange(L, dtype=F32)
    diff = idx[:, None] - idx[None, :]
    causal = diff >= 0
    dec = jnp.where(causal[None], jnp.exp(jnp.where(causal, diff, 0.0)[None] * lg[:, None, None]), 0.0)
    lg_lane = jnp.repeat(lg, hd)[None, :]
    qd = jnp.exp((idx + 1.0)[:, None] * lg_lane)
    kd = jnp.exp((L - 1.0 - idx)[:, None] * lg_lane)
    gl = jnp.exp(L * lg_lane)
    hid = jnp.arange(w) // hd
    bd = (hid[:, None] == hid[None, :]).astype(F32)
    tok = lambda j: pl.BlockSpec((1, tc, w), lambda bi, ti: (bi, ti, j))
    tab = pl.BlockSpec((tc, w), lambda bi, ti: (ti, 0))
    const = lambda shape: pl.BlockSpec(shape, lambda bi, ti: (0,) * len(shape))
    state = pl.BlockSpec((1, w, w), lambda bi, ti: (bi, 0, 0))
    return pl.pallas_call(
        functools.partial(_ret_kernel, tc=tc, nh=nh, hd=hd, k_scale=hd ** -0.5),
        grid=(b, t // tc),
        in_specs=[tok(0), tok(1), tok(2), tab, tab, const((nh, L, L)), const((L, w)), const((L, w)),
                  const((1, w)), const((w, w)), state],
        out_specs=[pl.BlockSpec((1, tc, w), lambda bi, ti: (bi, ti, 0)), state],
        out_shape=[jax.ShapeDtypeStruct((b, t, w), F32), jax.ShapeDtypeStruct((b, w, w), F32)],
        scratch_shapes=[pltpu.VMEM((w, w), F32)],
        compiler_params=_cparams("parallel", "arbitrary"),
        name="retention",
    )(ret, ret, ret, cos, sin, dec, qd, kd, gl, bd, s0)


def _block_diag_state(st, nh, hd):
    b = st.shape[0]
    eye = jnp.eye(nh, dtype=st.dtype)
    return jnp.einsum('bhde,hg->bhdge', st, eye).reshape(b, nh * hd, nh * hd)


def _diag_blocks(s, nh, hd):
    b = s.shape[0]
    s5 = s.reshape(b, nh, hd, nh, hd)
    return jnp.stack([s5[:, h, :, h, :] for h in range(nh)], axis=1)


def _head_mean_matrix(w, hd):
    hid = np.arange(w) // hd
    return jnp.asarray((hid[:, None] == hid[None, :]).astype(np.float32) / hd, dtype=BF16)


def _headnorm(x, g_mat, gain):
    ms = _dot_split(x * x, g_mat)
    return x * lax.rsqrt(ms + EPS) * gain


def _merge_kernel(x_ref, sb_ref, ret_ref, gate_ref, qm_ref, mk_ref, mv_ref, gsb_ref, gret_ref, gmo_ref,
                  msb_ref, mrt_ref, wo_ref, x1_ref, *, nh_mem, hd):
    wsb = sb_ref.shape[2]
    wrt = ret_ref.shape[2]
    qm = qm_ref[0]
    mkb = mk_ref[0].astype(BF16)
    mvb = mv_ref[0].astype(BF16)
    head_q = lax.broadcasted_iota(I32, qm.shape, 1) // hd
    head_m = lax.broadcasted_iota(I32, mkb.shape, 1) // hd
    mem_o = jnp.zeros(qm.shape, F32)
    for h in range(nh_mem):
        s = _dot_nt(jnp.where(head_q == h, qm, jnp.zeros_like(qm)), mkb)
        e = jnp.exp(s - jnp.max(s, axis=-1, keepdims=True))
        p = e / jnp.sum(e, axis=-1, keepdims=True)
        mem_o = mem_o + _dot(p.astype(BF16), jnp.where(head_m == h, mvb, jnp.zeros_like(mvb)))
    sb_n = _headnorm(sb_ref[0], msb_ref[...], gsb_ref[...])
    gate = gate_ref[0]
    ret_n = _headnorm(ret_ref[0], mrt_ref[...], gret_ref[...]) * (gate / (1.0 + jnp.exp(-gate)))
    mem_n = _headnorm(mem_o, mrt_ref[...], gmo_ref[...])
    y = (_dot(sb_n.astype(BF16), wo_ref[0:wsb, :])
         + _dot(ret_n.astype(BF16), wo_ref[wsb:wsb + wrt, :])
         + _dot(mem_n.astype(BF16), wo_ref[wsb + wrt:, :]))
    x1_ref[0] = x_ref[0] + y


def _merge(x, sb_o, ret_o, ret, qm, mk, mv, g_sb, g_ret, g_mo, wo_bf16, hd, tm):
    b, t, d = x.shape
    wsb = sb_o.shape[2]
    wrt = ret_o.shape[2]
    nmem = mk.shape[1]
    tok = lambda wd, j=0: pl.BlockSpec((1, tm, wd), lambda bi, ti: (bi, ti, j))
    const = lambda shape: pl.BlockSpec(shape, lambda bi, ti: (0,) * len(shape))
    memspec = pl.BlockSpec((1, nmem, wrt), lambda bi, ti: (bi, 0, 0))
    return pl.pallas_call(
        functools.partial(_merge_kernel, nh_mem=wrt // hd, hd=hd),
        grid=(b, t // tm),
        in_specs=[tok(d), tok(wsb), tok(wrt), tok(wrt, 3), tok(wrt), memspec, memspec,
                  const((1, wsb)), const((1, wrt)), const((1, wrt)),
                  const((wsb, wsb)), const((wrt, wrt)), const((d, d))],
        out_specs=tok(d),
        out_shape=jax.ShapeDtypeStruct((b, t, d), F32),
        compiler_params=_cparams("parallel", "parallel"),
        name="merge",
    )(x, sb_o, ret_o, ret, qm, mk, mv, g_sb.reshape(1, wsb), g_ret.reshape(1, wrt), g_mo.reshape(1, wrt),
      _head_mean_matrix(wsb, hd), _head_mean_matrix(wrt, hd), wo_bf16)


def _topk_rows(s, k, rid=None):
    if rid is None:
        rid = lax.broadcasted_iota(I32, s.shape, 0)
    big = jnp.int32(2 ** 30)
    vals, idxs = [], []
    for _ in range(k):
        m = jnp.max(s, axis=0, keepdims=True)
        i = jnp.min(jnp.where(s == m, rid, big), axis=0, keepdims=True)
        vals.append(m)
        idxs.append(i)
        s = jnp.where(rid == i, -jnp.inf, s)
    return jnp.concatenate(vals, axis=0), jnp.concatenate(idxs, axis=0)


def _staircase_blocks(k, rows=8):
    blocks = []
    singles = [a for a in range(k) if k // (a + 1) == 1]
    for a in range(k):
        nb = k // (a + 1)
        if nb > 1:
            blocks.extend(('row', a, b0) for b0 in range(0, nb, rows))
    assert singles and singles[0] % rows == 0 and len(singles) % rows == 0 and singles[-1] == k - 1
    blocks.extend(('col', a0, 0) for a0 in range(singles[0], k, rows))
    flat = []
    for kind, a, b in blocks:
        flat.extend([a * k + b + i for i in range(rows)] if kind == 'row' else [(a + i) * k + b for i in range(rows)])
    need = {(a, b) for a in range(k) for b in range(k) if (a + 1) * (b + 1) <= k}
    assert need <= {(f // k, f % k) for f in flat} and len(set(flat)) == len(flat)
    return blocks, np.array(flat, np.int32)[:, None]


def _take_rows(tbl, idx):
    out = jnp.zeros(idx.shape, tbl.dtype)
    for a in range(tbl.shape[0]):
        out = out + jnp.where(idx == a, tbl[a:a + 1, :], 0)
    return out


def _peer_score_kernel(x_ref, g_ref, wq_ref, sk_ref, cf_ref, h_ref, eid_ref, gw_ref, hb_scr,
                       eid_scr, gw_scr, sc_scr, *, topk, nkeys, blocks):
    head = pl.program_id(1)

    @pl.when(head == 0)
    def _():
        x = x_ref[...]
        ms = jnp.mean(x * x, axis=-1, keepdims=True)
        hn = x * lax.rsqrt(ms + EPS) * g_ref[...]
        for r in range(ROW_TILES):
            h_ref[pl.ds(r, hn.shape[0], stride=ROW_TILES), :] = hn[:, r * LANES:(r + 1) * LANES]
        hb_scr[...] = hn.astype(BF16)

    qb = _dot(hb_scr[...], wq_ref[...]).astype(BF16)
    hq = qb.shape[1] // 2
    sc_scr[0] = _dot_nt(sk_ref[0], qb[:, :hq])
    sc_scr[1] = _dot_nt(sk_ref[1], qb[:, hq:])
    brows = cf_ref.shape[0] // len(blocks)
    out_rows = pl.ds(pl.multiple_of(head * topk, topk), topk)
    width = min(LANES, qb.shape[0])

    def column_group(c):
        cols = slice(c * width, (c + 1) * width)
        sv0, si0 = _topk_rows(sc_scr[0, :, cols], topk)
        sv1, si1 = _topk_rows(sc_scr[1, :, cols], topk)
        cand = jnp.concatenate(
            [sv0[a:a + 1, :] + sv1[b:b + brows, :] if kind == 'row' else sv0[a:a + brows, :] + sv1[b:b + 1, :]
             for kind, a, b in blocks], axis=0)
        cv, ci = _topk_rows(cand, topk, cf_ref[:, cols])
        e1 = _take_rows(si0, ci // topk)
        e2 = _take_rows(si1, ci % topk)
        eid_scr[out_rows, cols] = ((e1 * nkeys + e2) * HALF_ROWS).astype(F32)
        e = jnp.exp(cv - cv[0:1, :])
        gw_scr[out_rows, cols] = e / jnp.sum(e, axis=0, keepdims=True)

    for c in range(qb.shape[0] // width):
        column_group(c)

    @pl.when(head == pl.num_programs(1) - 1)
    def _():
        eid_ref[...] = eid_scr[...].T.astype(I32)
        gw_ref[...] = gw_scr[...].T


def _peer_score(x, g, wq_bf16, sk_bf16, tm):
    n, d = x.shape
    nkeys, khalf = sk_bf16.shape[1], sk_bf16.shape[2]
    nheads = wq_bf16.shape[1] // (2 * khalf)
    k = PEER_TOPK
    blocks, flat = _staircase_blocks(k)
    cf = jnp.asarray(np.broadcast_to(flat, (flat.shape[0], tm)))
    xs = pl.BlockSpec((tm, d), lambda i, h: (i, 0))
    cs = pl.BlockSpec(cf.shape, lambda i, h: (0, 0))
    return pl.pallas_call(
        functools.partial(_peer_score_kernel, topk=k, nkeys=nkeys, blocks=tuple(blocks)),
        grid=(n // tm, nheads),
        in_specs=[xs, pl.BlockSpec((1, d), lambda i, h: (0, 0)),
                  pl.BlockSpec((d, 2 * khalf), lambda i, h: (0, h)),
                  pl.BlockSpec((2, nkeys, khalf), lambda i, h: (0, 0, 0)), cs],
        out_specs=[pl.BlockSpec((tm * ROW_TILES, LANES), lambda i, h: (i, 0)),
                   pl.BlockSpec((tm, nheads * k), lambda i, h: (i, 0)),
                   pl.BlockSpec((tm, nheads * k), lambda i, h: (i, 0))],
        out_shape=[jax.ShapeDtypeStruct((n * ROW_TILES, LANES), F32),
                   jax.ShapeDtypeStruct((n, nheads * k), I32), jax.ShapeDtypeStruct((n, nheads * k), F32)],
        scratch_shapes=[pltpu.VMEM((tm, d), BF16), pltpu.VMEM((nheads * k, tm), F32),
                        pltpu.VMEM((nheads * k, tm), F32), pltpu.VMEM((2, nkeys, tm), F32)],
        compiler_params=_cparams("parallel", "arbitrary"),
        name="peer_score",
    )(x, g.reshape(1, d), wq_bf16, sk_bf16, cf)


def _pack_table(tab):
    e, d = tab.shape
    assert d == ROW_TILES * LANES
    te = _row_tile(e, 512)
    return pl.pallas_call(
        _pack_kernel,
        grid=(e // te,),
        in_specs=[pl.BlockSpec((te, d), lambda i: (i, 0))],
        out_specs=pl.BlockSpec((te * HALF_ROWS, LANES), lambda i: (i, 0)),
        out_shape=jax.ShapeDtypeStruct((e * HALF_ROWS, LANES), I32),
        compiler_params=_cparams("parallel"),
        name="pack_table",
    )(tab)


def _pack_words(even, odd):
    odd = lax.bitcast_convert_type(odd, I32)
    low = lax.shift_right_logical(lax.bitcast_convert_type(even.astype(BF16).astype(F32), I32), 16)
    sign = odd & jnp.int32(-2 ** 31)
    mag = (odd & jnp.int32(2 ** 31 - 1)) + jnp.int32(0x8000)
    high = jnp.where(mag >= low, (mag - low) >> 16, 0)
    return sign | (high << 16) | low


def _pack_kernel(t_ref, o_ref):
    te = t_ref.shape[0]
    x = t_ref[...]
    for s in range(HALF_ROWS):
        o_ref[pl.ds(s, te, stride=HALF_ROWS), :] = _pack_words(x[:, (2 * s) * LANES:(2 * s + 1) * LANES],
                                                               x[:, (2 * s + 1) * LANES:(2 * s + 2) * LANES])


def _unpack(wd):
    even = lax.bitcast_convert_type(wd << 16, F32)
    odd = lax.bitcast_convert_type(wd, F32)
    return even, odd


def _erf(x):
    return lax.erf(x)


def _token_tiles(ref, t):
    base = pl.multiple_of(t * ROW_TILES, ROW_TILES)
    return ref[pl.ds(base, HALF_ROWS, stride=2), :], ref[pl.ds(base + 1, HALF_ROWS, stride=2), :]


PEER_U_SET = 4


def _peer_u_kernel(eid_ref, h_ref, gw_ref, tab_ref, w_ref, pa, pb, q_scr, act_scr, *, tg, npairs):
    ones = jnp.ones((LANES, LANES), BF16)
    row = lax.broadcasted_iota(I32, (npairs, LANES), 0)
    eye = (row == lax.broadcasted_iota(I32, (npairs, LANES), 1)).astype(F32)
    ns = PEER_U_SET

    def products(t, p_scr, qset, k):
        he, ho = _token_tiles(h_ref, t)
        he2 = jnp.concatenate([he, he], axis=0)
        ho2 = jnp.concatenate([ho, ho], axis=0)
        half = npairs // 2
        for j in range(half):
            wa = tab_ref[pl.ds(pl.multiple_of(eid_ref[t, j], HALF_ROWS), HALF_ROWS), :]
            wb = tab_ref[pl.ds(pl.multiple_of(eid_ref[t, j + half], HALF_ROWS), HALF_ROWS), :]
            even, odd = _unpack(jnp.concatenate([wa, wb], axis=0))
            x = even * he2 + odd * ho2
            x = x + pltpu.roll(x, 1, 0)
            p_scr[j * ROW_TILES:(j + 1) * ROW_TILES, :] = x + pltpu.roll(x, 2, 0)
        q = jnp.concatenate([p_scr[pl.ds(HALF_ROWS - 1, half, stride=ROW_TILES), :],
                             p_scr[pl.ds(ROW_TILES - 1, half, stride=ROW_TILES), :]], axis=0)
        hi, lo = _split_bf16(q)
        q_scr[qset, 2 * k * npairs:(2 * k + 1) * npairs, :] = hi
        q_scr[qset, (2 * k + 1) * npairs:(2 * k + 2) * npairs, :] = lo

    def lane_sums(qset, tokens):
        r = _dot(q_scr[qset], ones)
        for k, t in enumerate(tokens):
            rk = r[2 * k * npairs:(2 * k + 1) * npairs, :] + r[(2 * k + 1) * npairs:(2 * k + 2) * npairs, :]
            act_scr[pl.ds(t, 1), :] = jnp.sum(rk * eye, axis=0, keepdims=True)

    q_scr[1] = jnp.zeros(q_scr.shape[1:], BF16)

    def body(i, carry):
        t0 = 2 * ns * i
        lane_sums(1, [jnp.maximum(t0 - ns + k, 0) for k in range(ns)])
        for k in range(ns):
            products(t0 + k, (pa, pb)[k % 2], 0, k)
        lane_sums(0, [t0 + k for k in range(ns)])
        for k in range(ns):
            products(t0 + ns + k, (pa, pb)[k % 2], 1, k)
        return carry

    lax.fori_loop(0, tg // (2 * ns), body, 0)
    lane_sums(1, [tg - ns + k for k in range(ns)])
    act = act_scr[...]
    gelu = 0.5 * act * (1.0 + _erf(act * (2.0 ** -0.5)))
    w = gw_ref[...] * gelu
    w_ref[...] = _pack_words(w, pltpu.roll(w, npairs - 1, 1))


def _peer_u(eid, h, gw, tab_packed, tg):
    n, npairs = eid.shape
    assert npairs == LANES and h.shape == (n * ROW_TILES, LANES) and tg % (2 * PEER_U_SET) == 0
    tspec = pl.BlockSpec((tg, npairs), lambda i: (i, 0))
    pshape = pltpu.VMEM((npairs * HALF_ROWS, LANES), F32)
    qshape = pltpu.VMEM((2, 2 * PEER_U_SET * npairs, LANES), BF16)
    return pl.pallas_call(
        functools.partial(_peer_u_kernel, tg=tg, npairs=npairs),
        grid=(n // tg,),
        in_specs=[pl.BlockSpec((tg, npairs), lambda i: (i, 0), memory_space=pltpu.SMEM),
                  pl.BlockSpec((tg * ROW_TILES, LANES), lambda i: (i, 0)), tspec,
                  pl.BlockSpec(memory_space=pltpu.VMEM)],
        out_specs=tspec,
        out_shape=jax.ShapeDtypeStruct((n, npairs), I32),
        scratch_shapes=[pshape, pshape, qshape, pltpu.VMEM((tg, npairs), F32)],
        compiler_params=_cparams("arbitrary"),
        name="peer_u",
    )(eid, h, gw, tab_packed)


PEER_V_ACCUMULATORS = 2


def _peer_v_kernel(eid_ref, w_ref, tab_ref, o_ref, *, tg, npairs):
    nacc = PEER_V_ACCUMULATORS

    def token(t):
        acc_e = [jnp.zeros((HALF_ROWS, LANES), F32) for _ in range(nacc)]
        acc_o = [jnp.zeros((HALF_ROWS, LANES), F32) for _ in range(nacc)]
        for j in range(npairs):
            if j % 2 == 0:
                wpair = _unpack(jnp.full((HALF_ROWS, LANES), w_ref[t, j], I32))
            wt = wpair[j % 2]
            e4 = pl.multiple_of(eid_ref[t, j], HALF_ROWS)
            even, odd = _unpack(tab_ref[pl.ds(e4, HALF_ROWS), :])
            acc_e[j % nacc] = acc_e[j % nacc] + wt * even
            acc_o[j % nacc] = acc_o[j % nacc] + wt * odd
        base = pl.multiple_of(t * ROW_TILES, ROW_TILES)
        o_ref[pl.ds(base, HALF_ROWS, stride=2), :] = functools.reduce(lambda a, b: a + b, acc_e)
        o_ref[pl.ds(base + 1, HALF_ROWS, stride=2), :] = functools.reduce(lambda a, b: a + b, acc_o)

    def body(i, carry):
        token(2 * i)
        token(2 * i + 1)
        return carry

    lax.fori_loop(0, tg // 2, body, 0)


def _peer_v(eid, w, tab_packed, tg):
    n, npairs = eid.shape
    assert tg % 2 == 0
    sspec = pl.BlockSpec((tg, npairs), lambda i: (i, 0), memory_space=pltpu.SMEM)
    return pl.pallas_call(
        functools.partial(_peer_v_kernel, tg=tg, npairs=npairs),
        grid=(n // tg,),
        in_specs=[sspec, sspec, pl.BlockSpec(memory_space=pltpu.VMEM)],
        out_specs=pl.BlockSpec((tg * ROW_TILES, LANES), lambda i: (i, 0)),
        out_shape=jax.ShapeDtypeStruct((n * ROW_TILES, LANES), F32),
        compiler_params=_cparams("arbitrary"),
        name="peer_v",
    )(eid, w, tab_packed)


def _final_kernel(x_ref, p_ref, g_ref, y_ref):
    tm = x_ref.shape[0]
    p = jnp.concatenate([p_ref[pl.ds(r, tm, stride=ROW_TILES), :] for r in range(ROW_TILES)], axis=1)
    x = x_ref[...] + p
    ms = jnp.mean(x * x, axis=-1, keepdims=True)
    y_ref[...] = x * lax.rsqrt(ms + EPS) * g_ref[...]


def _final(x, p, g, tm):
    n, d = x.shape
    xs = pl.BlockSpec((tm, d), lambda i: (i, 0))
    return pl.pallas_call(
        _final_kernel,
        grid=(n // tm,),
        in_specs=[xs, pl.BlockSpec((tm * ROW_TILES, LANES), lambda i: (i, 0)), pl.BlockSpec((1, d), lambda i: (0, 0))],
        out_specs=xs,
        out_shape=jax.ShapeDtypeStruct((n, d), F32),
        compiler_params=_cparams("parallel"),
        name="final_norm",
    )(x, p, g.reshape(1, d))


def _row_tile(n, pref):
    t = min(pref, n)
    assert n % t == 0, (n, t)
    return t


def _stream(x, w_in_bf, g_mix, dims):
    b, t, d = x.shape
    wsb, wrt, wmm, hd = dims
    scale = hd ** -0.5
    o = 3 * wsb
    outs = [(0, wsb, scale, BF16, 0),
            (wsb, 2 * wsb, 1.0, F32, wsb // hd), (2 * wsb, o, 1.0, F32, wsb // hd),
            (wsb, 2 * wsb, 1.0, BF16, 0), (2 * wsb, o, 1.0, BF16, 0),
            (o, o + 4 * wrt, 1.0, F32, 0),
            (o + 4 * wrt, o + 4 * wrt + wmm, scale, BF16, 0)]
    res = _norm_proj(x.reshape(b * t, d), g_mix, w_in_bf, outs, _row_tile(b * t, 512))
    return [r.reshape(b, t, r.shape[1]) if r.shape[0] == b * t else r for r in res]


def _peer_and_final(x1, g_ffn, wq_bf, sk_bf, u_packed, v_packed, g_final):
    n = x1.shape[0]
    h, eid, gw = _peer_score(x1, g_ffn, wq_bf, sk_bf, _row_tile(n, 512))
    tg = _row_tile(n, 128)
    w = _peer_u(eid, h, gw, u_packed, tg)
    p = _peer_v(eid, w, v_packed, tg)
    return _final(x1, p, g_final, _row_tile(n, 512))


def kernel(x_prompt, x_sample, mem_prompt, cache_sb_k, cache_sb_v, state_ret, cache_mem_k, cache_mem_v,
           g_mix, w_in, w_out, g_sb_out, g_ret_out, g_mem_out, g_mem, w_mem_kv,
           g_ffn, w_peer_q, peer_sub_keys, peer_u, peer_v, g_final):
    depth = w_in.shape[0]
    assert depth == 1, "single-layer step"
    bp, tp, d = x_prompt.shape
    bs, ts, _ = x_sample.shape
    _, _, past, sb_heads, hd = cache_sb_k.shape
    ret_heads = state_ret.shape[2]
    mem_heads = cache_mem_k.shape[3]
    nmem = mem_prompt.shape[1]
    wsb, wrt, wmm = sb_heads * hd, ret_heads * hd, mem_heads * hd
    assert hd == 64 and wrt == wmm and d == 2 * HALF_ROWS * LANES
    l = 0
    w_in_bf = w_in[l].astype(BF16)
    wo_bf = w_out[l].astype(BF16)
    dims = (wsb, wrt, wmm, hd)

    q_p, k_p, v_p, kb_p, vb_p, ret_p, qm_p = _stream(x_prompt, w_in_bf, g_mix[l], dims)
    sb_p = _sb_prompt(q_p, kb_p, vb_p, hd)
    zeros_state = jnp.zeros((bp, wrt, wrt), F32)
    ro_p, st_p = _retention(ret_p, jnp.arange(tp, dtype=I32), zeros_state, ret_heads, hd, _row_tile(tp, 512))
    mk_p, mv_p = _norm_proj(mem_prompt.reshape(bp * nmem, d), g_mem[l], w_mem_kv[l].astype(BF16),
                            [(0, wmm, 1.0, F32, 0), (wmm, 2 * wmm, 1.0, F32, 0)], _row_tile(bp * nmem, 512))
    mk_p = mk_p.reshape(bp, nmem, wmm)
    mv_p = mv_p.reshape(bp, nmem, wmm)
    x1_p = _merge(x_prompt, sb_p, ro_p, ret_p, qm_p, mk_p, mv_p, g_sb_out[l], g_ret_out[l], g_mem_out[l],
                  wo_bf, hd, _row_tile(tp, 256))

    q_s, k_s, v_s, kb_s, vb_s, ret_s, qm_s = _stream(x_sample, w_in_bf, g_mix[l], dims)
    sb_s = _sb_sample(q_s, kb_s, vb_s, cache_sb_k[l].reshape(bs, past, wsb), cache_sb_v[l].reshape(bs, past, wsb), hd)
    s0 = _block_diag_state(state_ret[l].astype(F32), ret_heads, hd)
    ro_s, st_s = _retention(ret_s, past + jnp.arange(ts, dtype=I32), s0, ret_heads, hd, _row_tile(ts, 512))
    x1_s = _merge(x_sample, sb_s, ro_s, ret_s, qm_s, cache_mem_k[l].reshape(bs, nmem, wmm),
                  cache_mem_v[l].reshape(bs, nmem, wmm), g_sb_out[l], g_ret_out[l], g_mem_out[l],
                  wo_bf, hd, _row_tile(ts, 256))

    wq_bf = w_peer_q[l].astype(BF16)
    sk_bf = peer_sub_keys[l].astype(BF16)
    u_packed = _pack_table(peer_u[l])
    v_packed = _pack_table(peer_v[l])
    y_p = _peer_and_final(x1_p.reshape(bp * tp, d), g_ffn[l], wq_bf, sk_bf, u_packed, v_packed, g_final)
    y_s = _peer_and_final(x1_s.reshape(bs * ts, d), g_ffn[l], wq_bf, sk_bf, u_packed, v_packed, g_final)

    return (y_p.reshape(bp, tp, d), y_s.reshape(bs, ts, d),
            k_p.reshape(1, bp, tp, sb_heads, hd), v_p.reshape(1, bp, tp, sb_heads, hd),
            _diag_blocks(st_p, ret_heads, hd)[None],
            mk_p.reshape(1, bp, nmem, mem_heads, hd), mv_p.reshape(1, bp, nmem, mem_heads, hd),
            k_s.reshape(1, bs, ts, sb_heads, hd), v_s.reshape(1, bs, ts, sb_heads, hd),
            _diag_blocks(st_s, ret_heads, hd)[None].astype(state_ret.dtype))
```

```python
import functools

import numpy as np
import jax
import jax.numpy as jnp
from jax import lax
from jax.experimental import pallas as pl
from jax.experimental.pallas import tpu as pltpu

F32 = jnp.float32
BF16 = jnp.bfloat16
I32 = jnp.int32

EPS = 1e-6
ROPE_BASE = 10000.0
PEER_TOPK = 16
RET_CHUNK = 64
LANES = 128
ROW_TILES = 8
HALF_ROWS = 4
VMEM_LIMIT = 56 * 1024 * 1024
SB_DEAD_LOG = -105.0


def _cparams(*sem):
    return pltpu.CompilerParams(dimension_semantics=sem, vmem_limit_bytes=VMEM_LIMIT)


def _dot(a, b):
    return jnp.dot(a, b, preferred_element_type=F32)


def _dot_nt(a, b):
    return lax.dot_general(a, b, (((1,), (1,)), ((), ())), preferred_element_type=F32)


def _split_bf16(x):
    hi = x.astype(BF16)
    lo = (x - hi.astype(F32)).astype(BF16)
    return hi, lo


def _dot_split(x, m_bf16):
    hi, lo = _split_bf16(x)
    return _dot(hi, m_bf16) + _dot(lo, m_bf16)


def _norm_proj_kernel(x_ref, g_ref, w_ref, *out_refs, cols):
    x = x_ref[...]
    ms = jnp.mean(x * x, axis=-1, keepdims=True)
    h = x * lax.rsqrt(ms + EPS) * g_ref[...]
    p = _dot(h.astype(BF16), w_ref[...])
    tm = x.shape[0]
    for ref, (lo, hi, scale, heads) in zip(out_refs, cols):
        v = p[:, lo:hi]
        if scale != 1.0:
            v = v * scale
        if heads:
            hd = (hi - lo) // heads
            for h in range(heads):
                ref[pl.ds(h, tm, stride=heads), :] = v[:, h * hd:(h + 1) * hd].astype(ref.dtype)
        else:
            ref[...] = v.astype(ref.dtype)


def _norm_proj(x, g, w_bf16, outs, tm):
    n, d = x.shape
    wcols = w_bf16.shape[1]
    shape = lambda rows, lo, hi, hds: (rows * hds, (hi - lo) // hds) if hds else (rows, hi - lo)
    out_shape = [jax.ShapeDtypeStruct(shape(n, lo, hi, hds), dt) for lo, hi, _, dt, hds in outs]
    out_specs = [pl.BlockSpec(shape(tm, lo, hi, hds), lambda i: (i, 0)) for lo, hi, _, _, hds in outs]
    return pl.pallas_call(
        functools.partial(_norm_proj_kernel, cols=tuple((lo, hi, sc, hds) for lo, hi, sc, _, hds in outs)),
        grid=(n // tm,),
        in_specs=[pl.BlockSpec((tm, d), lambda i: (i, 0)),
                  pl.BlockSpec((1, d), lambda i: (0, 0)),
                  pl.BlockSpec((d, wcols), lambda i: (0, 0))],
        out_specs=out_specs,
        out_shape=out_shape,
        compiler_params=_cparams("parallel"),
        name="norm_proj",
    )(x, g.reshape(1, d), w_bf16)


def _suffix_matrix(tk):
    j = np.arange(2 * tk)[:, None]
    s = np.arange(2 * tk)[None, :]
    return jnp.asarray(((j // tk == s // tk) & (j >= s)).astype(np.float32), dtype=BF16)


def _sb_step(q2, kb, vb, u, carry, tk, hd, valid):
    c0, c1, acc = carry
    tq = q2.shape[0]
    lane = lax.broadcasted_iota(I32, kb.shape, 1)
    first = lane < hd
    zero = jnp.zeros_like(kb)
    kk = jnp.concatenate([jnp.where(first, kb, zero), jnp.where(first, zero, kb)], axis=0)
    vv = jnp.concatenate([jnp.where(first, vb, zero), jnp.where(first, zero, vb)], axis=0)
    z = _dot_nt(q2, kk)
    sp = jnp.log(1.0 + jnp.exp(-jnp.abs(z)))
    lk = jnp.minimum(-z, 0.0) - sp
    if valid is not None:
        lk = jnp.where(valid, lk, 0.0)
    m = _dot_split(lk, u)
    c = jnp.concatenate([jnp.broadcast_to(c0, (tq, tk)), jnp.broadcast_to(c1, (tq, tk))], axis=1)
    a = jnp.exp(z + c + m)
    if valid is not None:
        a = jnp.where(valid, a, 0.0)
    acc = acc + _dot(a.astype(BF16), vv)
    c0 = c0 + jnp.sum(lk[:, :tk], axis=-1, keepdims=True)
    c1 = c1 + jnp.sum(lk[:, tk:], axis=-1, keepdims=True)
    return c0, c1, acc


def _sb_alive(carries):
    m = functools.reduce(jnp.maximum, [jnp.maximum(c0, c1) for c0, c1, _ in carries])
    return jnp.max(m) > SB_DEAD_LOG


def _sb_far_blocks(qs, block, nblk, u, carries, tk, hd, first=0):
    n = len(qs)
    flatten = lambda cs: tuple(x for c in cs for x in c)
    unflatten = lambda flat: [tuple(flat[3 * p:3 * p + 3]) for p in range(n)]

    def cond(state):
        return jnp.logical_and(state[0] < nblk, state[1])

    def body(state):
        i = state[0]
        kvs = block(i)
        cs = [_sb_step(qs[p], kvs[p][0], kvs[p][1], u, c, tk, hd, None) for p, c in enumerate(unflatten(state[2:]))]
        return (i + 1, _sb_alive(cs)) + flatten(cs)

    state = lax.while_loop(cond, body, (jnp.int32(first), _sb_alive(carries)) + flatten(carries))
    return unflatten(state[2:])


SB_PAIRS_PER_STEP = 4


def _sb_prompt_kernel(q_ref, k_ref, v_ref, u_ref, o_ref, *, tq, tk, hd, npairs):
    q0 = pl.program_id(2) * tq
    lanes = [slice(p * LANES, (p + 1) * LANES) for p in range(npairs)]
    qs = [q_ref[0, :, ln] for ln in lanes]
    u = u_ref[...]
    row = lax.broadcasted_iota(I32, (tq, 2 * tk), 0)
    koff = lax.broadcasted_iota(I32, (tq, 2 * tk), 1) & (tk - 1)
    carries = [(jnp.zeros((tq, 1), F32), jnp.zeros((tq, 1), F32), jnp.zeros((tq, LANES), F32)) for _ in lanes]
    nd = tq // tk
    for d in range(nd + 1):
        off = (nd - 1 - d) * tk
        limit = row if d < nd else row + jnp.where(q0 >= tk, 0, -4 * tq)
        valid = (koff + off) < limit
        ks = pl.multiple_of(jnp.maximum(q0 + off, 0), tk)
        carries = [_sb_step(qs[p], k_ref[0, pl.ds(ks, tk), ln], v_ref[0, pl.ds(ks, tk), ln], u, carries[p], tk, hd, valid)
                   for p, ln in enumerate(lanes)]

    def block(i):
        ks = pl.multiple_of(q0 - (i + 1) * tk, tk)
        return [(k_ref[0, pl.ds(ks, tk), ln], v_ref[0, pl.ds(ks, tk), ln]) for ln in lanes]

    carries = _sb_far_blocks(qs, block, q0 // tk, u, carries, tk, hd, first=1)
    for p, ln in enumerate(lanes):
        o_ref[0, :, ln] = carries[p][2]


def _sb_prompt(q, k, v, hd, tq=256, tk=128):
    b, t, w = q.shape
    npairs = SB_PAIRS_PER_STEP
    wb = npairs * LANES
    assert w % wb == 0
    return pl.pallas_call(
        functools.partial(_sb_prompt_kernel, tq=tq, tk=tk, hd=hd, npairs=npairs),
        grid=(b, w // wb, t // tq),
        in_specs=[pl.BlockSpec((1, tq, wb), lambda bi, hp, qi: (bi, qi, hp)),
                  pl.BlockSpec((1, t, wb), lambda bi, hp, qi: (bi, 0, hp)),
                  pl.BlockSpec((1, t, wb), lambda bi, hp, qi: (bi, 0, hp)),
                  pl.BlockSpec((2 * tk, 2 * tk), lambda bi, hp, qi: (0, 0))],
        out_specs=pl.BlockSpec((1, tq, wb), lambda bi, hp, qi: (bi, qi, hp)),
        out_shape=jax.ShapeDtypeStruct((b, t, w), F32),
        compiler_params=_cparams("parallel", "parallel", "arbitrary"),
        name="sb_prompt",
    )(q, k, v, _suffix_matrix(tk))


def _sb_sample_kernel(q_ref, kn_ref, vn_ref, kc_ref, vc_ref, un_ref, u_ref, o_ref, *, ts, tk, hd, past):
    q2 = q_ref[0]
    row = lax.broadcasted_iota(I32, (ts, 2 * ts), 0)
    koff = lax.broadcasted_iota(I32, (ts, 2 * ts), 1) & (ts - 1)
    carry = (jnp.zeros((ts, 1), F32), jnp.zeros((ts, 1), F32), jnp.zeros((ts, LANES), F32))
    carry = _sb_step(q2, kn_ref[0], vn_ref[0], un_ref[...], carry, ts, hd, koff < row)
    u = u_ref[...]

    def block(i):
        ks = pl.multiple_of(past - (i + 1) * tk, tk)
        return kc_ref[0, pl.ds(ks, tk), :].astype(BF16), vc_ref[0, pl.ds(ks, tk), :].astype(BF16)

    (carry,) = _sb_far_blocks([q2], lambda i: [block(i)], past // tk, u, [carry], tk, hd)
    o_ref[0] = carry[2]


def _sb_sample(q, kn, vn, kc, vc, hd, tk=128):
    b, ts, w = q.shape
    past = kc.shape[1]
    new_spec = pl.BlockSpec((1, ts, LANES), lambda bi, hp: (bi, 0, hp))
    cache_spec = pl.BlockSpec((1, past, LANES), lambda bi, hp: (bi, 0, hp))
    return pl.pallas_call(
        functools.partial(_sb_sample_kernel, ts=ts, tk=tk, hd=hd, past=past),
        grid=(b, w // LANES),
        in_specs=[new_spec, new_spec, new_spec, cache_spec, cache_spec,
                  pl.BlockSpec((2 * ts, 2 * ts), lambda bi, hp: (0, 0)),
                  pl.BlockSpec((2 * tk, 2 * tk), lambda bi, hp: (0, 0))],
        out_specs=new_spec,
        out_shape=jax.ShapeDtypeStruct((b, ts, w), F32),
        compiler_params=_cparams("parallel", "parallel"),
        name="sb_sample",
    )(q, kn, vn, kc, vc, _suffix_matrix(ts), _suffix_matrix(tk))


def _ret_kernel(q_ref, k_ref, v_ref, cos_ref, sin_ref, dec_ref, qd_ref, kd_ref, gl_ref, bd_ref, s0_ref,
                o_ref, st_ref, s_scr, *, tc, nh, hd, k_scale):
    t = pl.program_id(1)
    w = nh * hd
    L = RET_CHUNK

    @pl.when(t == 0)
    def _():
        s_scr[...] = s0_ref[0]

    lane = lax.broadcasted_iota(I32, (L, w), 1)
    first_half = (lane & (hd - 1)) < hd // 2
    head = lane // hd

    def rope(x, cos, sin):
        swapped = jnp.where(first_half, pltpu.roll(x, w - hd // 2, 1), pltpu.roll(x, hd // 2, 1))
        return x * cos + swapped * sin

    s = s_scr[...]
    for ci in range(tc // L):
        sl = slice(ci * L, (ci + 1) * L)
        cos = cos_ref[sl, :]
        sin = sin_ref[sl, :]
        q = rope(q_ref[0, sl, :], cos, sin)
        k = rope(k_ref[0, sl, :], cos, sin) * k_scale
        qb = q.astype(BF16)
        kb = k.astype(BF16)
        vb = v_ref[0, sl, :].astype(BF16)
        zero = jnp.zeros_like(qb)
        out = _dot(qb, s.astype(BF16)) * qd_ref[...]
        for h in range(nh):
            mh = head == h
            sc = _dot_nt(jnp.where(mh, qb, zero), kb)
            p = (sc * dec_ref[h]).astype(BF16)
            out = out + _dot(p, jnp.where(mh, vb, zero))
        o_ref[0, sl, :] = out
        kd = (k * kd_ref[...]).astype(BF16)
        outer = lax.dot_general(kd, vb, (((0,), (0,)), ((), ())), preferred_element_type=F32)
        s = gl_ref[...] * s + bd_ref[...] * outer
    s_scr[...] = s

    @pl.when(t == pl.num_programs(1) - 1)
    def _():
        st_ref[0] = s


def _retention(ret, pos, s0, nh, hd, tc):
    b, t, _ = ret.shape
    w = nh * hd
    L = RET_CHUNK
    half = hd // 2
    freqs = ROPE_BASE ** (-jnp.arange(half, dtype=F32) / half)
    ang = pos.astype(F32)[:, None] * freqs[None, :]
    cos = jnp.tile(jnp.cos(ang), (1, 2 * nh))
    sin = jnp.tile(jnp.concatenate([-jnp.sin(ang), jnp.sin(ang)], axis=1), (1, nh))
    lg = jnp.log1p(-jnp.exp2(-5.0 - jnp.arange(nh, dtype=F32)))
    idx = jnp.arange(L, dtype=F32)
    diff = idx[:, None] - idx[None, :]
    causal = diff >= 0
    dec = jnp.where(causal[None], jnp.exp(jnp.where(causal, diff, 0.0)[None] * lg[:, None, None]), 0.0)
    lg_lane = jnp.repeat(lg, hd)[None, :]
    qd = jnp.exp((idx + 1.0)[:, None] * lg_lane)
    kd = jnp.exp((L - 1.0 - idx)[:, None] * lg_lane)
    gl = jnp.exp(L * lg_lane)
    hid = jnp.arange(w) // hd
    bd = (hid[:, None] == hid[None, :]).astype(F32)
    tok = lambda j: pl.BlockSpec((1, tc, w), lambda bi, ti: (bi, ti, j))
    tab = pl.BlockSpec((tc, w), lambda bi, ti: (ti, 0))
    const = lambda shape: pl.BlockSpec(shape, lambda bi, ti: (0,) * len(shape))
    state = pl.BlockSpec((1, w, w), lambda bi, ti: (bi, 0, 0))
    return pl.pallas_call(
        functools.partial(_ret_kernel, tc=tc, nh=nh, hd=hd, k_scale=hd ** -0.5),
        grid=(b, t // tc),
        in_specs=[tok(0), tok(1), tok(2), tab, tab, const((nh, L, L)), const((L, w)), const((L, w)),
                  const((1, w)), const((w, w)), state],
        out_specs=[pl.BlockSpec((1, tc, w), lambda bi, ti: (bi, ti, 0)), state],
        out_shape=[jax.ShapeDtypeStruct((b, t, w), F32), jax.ShapeDtypeStruct((b, w, w), F32)],
        scratch_shapes=[pltpu.VMEM((w, w), F32)],
        compiler_params=_cparams("parallel", "arbitrary"),
        name="retention",
    )(ret, ret, ret, cos, sin, dec, qd, kd, gl, bd, s0)


def _block_diag_state(st, nh, hd):
    b = st.shape[0]
    eye = jnp.eye(nh, dtype=st.dtype)
    return jnp.einsum('bhde,hg->bhdge', st, eye).reshape(b, nh * hd, nh * hd)


def _diag_blocks(s, nh, hd):
    b = s.shape[0]
    s5 = s.reshape(b, nh, hd, nh, hd)
    return jnp.stack([s5[:, h, :, h, :] for h in range(nh)], axis=1)


def _head_mean_matrix(w, hd):
    hid = np.arange(w) // hd
    return jnp.asarray((hid[:, None] == hid[None, :]).astype(np.float32) / hd, dtype=BF16)


def _headnorm(x, g_mat, gain):
    ms = _dot_split(x * x, g_mat)
    return x * lax.rsqrt(ms + EPS) * gain


def _merge_kernel(x_ref, sb_ref, ret_ref, gate_ref, qm_ref, mk_ref, mv_ref, gsb_ref, gret_ref, gmo_ref,
                  msb_ref, mrt_ref, wo_ref, x1_ref, *, nh_mem, hd):
    wsb = sb_ref.shape[2]
    wrt = ret_ref.shape[2]
    qm = qm_ref[0]
    mkb = mk_ref[0].astype(BF16)
    mvb = mv_ref[0].astype(BF16)
    head_q = lax.broadcasted_iota(I32, qm.shape, 1) // hd
    head_m = lax.broadcasted_iota(I32, mkb.shape, 1) // hd
    mem_o = jnp.zeros(qm.shape, F32)
    for h in range(nh_mem):
        s = _dot_nt(jnp.where(head_q == h, qm, jnp.zeros_like(qm)), mkb)
        e = jnp.exp(s - jnp.max(s, axis=-1, keepdims=True))
        p = e / jnp.sum(e, axis=-1, keepdims=True)
        mem_o = mem_o + _dot(p.astype(BF16), jnp.where(head_m == h, mvb, jnp.zeros_like(mvb)))
    sb_n = _headnorm(sb_ref[0], msb_ref[...], gsb_ref[...])
    gate = gate_ref[0]
    ret_n = _headnorm(ret_ref[0], mrt_ref[...], gret_ref[...]) * (gate / (1.0 + jnp.exp(-gate)))
    mem_n = _headnorm(mem_o, mrt_ref[...], gmo_ref[...])
    y = (_dot(sb_n.astype(BF16), wo_ref[0:wsb, :])
         + _dot(ret_n.astype(BF16), wo_ref[wsb:wsb + wrt, :])
         + _dot(mem_n.astype(BF16), wo_ref[wsb + wrt:, :]))
    x1_ref[0] = x_ref[0] + y


def _merge(x, sb_o, ret_o, ret, qm, mk, mv, g_sb, g_ret, g_mo, wo_bf16, hd, tm):
    b, t, d = x.shape
    wsb = sb_o.shape[2]
    wrt = ret_o.shape[2]
    nmem = mk.shape[1]
    tok = lambda wd, j=0: pl.BlockSpec((1, tm, wd), lambda bi, ti: (bi, ti, j))
    const = lambda shape: pl.BlockSpec(shape, lambda bi, ti: (0,) * len(shape))
    memspec = pl.BlockSpec((1, nmem, wrt), lambda bi, ti: (bi, 0, 0))
    return pl.pallas_call(
        functools.partial(_merge_kernel, nh_mem=wrt // hd, hd=hd),
        grid=(b, t // tm),
        in_specs=[tok(d), tok(wsb), tok(wrt), tok(wrt, 3), tok(wrt), memspec, memspec,
                  const((1, wsb)), const((1, wrt)), const((1, wrt)),
                  const((wsb, wsb)), const((wrt, wrt)), const((d, d))],
        out_specs=tok(d),
        out_shape=jax.ShapeDtypeStruct((b, t, d), F32),
        compiler_params=_cparams("parallel", "parallel"),
        name="merge",
    )(x, sb_o, ret_o, ret, qm, mk, mv, g_sb.reshape(1, wsb), g_ret.reshape(1, wrt), g_mo.reshape(1, wrt),
      _head_mean_matrix(wsb, hd), _head_mean_matrix(wrt, hd), wo_bf16)


def _topk_rows(s, k, rid=None):
    if rid is None:
        rid = lax.broadcasted_iota(I32, s.shape, 0)
    big = jnp.int32(2 ** 30)
    vals, idxs = [], []
    for _ in range(k):
        m = jnp.max(s, axis=0, keepdims=True)
        i = jnp.min(jnp.where(s == m, rid, big), axis=0, keepdims=True)
        vals.append(m)
        idxs.append(i)
        s = jnp.where(rid == i, -jnp.inf, s)
    return jnp.concatenate(vals, axis=0), jnp.concatenate(idxs, axis=0)


def _staircase_blocks(k, rows=8):
    blocks = []
    singles = [a for a in range(k) if k // (a + 1) == 1]
    for a in range(k):
        nb = k // (a + 1)
        if nb > 1:
            blocks.extend(('row', a, b0) for b0 in range(0, nb, rows))
    assert singles and singles[0] % rows == 0 and len(singles) % rows == 0 and singles[-1] == k - 1
    blocks.extend(('col', a0, 0) for a0 in range(singles[0], k, rows))
    flat = []
    for kind, a, b in blocks:
        flat.extend([a * k + b + i for i in range(rows)] if kind == 'row' else [(a + i) * k + b for i in range(rows)])
    need = {(a, b) for a in range(k) for b in range(k) if (a + 1) * (b + 1) <= k}
    assert need <= {(f // k, f % k) for f in flat} and len(set(flat)) == len(flat)
    return blocks, np.array(flat, np.int32)[:, None]


def _take_rows(tbl, idx):
    out = jnp.zeros(idx.shape, tbl.dtype)
    for a in range(tbl.shape[0]):
        out = out + jnp.where(idx == a, tbl[a:a + 1, :], 0)
    return out


def _peer_score_kernel(x_ref, g_ref, wq_ref, sk_ref, cf_ref, h_ref, eid_ref, gw_ref, hb_scr,
                       eid_scr, gw_scr, sc_scr, *, topk, nkeys, blocks):
    head = pl.program_id(1)

    @pl.when(head == 0)
    def _():
        x = x_ref[...]
        ms = jnp.mean(x * x, axis=-1, keepdims=True)
        hn = x * lax.rsqrt(ms + EPS) * g_ref[...]
        for r in range(ROW_TILES):
            h_ref[pl.ds(r, hn.shape[0], stride=ROW_TILES), :] = hn[:, r * LANES:(r + 1) * LANES]
        hb_scr[...] = hn.astype(BF16)

    qb = _dot(hb_scr[...], wq_ref[...]).astype(BF16)
    hq = qb.shape[1] // 2
    sc_scr[0] = _dot_nt(sk_ref[0], qb[:, :hq])
    sc_scr[1] = _dot_nt(sk_ref[1], qb[:, hq:])
    brows = cf_ref.shape[0] // len(blocks)
    out_rows = pl.ds(pl.multiple_of(head * topk, topk), topk)
    width = min(LANES, qb.shape[0])

    def column_group(c):
        cols = slice(c * width, (c + 1) * width)
        sv0, si0 = _topk_rows(sc_scr[0, :, cols], topk)
        sv1, si1 = _topk_rows(sc_scr[1, :, cols], topk)
        cand = jnp.concatenate(
            [sv0[a:a + 1, :] + sv1[b:b + brows, :] if kind == 'row' else sv0[a:a + brows, :] + sv1[b:b + 1, :]
             for kind, a, b in blocks], axis=0)
        cv, ci = _topk_rows(cand, topk, cf_ref[:, cols])
        e1 = _take_rows(si0, ci // topk)
        e2 = _take_rows(si1, ci % topk)
        eid_scr[out_rows, cols] = ((e1 * nkeys + e2) * HALF_ROWS).astype(F32)
        e = jnp.exp(cv - cv[0:1, :])
        gw_scr[out_rows, cols] = e / jnp.sum(e, axis=0, keepdims=True)

    for c in range(qb.shape[0] // width):
        column_group(c)

    @pl.when(head == pl.num_programs(1) - 1)
    def _():
        eid_ref[...] = eid_scr[...].T.astype(I32)
        gw_ref[...] = gw_scr[...].T


def _peer_score(x, g, wq_bf16, sk_bf16, tm):
    n, d = x.shape
    nkeys, khalf = sk_bf16.shape[1], sk_bf16.shape[2]
    nheads = wq_bf16.shape[1] // (2 * khalf)
    k = PEER_TOPK
    blocks, flat = _staircase_blocks(k)
    cf = jnp.asarray(np.broadcast_to(flat, (flat.shape[0], tm)))
    xs = pl.BlockSpec((tm, d), lambda i, h: (i, 0))
    cs = pl.BlockSpec(cf.shape, lambda i, h: (0, 0))
    return pl.pallas_call(
        functools.partial(_peer_score_kernel, topk=k, nkeys=nkeys, blocks=tuple(blocks)),
        grid=(n // tm, nheads),
        in_specs=[xs, pl.BlockSpec((1, d), lambda i, h: (0, 0)),
                  pl.BlockSpec((d, 2 * khalf), lambda i, h: (0, h)),
                  pl.BlockSpec((2, nkeys, khalf), lambda i, h: (0, 0, 0)), cs],
        out_specs=[pl.BlockSpec((tm * ROW_TILES, LANES), lambda i, h: (i, 0)),
                   pl.BlockSpec((tm, nheads * k), lambda i, h: (i, 0)),
                   pl.BlockSpec((tm, nheads * k), lambda i, h: (i, 0))],
        out_shape=[jax.ShapeDtypeStruct((n * ROW_TILES, LANES), F32),
                   jax.ShapeDtypeStruct((n, nheads * k), I32), jax.ShapeDtypeStruct((n, nheads * k), F32)],
        scratch_shapes=[pltpu.VMEM((tm, d), BF16), pltpu.VMEM((nheads * k, tm), F32),
                        pltpu.VMEM((nheads * k, tm), F32), pltpu.VMEM((2, nkeys, tm), F32)],
        compiler_params=_cparams("parallel", "arbitrary"),
        name="peer_score",
    )(x, g.reshape(1, d), wq_bf16, sk_bf16, cf)


def _pack_table(tab):
    e, d = tab.shape
    assert d == ROW_TILES * LANES
    te = _row_tile(e, 512)
    return pl.pallas_call(
        _pack_kernel,
        grid=(e // te,),
        in_specs=[pl.BlockSpec((te, d), lambda i: (i, 0))],
        out_specs=pl.BlockSpec((te * HALF_ROWS, LANES), lambda i: (i, 0)),
        out_shape=jax.ShapeDtypeStruct((e * HALF_ROWS, LANES), I32),
        compiler_params=_cparams("parallel"),
        name="pack_table",
    )(tab)


def _pack_words(even, odd):
    odd = lax.bitcast_convert_type(odd, I32)
    low = lax.shift_right_logical(lax.bitcast_convert_type(even.astype(BF16).astype(F32), I32), 16)
    sign = odd & jnp.int32(-2 ** 31)
    mag = (odd & jnp.int32(2 ** 31 - 1)) + jnp.int32(0x8000)
    high = jnp.where(mag >= low, (mag - low) >> 16, 0)
    return sign | (high << 16) | low


def _pack_kernel(t_ref, o_ref):
    te = t_ref.shape[0]
    x = t_ref[...]
    for s in range(HALF_ROWS):
        o_ref[pl.ds(s, te, stride=HALF_ROWS), :] = _pack_words(x[:, (2 * s) * LANES:(2 * s + 1) * LANES],
                                                               x[:, (2 * s + 1) * LANES:(2 * s + 2) * LANES])


def _unpack(wd):
    even = lax.bitcast_convert_type(wd << 16, F32)
    odd = lax.bitcast_convert_type(wd, F32)
    return even, odd


def _erf(x):
    return lax.erf(x)


def _token_tiles(ref, t):
    base = pl.multiple_of(t * ROW_TILES, ROW_TILES)
    return ref[pl.ds(base, HALF_ROWS, stride=2), :], ref[pl.ds(base + 1, HALF_ROWS, stride=2), :]


PEER_U_SET = 4


def _peer_u_kernel(eid_ref, h_ref, gw_ref, tab_ref, w_ref, pa, pb, q_scr, act_scr, *, tg, npairs):
    ones = jnp.ones((LANES, LANES), BF16)
    row = lax.broadcasted_iota(I32, (npairs, LANES), 0)
    eye = (row == lax.broadcasted_iota(I32, (npairs, LANES), 1)).astype(F32)
    ns = PEER_U_SET

    def products(t, p_scr, qset, k):
        he, ho = _token_tiles(h_ref, t)
        he2 = jnp.concatenate([he, he], axis=0)
        ho2 = jnp.concatenate([ho, ho], axis=0)
        for j in range(0, npairs, 2):
            wa = tab_ref[pl.ds(pl.multiple_of(eid_ref[t, j], HALF_ROWS), HALF_ROWS), :]
            wb = tab_ref[pl.ds(pl.multiple_of(eid_ref[t, j + 1], HALF_ROWS), HALF_ROWS), :]
            even, odd = _unpack(jnp.concatenate([wa, wb], axis=0))
            p_scr[j * HALF_ROWS:(j + 2) * HALF_ROWS, :] = even * he2 + odd * ho2
        q = p_scr[pl.ds(0, npairs, stride=HALF_ROWS), :]
        for s in range(1, HALF_ROWS):
            q = q + p_scr[pl.ds(s, npairs, stride=HALF_ROWS), :]
        hi, lo = _split_bf16(q)
        q_scr[qset, 2 * k * npairs:(2 * k + 1) * npairs, :] = hi
        q_scr[qset, (2 * k + 1) * npairs:(2 * k + 2) * npairs, :] = lo

    def lane_sums(qset, tokens):
        r = _dot(q_scr[qset], ones)
        for k, t in enumerate(tokens):
            rk = r[2 * k * npairs:(2 * k + 1) * npairs, :] + r[(2 * k + 1) * npairs:(2 * k + 2) * npairs, :]
            act_scr[pl.ds(t, 1), :] = jnp.sum(rk * eye, axis=0, keepdims=True)

    q_scr[1] = jnp.zeros(q_scr.shape[1:], BF16)

    def body(i, carry):
        t0 = 2 * ns * i
        lane_sums(1, [jnp.maximum(t0 - ns + k, 0) for k in range(ns)])
        for k in range(ns):
            products(t0 + k, (pa, pb)[k % 2], 0, k)
        lane_sums(0, [t0 + k for k in range(ns)])
        for k in range(ns):
            products(t0 + ns + k, (pa, pb)[k % 2], 1, k)
        return carry

    lax.fori_loop(0, tg // (2 * ns), body, 0)
    lane_sums(1, [tg - ns + k for k in range(ns)])
    act = act_scr[...]
    gelu = 0.5 * act * (1.0 + _erf(act * (2.0 ** -0.5)))
    w = gw_ref[...] * gelu
    w_ref[...] = _pack_words(w, pltpu.roll(w, npairs - 1, 1))


def _peer_u(eid, h, gw, tab_packed, tg):
    n, npairs = eid.shape
    assert npairs == LANES and h.shape == (n * ROW_TILES, LANES) and tg % (2 * PEER_U_SET) == 0
    tspec = pl.BlockSpec((tg, npairs), lambda i: (i, 0))
    pshape = pltpu.VMEM((npairs * HALF_ROWS, LANES), F32)
    qshape = pltpu.VMEM((2, 2 * PEER_U_SET * npairs, LANES), BF16)
    return pl.pallas_call(
        functools.partial(_peer_u_kernel, tg=tg, npairs=npairs),
        grid=(n // tg,),
        in_specs=[pl.BlockSpec((tg, npairs), lambda i: (i, 0), memory_space=pltpu.SMEM),
                  pl.BlockSpec((tg * ROW_TILES, LANES), lambda i: (i, 0)), tspec,
                  pl.BlockSpec(memory_space=pltpu.VMEM)],
        out_specs=tspec,
        out_shape=jax.ShapeDtypeStruct((n, npairs), I32),
        scratch_shapes=[pshape, pshape, qshape, pltpu.VMEM((tg, npairs), F32)],
        compiler_params=_cparams("arbitrary"),
        name="peer_u",
    )(eid, h, gw, tab_packed)


PEER_V_ACCUMULATORS = 2


def _peer_v_kernel(eid_ref, w_ref, tab_ref, o_ref, *, tg, npairs):
    nacc = PEER_V_ACCUMULATORS

    def token(t):
        acc_e = [jnp.zeros((HALF_ROWS, LANES), F32) for _ in range(nacc)]
        acc_o = [jnp.zeros((HALF_ROWS, LANES), F32) for _ in range(nacc)]
        for j in range(npairs):
            if j % 2 == 0:
                wpair = _unpack(jnp.full((HALF_ROWS, LANES), w_ref[t, j], I32))
            wt = wpair[j % 2]
            e4 = pl.multiple_of(eid_ref[t, j], HALF_ROWS)
            even, odd = _unpack(tab_ref[pl.ds(e4, HALF_ROWS), :])
            acc_e[j % nacc] = acc_e[j % nacc] + wt * even
            acc_o[j % nacc] = acc_o[j % nacc] + wt * odd
        base = pl.multiple_of(t * ROW_TILES, ROW_TILES)
        o_ref[pl.ds(base, HALF_ROWS, stride=2), :] = functools.reduce(lambda a, b: a + b, acc_e)
        o_ref[pl.ds(base + 1, HALF_ROWS, stride=2), :] = functools.reduce(lambda a, b: a + b, acc_o)

    def body(i, carry):
        token(2 * i)
        token(2 * i + 1)
        return carry

    lax.fori_loop(0, tg // 2, body, 0)


def _peer_v(eid, w, tab_packed, tg):
    n, npairs = eid.shape
    assert tg % 2 == 0
    sspec = pl.BlockSpec((tg, npairs), lambda i: (i, 0), memory_space=pltpu.SMEM)
    return pl.pallas_call(
        functools.partial(_peer_v_kernel, tg=tg, npairs=npairs),
        grid=(n // tg,),
        in_specs=[sspec, sspec, pl.BlockSpec(memory_space=pltpu.VMEM)],
        out_specs=pl.BlockSpec((tg * ROW_TILES, LANES), lambda i: (i, 0)),
        out_shape=jax.ShapeDtypeStruct((n * ROW_TILES, LANES), F32),
        compiler_params=_cparams("arbitrary"),
        name="peer_v",
    )(eid, w, tab_packed)


def _final_kernel(x_ref, p_ref, g_ref, y_ref):
    tm = x_ref.shape[0]
    p = jnp.concatenate([p_ref[pl.ds(r, tm, stride=ROW_TILES), :] for r in range(ROW_TILES)], axis=1)
    x = x_ref[...] + p
    ms = jnp.mean(x * x, axis=-1, keepdims=True)
    y_ref[...] = x * lax.rsqrt(ms + EPS) * g_ref[...]


def _final(x, p, g, tm):
    n, d = x.shape
    xs = pl.BlockSpec((tm, d), lambda i: (i, 0))
    return pl.pallas_call(
        _final_kernel,
        grid=(n // tm,),
        in_specs=[xs, pl.BlockSpec((tm * ROW_TILES, LANES), lambda i: (i, 0)), pl.BlockSpec((1, d), lambda i: (0, 0))],
        out_specs=xs,
        out_shape=jax.ShapeDtypeStruct((n, d), F32),
        compiler_params=_cparams("parallel"),
        name="final_norm",
    )(x, p, g.reshape(1, d))


def _row_tile(n, pref):
    t = min(pref, n)
    assert n % t == 0, (n, t)
    return t


def _stream(x, w_in_bf, g_mix, dims):
    b, t, d = x.shape
    wsb, wrt, wmm, hd = dims
    scale = hd ** -0.5
    o = 3 * wsb
    outs = [(0, wsb, scale, BF16, 0),
            (wsb, 2 * wsb, 1.0, F32, wsb // hd), (2 * wsb, o, 1.0, F32, wsb // hd),
            (wsb, 2 * wsb, 1.0, BF16, 0), (2 * wsb, o, 1.0, BF16, 0),
            (o, o + 4 * wrt, 1.0, F32, 0),
            (o + 4 * wrt, o + 4 * wrt + wmm, scale, BF16, 0)]
    res = _norm_proj(x.reshape(b * t, d), g_mix, w_in_bf, outs, _row_tile(b * t, 512))
    return [r.reshape(b, t, r.shape[1]) if r.shape[0] == b * t else r for r in res]


def _peer_and_final(x1, g_ffn, wq_bf, sk_bf, u_packed, v_packed, g_final):
    n = x1.shape[0]
    h, eid, gw = _peer_score(x1, g_ffn, wq_bf, sk_bf, _row_tile(n, 512))
    tg = _row_tile(n, 128)
    w = _peer_u(eid, h, gw, u_packed, tg)
    p = _peer_v(eid, w, v_packed, tg)
    return _final(x1, p, g_final, _row_tile(n, 512))


def kernel(x_prompt, x_sample, mem_prompt, cache_sb_k, cache_sb_v, state_ret, cache_mem_k, cache_mem_v,
           g_mix, w_in, w_out, g_sb_out, g_ret_out, g_mem_out, g_mem, w_mem_kv,
           g_ffn, w_peer_q, peer_sub_keys, peer_u, peer_v, g_final):
    depth = w_in.shape[0]
    assert depth == 1, "single-layer step"
    bp, tp, d = x_prompt.shape
    bs, ts, _ = x_sample.shape
    _, _, past, sb_heads, hd = cache_sb_k.shape
    ret_heads = state_ret.shape[2]
    mem_heads = cache_mem_k.shape[3]
    nmem = mem_prompt.shape[1]
    wsb, wrt, wmm = sb_heads * hd, ret_heads * hd, mem_heads * hd
    assert hd == 64 and wrt == wmm and d == 2 * HALF_ROWS * LANES
    l = 0
    w_in_bf = w_in[l].astype(BF16)
    wo_bf = w_out[l].astype(BF16)
    dims = (wsb, wrt, wmm, hd)

    q_p, k_p, v_p, kb_p, vb_p, ret_p, qm_p = _stream(x_prompt, w_in_bf, g_mix[l], dims)
    sb_p = _sb_prompt(q_p, kb_p, vb_p, hd)
    zeros_state = jnp.zeros((bp, wrt, wrt), F32)
    ro_p, st_p = _retention(ret_p, jnp.arange(tp, dtype=I32), zeros_state, ret_heads, hd, _row_tile(tp, 512))
    mk_p, mv_p = _norm_proj(mem_prompt.reshape(bp * nmem, d), g_mem[l], w_mem_kv[l].astype(BF16),
                            [(0, wmm, 1.0, F32, 0), (wmm, 2 * wmm, 1.0, F32, 0)], _row_tile(bp * nmem, 512))
    mk_p = mk_p.reshape(bp, nmem, wmm)
    mv_p = mv_p.reshape(bp, nmem, wmm)
    x1_p = _merge(x_prompt, sb_p, ro_p, ret_p, qm_p, mk_p, mv_p, g_sb_out[l], g_ret_out[l], g_mem_out[l],
                  wo_bf, hd, _row_tile(tp, 256))

    q_s, k_s, v_s, kb_s, vb_s, ret_s, qm_s = _stream(x_sample, w_in_bf, g_mix[l], dims)
    sb_s = _sb_sample(q_s, kb_s, vb_s, cache_sb_k[l].reshape(bs, past, wsb), cache_sb_v[l].reshape(bs, past, wsb), hd)
    s0 = _block_diag_state(state_ret[l].astype(F32), ret_heads, hd)
    ro_s, st_s = _retention(ret_s, past + jnp.arange(ts, dtype=I32), s0, ret_heads, hd, _row_tile(ts, 512))
    x1_s = _merge(x_sample, sb_s, ro_s, ret_s, qm_s, cache_mem_k[l].reshape(bs, nmem, wmm),
                  cache_mem_v[l].reshape(bs, nmem, wmm), g_sb_out[l], g_ret_out[l], g_mem_out[l],
                  wo_bf, hd, _row_tile(ts, 256))

    wq_bf = w_peer_q[l].astype(BF16)
    sk_bf = peer_sub_keys[l].astype(BF16)
    u_packed = _pack_table(peer_u[l])
    v_packed = _pack_table(peer_v[l])
    y_p = _peer_and_final(x1_p.reshape(bp * tp, d), g_ffn[l], wq_bf, sk_bf, u_packed, v_packed, g_final)
    y_s = _peer_and_final(x1_s.reshape(bs * ts, d), g_ffn[l], wq_bf, sk_bf, u_packed, v_packed, g_final)

    return (y_p.reshape(bp, tp, d), y_s.reshape(bs, ts, d),
            k_p.reshape(1, bp, tp, sb_heads, hd), v_p.reshape(1, bp, tp, sb_heads, hd),
            _diag_blocks(st_p, ret_heads, hd)[None],
            mk_p.reshape(1, bp, nmem, mem_heads, hd), mv_p.reshape(1, bp, nmem, mem_heads, hd),
            k_s.reshape(1, bs, ts, sb_heads, hd), v_s.reshape(1, bs, ts, sb_heads, hd),
            _diag_blocks(st_s, ret_heads, hd)[None].astype(state_ret.dtype))
```

```python
import functools

import numpy as np
import jax
import jax.numpy as jnp
from jax import lax
from jax.experimental import pallas as pl
from jax.experimental.pallas import tpu as pltpu

F32 = jnp.float32
BF16 = jnp.bfloat16
I32 = jnp.int32

EPS = 1e-6
ROPE_BASE = 10000.0
PEER_TOPK = 16
RET_CHUNK = 64
LANES = 128
ROW_TILES = 8
HALF_ROWS = 4
VMEM_LIMIT = 56 * 1024 * 1024
SB_DEAD_LOG = -105.0


def _cparams(*sem):
    return pltpu.CompilerParams(dimension_semantics=sem, vmem_limit_bytes=VMEM_LIMIT)


def _dot(a, b):
    return jnp.dot(a, b, preferred_element_type=F32)


def _dot_nt(a, b):
    return lax.dot_general(a, b, (((1,), (1,)), ((), ())), preferred_element_type=F32)


def _split_bf16(x):
    hi = x.astype(BF16)
    lo = (x - hi.astype(F32)).astype(BF16)
    return hi, lo


def _dot_split(x, m_bf16):
    hi, lo = _split_bf16(x)
    return _dot(hi, m_bf16) + _dot(lo, m_bf16)


def _norm_proj_kernel(x_ref, g_ref, w_ref, *out_refs, cols):
    x = x_ref[...]
    ms = jnp.mean(x * x, axis=-1, keepdims=True)
    h = x * lax.rsqrt(ms + EPS) * g_ref[...]
    p = _dot(h.astype(BF16), w_ref[...])
    tm = x.shape[0]
    for ref, (lo, hi, scale, heads) in zip(out_refs, cols):
        v = p[:, lo:hi]
        if scale != 1.0:
            v = v * scale
        if heads:
            hd = (hi - lo) // heads
            for h in range(heads):
                ref[pl.ds(h, tm, stride=heads), :] = v[:, h * hd:(h + 1) * hd].astype(ref.dtype)
        else:
            ref[...] = v.astype(ref.dtype)


def _norm_proj(x, g, w_bf16, outs, tm):
    n, d = x.shape
    wcols = w_bf16.shape[1]
    shape = lambda rows, lo, hi, hds: (rows * hds, (hi - lo) // hds) if hds else (rows, hi - lo)
    out_shape = [jax.ShapeDtypeStruct(shape(n, lo, hi, hds), dt) for lo, hi, _, dt, hds in outs]
    out_specs = [pl.BlockSpec(shape(tm, lo, hi, hds), lambda i: (i, 0)) for lo, hi, _, _, hds in outs]
    return pl.pallas_call(
        functools.partial(_norm_proj_kernel, cols=tuple((lo, hi, sc, hds) for lo, hi, sc, _, hds in outs)),
        grid=(n // tm,),
        in_specs=[pl.BlockSpec((tm, d), lambda i: (i, 0)),
                  pl.BlockSpec((1, d), lambda i: (0, 0)),
                  pl.BlockSpec((d, wcols), lambda i: (0, 0))],
        out_specs=out_specs,
        out_shape=out_shape,
        compiler_params=_cparams("parallel"),
        name="norm_proj",
    )(x, g.reshape(1, d), w_bf16)


def _suffix_matrix(tk):
    j = np.arange(2 * tk)[:, None]
    s = np.arange(2 * tk)[None, :]
    return jnp.asarray(((j // tk == s // tk) & (j >= s)).astype(np.float32), dtype=BF16)


def _sb_step(q2, kb, vb, u, carry, tk, hd, valid):
    c0, c1, acc = carry
    tq = q2.shape[0]
    lane = lax.broadcasted_iota(I32, kb.shape, 1)
    first = lane < hd
    zero = jnp.zeros_like(kb)
    kk = jnp.concatenate([jnp.where(first, kb, zero), jnp.where(first, zero, kb)], axis=0)
    vv = jnp.concatenate([jnp.where(first, vb, zero), jnp.where(first, zero, vb)], axis=0)
    z = _dot_nt(q2, kk)
    sp = jnp.log(1.0 + jnp.exp(-jnp.abs(z)))
    lk = jnp.minimum(-z, 0.0) - sp
    if valid is not None:
        lk = jnp.where(valid, lk, 0.0)
    m = _dot_split(lk, u)
    c = jnp.concatenate([jnp.broadcast_to(c0, (tq, tk)), jnp.broadcast_to(c1, (tq, tk))], axis=1)
    a = jnp.exp(z + c + m)
    if valid is not None:
        a = jnp.where(valid, a, 0.0)
    acc = acc + _dot(a.astype(BF16), vv)
    c0 = c0 + jnp.sum(lk[:, :tk], axis=-1, keepdims=True)
    c1 = c1 + jnp.sum(lk[:, tk:], axis=-1, keepdims=True)
    return c0, c1, acc


def _sb_alive(carries):
    m = functools.reduce(jnp.maximum, [jnp.maximum(c0, c1) for c0, c1, _ in carries])
    return jnp.max(m) > SB_DEAD_LOG


def _sb_far_blocks(qs, block, nblk, u, carries, tk, hd, first=0):
    n = len(qs)
    flatten = lambda cs: tuple(x for c in cs for x in c)
    unflatten = lambda flat: [tuple(flat[3 * p:3 * p + 3]) for p in range(n)]

    def cond(state):
        return jnp.logical_and(state[0] < nblk, state[1])

    def body(state):
        i = state[0]
        kvs = block(i)
        cs = [_sb_step(qs[p], kvs[p][0], kvs[p][1], u, c, tk, hd, None) for p, c in enumerate(unflatten(state[2:]))]
        return (i + 1, _sb_alive(cs)) + flatten(cs)

    state = lax.while_loop(cond, body, (jnp.int32(first), _sb_alive(carries)) + flatten(carries))
    return unflatten(state[2:])


SB_PAIRS_PER_STEP = 4


def _sb_prompt_kernel(q_ref, k_ref, v_ref, u_ref, o_ref, *, tq, tk, hd, npairs):
    q0 = pl.program_id(2) * tq
    lanes = [slice(p * LANES, (p + 1) * LANES) for p in range(npairs)]
    qs = [q_ref[0, :, ln] for ln in lanes]
    u = u_ref[...]
    row = lax.broadcasted_iota(I32, (tq, 2 * tk), 0)
    koff = lax.broadcasted_iota(I32, (tq, 2 * tk), 1) & (tk - 1)
    carries = [(jnp.zeros((tq, 1), F32), jnp.zeros((tq, 1), F32), jnp.zeros((tq, LANES), F32)) for _ in lanes]
    nd = tq // tk
    for d in range(nd + 1):
        off = (nd - 1 - d) * tk
        limit = row if d < nd else row + jnp.where(q0 >= tk, 0, -4 * tq)
        valid = (koff + off) < limit
        ks = pl.multiple_of(jnp.maximum(q0 + off, 0), tk)
        carries = [_sb_step(qs[p], k_ref[0, pl.ds(ks, tk), ln], v_ref[0, pl.ds(ks, tk), ln], u, carries[p], tk, hd, valid)
                   for p, ln in enumerate(lanes)]

    def block(i):
        ks = pl.multiple_of(q0 - (i + 1) * tk, tk)
        return [(k_ref[0, pl.ds(ks, tk), ln], v_ref[0, pl.ds(ks, tk), ln]) for ln in lanes]

    carries = _sb_far_blocks(qs, block, q0 // tk, u, carries, tk, hd, first=1)
    for p, ln in enumerate(lanes):
        o_ref[0, :, ln] = carries[p][2]


def _sb_prompt(q, k, v, hd, tq=256, tk=128):
    b, t, w = q.shape
    npairs = SB_PAIRS_PER_STEP
    wb = npairs * LANES
    assert w % wb == 0
    return pl.pallas_call(
        functools.partial(_sb_prompt_kernel, tq=tq, tk=tk, hd=hd, npairs=npairs),
        grid=(b, w // wb, t // tq),
        in_specs=[pl.BlockSpec((1, tq, wb), lambda bi, hp, qi: (bi, qi, hp)),
                  pl.BlockSpec((1, t, wb), lambda bi, hp, qi: (bi, 0, hp)),
                  pl.BlockSpec((1, t, wb), lambda bi, hp, qi: (bi, 0, hp)),
                  pl.BlockSpec((2 * tk, 2 * tk), lambda bi, hp, qi: (0, 0))],
        out_specs=pl.BlockSpec((1, tq, wb), lambda bi, hp, qi: (bi, qi, hp)),
        out_shape=jax.ShapeDtypeStruct((b, t, w), F32),
        compiler_params=_cparams("parallel", "parallel", "arbitrary"),
        name="sb_prompt",
    )(q, k, v, _suffix_matrix(tk))


def _sb_sample_kernel(q_ref, kn_ref, vn_ref, kc_ref, vc_ref, un_ref, u_ref, o_ref, *, ts, tk, hd, past):
    q2 = q_ref[0]
    row = lax.broadcasted_iota(I32, (ts, 2 * ts), 0)
    koff = lax.broadcasted_iota(I32, (ts, 2 * ts), 1) & (ts - 1)
    carry = (jnp.zeros((ts, 1), F32), jnp.zeros((ts, 1), F32), jnp.zeros((ts, LANES), F32))
    carry = _sb_step(q2, kn_ref[0], vn_ref[0], un_ref[...], carry, ts, hd, koff < row)
    u = u_ref[...]

    def block(i):
        ks = pl.multiple_of(past - (i + 1) * tk, tk)
        return kc_ref[0, pl.ds(ks, tk), :].astype(BF16), vc_ref[0, pl.ds(ks, tk), :].astype(BF16)

    (carry,) = _sb_far_blocks([q2], lambda i: [block(i)], past // tk, u, [carry], tk, hd)
    o_ref[0] = carry[2]


def _sb_sample(q, kn, vn, kc, vc, hd, tk=128):
    b, ts, w = q.shape
    past = kc.shape[1]
    new_spec = pl.BlockSpec((1, ts, LANES), lambda bi, hp: (bi, 0, hp))
    cache_spec = pl.BlockSpec((1, past, LANES), lambda bi, hp: (bi, 0, hp))
    return pl.pallas_call(
        functools.partial(_sb_sample_kernel, ts=ts, tk=tk, hd=hd, past=past),
        grid=(b, w // LANES),
        in_specs=[new_spec, new_spec, new_spec, cache_spec, cache_spec,
                  pl.BlockSpec((2 * ts, 2 * ts), lambda bi, hp: (0, 0)),
                  pl.BlockSpec((2 * tk, 2 * tk), lambda bi, hp: (0, 0))],
        out_specs=new_spec,
        out_shape=jax.ShapeDtypeStruct((b, ts, w), F32),
        compiler_params=_cparams("parallel", "parallel"),
        name="sb_sample",
    )(q, kn, vn, kc, vc, _suffix_matrix(ts), _suffix_matrix(tk))


def _ret_kernel(q_ref, k_ref, v_ref, cos_ref, sin_ref, dec_ref, qd_ref, kd_ref, gl_ref, bd_ref, s0_ref,
                o_ref, st_ref, s_scr, *, tc, nh, hd, k_scale):
    t = pl.program_id(1)
    w = nh * hd
    L = RET_CHUNK

    @pl.when(t == 0)
    def _():
        s_scr[...] = s0_ref[0]

    lane = lax.broadcasted_iota(I32, (L, w), 1)
    first_half = (lane & (hd - 1)) < hd // 2
    head = lane // hd

    def rope(x, cos, sin):
        swapped = jnp.where(first_half, pltpu.roll(x, w - hd // 2, 1), pltpu.roll(x, hd // 2, 1))
        return x * cos + swapped * sin

    s = s_scr[...]
    for ci in range(tc // L):
        sl = slice(ci * L, (ci + 1) * L)
        cos = cos_ref[sl, :]
        sin = sin_ref[sl, :]
        q = rope(q_ref[0, sl, :], cos, sin)
        k = rope(k_ref[0, sl, :], cos, sin) * k_scale
        qb = q.astype(BF16)
        kb = k.astype(BF16)
        vb = v_ref[0, sl, :].astype(BF16)
        zero = jnp.zeros_like(qb)
        out = _dot(qb, s.astype(BF16)) * qd_ref[...]
        for h in range(nh):
            mh = head == h
            sc = _dot_nt(jnp.where(mh, qb, zero), kb)
            p = (sc * dec_ref[h]).astype(BF16)
            out = out + _dot(p, jnp.where(mh, vb, zero))
        o_ref[0, sl, :] = out
        kd = (k * kd_ref[...]).astype(BF16)
        outer = lax.dot_general(kd, vb, (((0,), (0,)), ((), ())), preferred_element_type=F32)
        s = gl_ref[...] * s + bd_ref[...] * outer
    s_scr[...] = s

    @pl.when(t == pl.num_programs(1) - 1)
    def _():
        st_ref[0] = s


def _retention(ret, pos, s0, nh, hd, tc):
    b, t, _ = ret.shape
    w = nh * hd
    L = RET_CHUNK
    half = hd // 2
    freqs = ROPE_BASE ** (-jnp.arange(half, dtype=F32) / half)
    ang = pos.astype(F32)[:, None] * freqs[None, :]
    cos = jnp.tile(jnp.cos(ang), (1, 2 * nh))
    sin = jnp.tile(jnp.concatenate([-jnp.sin(ang), jnp.sin(ang)], axis=1), (1, nh))
    lg = jnp.log1p(-jnp.exp2(-5.0 - jnp.arange(nh, dtype=F32)))
    idx = jnp.arange(L, dtype=F32)
    diff = idx[:, None] - idx[None, :]
    causal = diff >= 0
    dec = jnp.where(causal[None], jnp.exp(jnp.where(causal, diff, 0.0)[None] * lg[:, None, None]), 0.0)
    lg_lane = jnp.repeat(lg, hd)[None, :]
    qd = jnp.exp((idx + 1.0)[:, None] * lg_lane)
    kd = jnp.exp((L - 1.0 - idx)[:, None] * lg_lane)
    gl = jnp.exp(L * lg_lane)
    hid = jnp.arange(w) // hd
    bd = (hid[:, None] == hid[None, :]).astype(F32)
    tok = lambda j: pl.BlockSpec((1, tc, w), lambda bi, ti: (bi, ti, j))
    tab = pl.BlockSpec((tc, w), lambda bi, ti: (ti, 0))
    const = lambda shape: pl.BlockSpec(shape, lambda bi, ti: (0,) * len(shape))
    state = pl.BlockSpec((1, w, w), lambda bi, ti: (bi, 0, 0))
    return pl.pallas_call(
        functools.partial(_ret_kernel, tc=tc, nh=nh, hd=hd, k_scale=hd ** -0.5),
        grid=(b, t // tc),
        in_specs=[tok(0), tok(1), tok(2), tab, tab, const((nh, L, L)), const((L, w)), const((L, w)),
                  const((1, w)), const((w, w)), state],
        out_specs=[pl.BlockSpec((1, tc, w), lambda bi, ti: (bi, ti, 0)), state],
        out_shape=[jax.ShapeDtypeStruct((b, t, w), F32), jax.ShapeDtypeStruct((b, w, w), F32)],
        scratch_shapes=[pltpu.VMEM((w, w), F32)],
        compiler_params=_cparams("parallel", "arbitrary"),
        name="retention",
    )(ret, ret, ret, cos, sin, dec, qd, kd, gl, bd, s0)


def _block_diag_state(st, nh, hd):
    b = st.shape[0]
    eye = jnp.eye(nh, dtype=st.dtype)
    return jnp.einsum('bhde,hg->bhdge', st, eye).reshape(b, nh * hd, nh * hd)


def _diag_blocks(s, nh, hd):
    b = s.shape[0]
    s5 = s.reshape(b, nh, hd, nh, hd)
    return jnp.stack([s5[:, h, :, h, :] for h in range(nh)], axis=1)


def _head_mean_matrix(w, hd):
    hid = np.arange(w) // hd
    return jnp.asarray((hid[:, None] == hid[None, :]).astype(np.float32) / hd, dtype=BF16)


def _headnorm(x, g_mat, gain):
    ms = _dot_split(x * x, g_mat)
    return x * lax.rsqrt(ms + EPS) * gain


def _merge_kernel(x_ref, sb_ref, ret_ref, gate_ref, qm_ref, mk_ref, mv_ref, gsb_ref, gret_ref, gmo_ref,
                  msb_ref, mrt_ref, wo_ref, x1_ref, *, nh_mem, hd):
    wsb = sb_ref.shape[2]
    wrt = ret_ref.shape[2]
    qm = qm_ref[0]
    mkb = mk_ref[0].astype(BF16)
    mvb = mv_ref[0].astype(BF16)
    head_q = lax.broadcasted_iota(I32, qm.shape, 1) // hd
    head_m = lax.broadcasted_iota(I32, mkb.shape, 1) // hd
    mem_o = jnp.zeros(qm.shape, F32)
    for h in range(nh_mem):
        s = _dot_nt(jnp.where(head_q == h, qm, jnp.zeros_like(qm)), mkb)
        e = jnp.exp(s - jnp.max(s, axis=-1, keepdims=True))
        p = e / jnp.sum(e, axis=-1, keepdims=True)
        mem_o = mem_o + _dot(p.astype(BF16), jnp.where(head_m == h, mvb, jnp.zeros_like(mvb)))
    sb_n = _headnorm(sb_ref[0], msb_ref[...], gsb_ref[...])
    gate = gate_ref[0]
    ret_n = _headnorm(ret_ref[0], mrt_ref[...], gret_ref[...]) * (gate / (1.0 + jnp.exp(-gate)))
    mem_n = _headnorm(mem_o, mrt_ref[...], gmo_ref[...])
    y = (_dot(sb_n.astype(BF16), wo_ref[0:wsb, :])
         + _dot(ret_n.astype(BF16), wo_ref[wsb:wsb + wrt, :])
         + _dot(mem_n.astype(BF16), wo_ref[wsb + wrt:, :]))
    x1_ref[0] = x_ref[0] + y


def _merge(x, sb_o, ret_o, ret, qm, mk, mv, g_sb, g_ret, g_mo, wo_bf16, hd, tm):
    b, t, d = x.shape
    wsb = sb_o.shape[2]
    wrt = ret_o.shape[2]
    nmem = mk.shape[1]
    tok = lambda wd, j=0: pl.BlockSpec((1, tm, wd), lambda bi, ti: (bi, ti, j))
    const = lambda shape: pl.BlockSpec(shape, lambda bi, ti: (0,) * len(shape))
    memspec = pl.BlockSpec((1, nmem, wrt), lambda bi, ti: (bi, 0, 0))
    return pl.pallas_call(
        functools.partial(_merge_kernel, nh_mem=wrt // hd, hd=hd),
        grid=(b, t // tm),
        in_specs=[tok(d), tok(wsb), tok(wrt), tok(wrt, 3), tok(wrt), memspec, memspec,
                  const((1, wsb)), const((1, wrt)), const((1, wrt)),
                  const((wsb, wsb)), const((wrt, wrt)), const((d, d))],
        out_specs=tok(d),
        out_shape=jax.ShapeDtypeStruct((b, t, d), F32),
        compiler_params=_cparams("parallel", "parallel"),
        name="merge",
    )(x, sb_o, ret_o, ret, qm, mk, mv, g_sb.reshape(1, wsb), g_ret.reshape(1, wrt), g_mo.reshape(1, wrt),
      _head_mean_matrix(wsb, hd), _head_mean_matrix(wrt, hd), wo_bf16)


def _topk_rows(s, k, rid=None):
    if rid is None:
        rid = lax.broadcasted_iota(I32, s.shape, 0)
    big = jnp.int32(2 ** 30)
    vals, idxs = [], []
    for _ in range(k):
        m = jnp.max(s, axis=0, keepdims=True)
        i = jnp.min(jnp.where(s == m, rid, big), axis=0, keepdims=True)
        vals.append(m)
        idxs.append(i)
        s = jnp.where(rid == i, -jnp.inf, s)
    return jnp.concatenate(vals, axis=0), jnp.concatenate(idxs, axis=0)


def _staircase_blocks(k, rows=8):
    blocks = []
    singles = [a for a in range(k) if k // (a + 1) == 1]
    for a in range(k):
        nb = k // (a + 1)
        if nb > 1:
            blocks.extend(('row', a, b0) for b0 in range(0, nb, rows))
    assert singles and singles[0] % rows == 0 and len(singles) % rows == 0 and singles[-1] == k - 1
    blocks.extend(('col', a0, 0) for a0 in range(singles[0], k, rows))
    flat = []
    for kind, a, b in blocks:
        flat.extend([a * k + b + i for i in range(rows)] if kind == 'row' else [(a + i) * k + b for i in range(rows)])
    need = {(a, b) for a in range(k) for b in range(k) if (a + 1) * (b + 1) <= k}
    assert need <= {(f // k, f % k) for f in flat} and len(set(flat)) == len(flat)
    return blocks, np.array(flat, np.int32)[:, None]


def _take_rows(tbl, idx):
    out = jnp.zeros(idx.shape, tbl.dtype)
    for a in range(tbl.shape[0]):
        out = out + jnp.where(idx == a, tbl[a:a + 1, :], 0)
    return out


def _peer_score_kernel(x_ref, g_ref, wq_ref, sk_ref, cf_ref, h_ref, eid_ref, gw_ref, hb_scr,
                       eid_scr, gw_scr, sc_scr, *, topk, nkeys, blocks):
    head = pl.program_id(1)

    @pl.when(head == 0)
    def _():
        x = x_ref[...]
        ms = jnp.mean(x * x, axis=-1, keepdims=True)
        hn = x * lax.rsqrt(ms + EPS) * g_ref[...]
        for r in range(ROW_TILES):
            h_ref[pl.ds(r, hn.shape[0], stride=ROW_TILES), :] = hn[:, r * LANES:(r + 1) * LANES]
        hb_scr[...] = hn.astype(BF16)

    qb = _dot(hb_scr[...], wq_ref[...]).astype(BF16)
    hq = qb.shape[1] // 2
    sc_scr[0] = _dot_nt(sk_ref[0], qb[:, :hq])
    sc_scr[1] = _dot_nt(sk_ref[1], qb[:, hq:])
    brows = cf_ref.shape[0] // len(blocks)
    out_rows = pl.ds(pl.multiple_of(head * topk, topk), topk)
    width = min(LANES, qb.shape[0])

    def column_group(c):
        cols = slice(c * width, (c + 1) * width)
        sv0, si0 = _topk_rows(sc_scr[0, :, cols], topk)
        sv1, si1 = _topk_rows(sc_scr[1, :, cols], topk)
        cand = jnp.concatenate(
            [sv0[a:a + 1, :] + sv1[b:b + brows, :] if kind == 'row' else sv0[a:a + brows, :] + sv1[b:b + 1, :]
             for kind, a, b in blocks], axis=0)
        cv, ci = _topk_rows(cand, topk, cf_ref[:, cols])
        e1 = _take_rows(si0, ci // topk)
        e2 = _take_rows(si1, ci % topk)
        eid_scr[out_rows, cols] = ((e1 * nkeys + e2) * HALF_ROWS).astype(F32)
        e = jnp.exp(cv - cv[0:1, :])
        gw_scr[out_rows, cols] = e / jnp.sum(e, axis=0, keepdims=True)

    for c in range(qb.shape[0] // width):
        column_group(c)

    @pl.when(head == pl.num_programs(1) - 1)
    def _():
        eid_ref[...] = eid_scr[...].T.astype(I32)
        gw_ref[...] = gw_scr[...].T


def _peer_score(x, g, wq_bf16, sk_bf16, tm):
    n, d = x.shape
    nkeys, khalf = sk_bf16.shape[1], sk_bf16.shape[2]
    nheads = wq_bf16.shape[1] // (2 * khalf)
    k = PEER_TOPK
    blocks, flat = _staircase_blocks(k)
    cf = jnp.asarray(np.broadcast_to(flat, (flat.shape[0], tm)))
    xs = pl.BlockSpec((tm, d), lambda i, h: (i, 0))
    cs = pl.BlockSpec(cf.shape, lambda i, h: (0, 0))
    return pl.pallas_call(
        functools.partial(_peer_score_kernel, topk=k, nkeys=nkeys, blocks=tuple(blocks)),
        grid=(n // tm, nheads),
        in_specs=[xs, pl.BlockSpec((1, d), lambda i, h: (0, 0)),
                  pl.BlockSpec((d, 2 * khalf), lambda i, h: (0, h)),
                  pl.BlockSpec((2, nkeys, khalf), lambda i, h: (0, 0, 0)), cs],
        out_specs=[pl.BlockSpec((tm * ROW_TILES, LANES), lambda i, h: (i, 0)),
                   pl.BlockSpec((tm, nheads * k), lambda i, h: (i, 0)),
                   pl.BlockSpec((tm, nheads * k), lambda i, h: (i, 0))],
        out_shape=[jax.ShapeDtypeStruct((n * ROW_TILES, LANES), F32),
                   jax.ShapeDtypeStruct((n, nheads * k), I32), jax.ShapeDtypeStruct((n, nheads * k), F32)],
        scratch_shapes=[pltpu.VMEM((tm, d), BF16), pltpu.VMEM((nheads * k, tm), F32),
                        pltpu.VMEM((nheads * k, tm), F32), pltpu.VMEM((2, nkeys, tm), F32)],
        compiler_params=_cparams("parallel", "arbitrary"),
        name="peer_score",
    )(x, g.reshape(1, d), wq_bf16, sk_bf16, cf)


def _pack_table(tab):
    e, d = tab.shape
    assert d == ROW_TILES * LANES
    te = _row_tile(e, 512)
    return pl.pallas_call(
        _pack_kernel,
        grid=(e // te,),
        in_specs=[pl.BlockSpec((te, d), lambda i: (i, 0))],
        out_specs=pl.BlockSpec((te * HALF_ROWS, LANES), lambda i: (i, 0)),
        out_shape=jax.ShapeDtypeStruct((e * HALF_ROWS, LANES), I32),
        compiler_params=_cparams("parallel"),
        name="pack_table",
    )(tab)


def _pack_words(even, odd):
    odd = lax.bitcast_convert_type(odd, I32)
    low = lax.shift_right_logical(lax.bitcast_convert_type(even.astype(BF16).astype(F32), I32), 16)
    sign = odd & jnp.int32(-2 ** 31)
    mag = (odd & jnp.int32(2 ** 31 - 1)) + jnp.int32(0x8000)
    high = jnp.where(mag >= low, (mag - low) >> 16, 0)
    return sign | (high << 16) | low


def _pack_kernel(t_ref, o_ref):
    te = t_ref.shape[0]
    x = t_ref[...]
    for s in range(HALF_ROWS):
        o_ref[pl.ds(s, te, stride=HALF_ROWS), :] = _pack_words(x[:, (2 * s) * LANES:(2 * s + 1) * LANES],
                                                               x[:, (2 * s + 1) * LANES:(2 * s + 2) * LANES])


def _unpack(wd):
    even = lax.bitcast_convert_type(wd << 16, F32)
    odd = lax.bitcast_convert_type(wd, F32)
    return even, odd


def _erf(x):
    return lax.erf(x)


def _token_tiles(ref, t):
    base = pl.multiple_of(t * ROW_TILES, ROW_TILES)
    return ref[pl.ds(base, HALF_ROWS, stride=2), :], ref[pl.ds(base + 1, HALF_ROWS, stride=2), :]


PEER_U_SET = 4


def _peer_u_kernel(eid_ref, h_ref, gw_ref, tab_ref, w_ref, pa, pb, q_scr, act_scr, *, tg, npairs):
    ones = jnp.ones((LANES, LANES), BF16)
    row = lax.broadcasted_iota(I32, (npairs, LANES), 0)
    eye = (row == lax.broadcasted_iota(I32, (npairs, LANES), 1)).astype(F32)
    ns = PEER_U_SET

    def products(t, p_scr, qset, k):
        he, ho = _token_tiles(h_ref, t)
        he2 = jnp.concatenate([he, he], axis=0)
        ho2 = jnp.concatenate([ho, ho], axis=0)
        for j in range(0, npairs, 2):
            wa = tab_ref[pl.ds(pl.multiple_of(eid_ref[t, j], HALF_ROWS), HALF_ROWS), :]
            wb = tab_ref[pl.ds(pl.multiple_of(eid_ref[t, j + 1], HALF_ROWS), HALF_ROWS), :]
            even, odd = _unpack(jnp.concatenate([wa, wb], axis=0))
            p_scr[j * HALF_ROWS:(j + 2) * HALF_ROWS, :] = even * he2 + odd * ho2
        q = p_scr[pl.ds(0, npairs, stride=HALF_ROWS), :]
        for s in range(1, HALF_ROWS):
            q = q + p_scr[pl.ds(s, npairs, stride=HALF_ROWS), :]
        hi, lo = _split_bf16(q)
        q_scr[qset, 2 * k * npairs:(2 * k + 1) * npairs, :] = hi
        q_scr[qset, (2 * k + 1) * npairs:(2 * k + 2) * npairs, :] = lo

    def lane_sums(qset, tokens):
        r = _dot(q_scr[qset], ones)
        for k, t in enumerate(tokens):
            rk = r[2 * k * npairs:(2 * k + 1) * npairs, :] + r[(2 * k + 1) * npairs:(2 * k + 2) * npairs, :]
            act_scr[pl.ds(t, 1), :] = jnp.sum(rk * eye, axis=0, keepdims=True)

    q_scr[1] = jnp.zeros(q_scr.shape[1:], BF16)

    def body(i, carry):
        t0 = 2 * ns * i
        lane_sums(1, [jnp.maximum(t0 - ns + k, 0) for k in range(ns)])
        for k in range(ns):
            products(t0 + k, (pa, pb)[k % 2], 0, k)
        lane_sums(0, [t0 + k for k in range(ns)])
        for k in range(ns):
            products(t0 + ns + k, (pa, pb)[k % 2], 1, k)
        return carry

    lax.fori_loop(0, tg // (2 * ns), body, 0)
    lane_sums(1, [tg - ns + k for k in range(ns)])
    act = act_scr[...]
    gelu = 0.5 * act * (1.0 + _erf(act * (2.0 ** -0.5)))
    w = gw_ref[...] * gelu
    w_ref[...] = _pack_words(w, pltpu.roll(w, npairs - 1, 1))


def _peer_u(eid, h, gw, tab_packed, tg):
    n, npairs = eid.shape
    assert npairs == LANES and h.shape == (n * ROW_TILES, LANES) and tg % (2 * PEER_U_SET) == 0
    tspec = pl.BlockSpec((tg, npairs), lambda i: (i, 0))
    pshape = pltpu.VMEM((npairs * HALF_ROWS, LANES), F32)
    qshape = pltpu.VMEM((2, 2 * PEER_U_SET * npairs, LANES), BF16)
    return pl.pallas_call(
        functools.partial(_peer_u_kernel, tg=tg, npairs=npairs),
        grid=(n // tg,),
        in_specs=[pl.BlockSpec((tg, npairs), lambda i: (i, 0), memory_space=pltpu.SMEM),
                  pl.BlockSpec((tg * ROW_TILES, LANES), lambda i: (i, 0)), tspec,
                  pl.BlockSpec(memory_space=pltpu.VMEM)],
        out_specs=tspec,
        out_shape=jax.ShapeDtypeStruct((n, npairs), I32),
        scratch_shapes=[pshape, pshape, qshape, pltpu.VMEM((tg, npairs), F32)],
        compiler_params=_cparams("arbitrary"),
        name="peer_u",
    )(eid, h, gw, tab_packed)


PEER_V_ACCUMULATORS = 2


def _peer_v_kernel(eid_ref, w_ref, tab_ref, o_ref, *, tg, npairs):
    nacc = PEER_V_ACCUMULATORS

    def token(t):
        acc_e = [jnp.zeros((HALF_ROWS, LANES), F32) for _ in range(nacc)]
        acc_o = [jnp.zeros((HALF_ROWS, LANES), F32) for _ in range(nacc)]
        for j in range(npairs):
            if j % 2 == 0:
                wpair = _unpack(jnp.full((HALF_ROWS, LANES), w_ref[t, j], I32))
            wt = wpair[j % 2]
            e4 = pl.multiple_of(eid_ref[t, j], HALF_ROWS)
            even, odd = _unpack(tab_ref[pl.ds(e4, HALF_ROWS), :])
            acc_e[j % nacc] = acc_e[j % nacc] + wt * even
            acc_o[j % nacc] = acc_o[j % nacc] + wt * odd
        base = pl.multiple_of(t * ROW_TILES, ROW_TILES)
        o_ref[pl.ds(base, HALF_ROWS, stride=2), :] = functools.reduce(lambda a, b: a + b, acc_e)
        o_ref[pl.ds(base + 1, HALF_ROWS, stride=2), :] = functools.reduce(lambda a, b: a + b, acc_o)

    def body(i, carry):
        token(2 * i)
        token(2 * i + 1)
        return carry

    lax.fori_loop(0, tg // 2, body, 0)


def _peer_v(eid, w, tab_packed, tg):
    n, npairs = eid.shape
    assert tg % 2 == 0
    sspec = pl.BlockSpec((tg, npairs), lambda i: (i, 0), memory_space=pltpu.SMEM)
    return pl.pallas_call(
        functools.partial(_peer_v_kernel, tg=tg, npairs=npairs),
        grid=(n // tg,),
        in_specs=[sspec, sspec, pl.BlockSpec(memory_space=pltpu.VMEM)],
        out_specs=pl.BlockSpec((tg * ROW_TILES, LANES), lambda i: (i, 0)),
        out_shape=jax.ShapeDtypeStruct((n * ROW_TILES, LANES), F32),
        compiler_params=_cparams("arbitrary"),
        name="peer_v",
    )(eid, w, tab_packed)


def _final_kernel(x_ref, p_ref, g_ref, y_ref):
    tm = x_ref.shape[0]
    p = jnp.concatenate([p_ref[pl.ds(r, tm, stride=ROW_TILES), :] for r in range(ROW_TILES)], axis=1)
    x = x_ref[...] + p
    ms = jnp.mean(x * x, axis=-1, keepdims=True)
    y_ref[...] = x * lax.rsqrt(ms + EPS) * g_ref[...]


def _final(x, p, g, tm):
    n, d = x.shape
    xs = pl.BlockSpec((tm, d), lambda i: (i, 0))
    return pl.pallas_call(
        _final_kernel,
        grid=(n // tm,),
        in_specs=[xs, pl.BlockSpec((tm * ROW_TILES, LANES), lambda i: (i, 0)), pl.BlockSpec((1, d), lambda i: (0, 0))],
        out_specs=xs,
        out_shape=jax.ShapeDtypeStruct((n, d), F32),
        compiler_params=_cparams("parallel"),
        name="final_norm",
    )(x, p, g.reshape(1, d))


def _row_tile(n, pref):
    t = min(pref, n)
    assert n % t == 0, (n, t)
    return t


def _stream(x, w_in_bf, g_mix, dims):
    b, t, d = x.shape
    wsb, wrt, wmm, hd = dims
    scale = hd ** -0.5
    o = 3 * wsb
    outs = [(0, wsb, scale, BF16, 0),
            (wsb, 2 * wsb, 1.0, F32, wsb // hd), (2 * wsb, o, 1.0, F32, wsb // hd),
            (wsb, 2 * wsb, 1.0, BF16, 0), (2 * wsb, o, 1.0, BF16, 0),
            (o, o + 4 * wrt, 1.0, F32, 0),
            (o + 4 * wrt, o + 4 * wrt + wmm, scale, BF16, 0)]
    res = _norm_proj(x.reshape(b * t, d), g_mix, w_in_bf, outs, _row_tile(b * t, 512))
    return [r.reshape(b, t, r.shape[1]) if r.shape[0] == b * t else r for r in res]


def _peer_and_final(x1, g_ffn, wq_bf, sk_bf, u_packed, v_packed, g_final):
    n = x1.shape[0]
    h, eid, gw = _peer_score(x1, g_ffn, wq_bf, sk_bf, _row_tile(n, 512))
    tg = _row_tile(n, 128)
    w = _peer_u(eid, h, gw, u_packed, tg)
    p = _peer_v(eid, w, v_packed, tg)
    return _final(x1, p, g_final, _row_tile(n, 512))


def kernel(x_prompt, x_sample, mem_prompt, cache_sb_k, cache_sb_v, state_ret, cache_mem_k, cache_mem_v,
           g_mix, w_in, w_out, g_sb_out, g_ret_out, g_mem_out, g_mem, w_mem_kv,
           g_ffn, w_peer_q, peer_sub_keys, peer_u, peer_v, g_final):
    depth = w_in.shape[0]
    assert depth == 1, "single-layer step"
    bp, tp, d = x_prompt.shape
    bs, ts, _ = x_sample.shape
    _, _, past, sb_heads, hd = cache_sb_k.shape
    ret_heads = state_ret.shape[2]
    mem_heads = cache_mem_k.shape[3]
    nmem = mem_prompt.shape[1]
    wsb, wrt, wmm = sb_heads * hd, ret_heads * hd, mem_heads * hd
    assert hd == 64 and wrt == wmm and d == 2 * HALF_ROWS * LANES
    l = 0
    w_in_bf = w_in[l].astype(BF16)
    wo_bf = w_out[l].astype(BF16)
    dims = (wsb, wrt, wmm, hd)

    q_p, k_p, v_p, kb_p, vb_p, ret_p, qm_p = _stream(x_prompt, w_in_bf, g_mix[l], dims)
    sb_p = _sb_prompt(q_p, kb_p, vb_p, hd)
    zeros_state = jnp.zeros((bp, wrt, wrt), F32)
    ro_p, st_p = _retention(ret_p, jnp.arange(tp, dtype=I32), zeros_state, ret_heads, hd, _row_tile(tp, 512))
    mk_p, mv_p = _norm_proj(mem_prompt.reshape(bp * nmem, d), g_mem[l], w_mem_kv[l].astype(BF16),
                            [(0, wmm, 1.0, F32, 0), (wmm, 2 * wmm, 1.0, F32, 0)], _row_tile(bp * nmem, 512))
    mk_p = mk_p.reshape(bp, nmem, wmm)
    mv_p = mv_p.reshape(bp, nmem, wmm)
    x1_p = _merge(x_prompt, sb_p, ro_p, ret_p, qm_p, mk_p, mv_p, g_sb_out[l], g_ret_out[l], g_mem_out[l],
                  wo_bf, hd, _row_tile(tp, 1024))

    q_s, k_s, v_s, kb_s, vb_s, ret_s, qm_s = _stream(x_sample, w_in_bf, g_mix[l], dims)
    sb_s = _sb_sample(q_s, kb_s, vb_s, cache_sb_k[l].reshape(bs, past, wsb), cache_sb_v[l].reshape(bs, past, wsb), hd)
    s0 = _block_diag_state(state_ret[l].astype(F32), ret_heads, hd)
    ro_s, st_s = _retention(ret_s, past + jnp.arange(ts, dtype=I32), s0, ret_heads, hd, _row_tile(ts, 512))
    x1_s = _merge(x_sample, sb_s, ro_s, ret_s, qm_s, cache_mem_k[l].reshape(bs, nmem, wmm),
                  cache_mem_v[l].reshape(bs, nmem, wmm), g_sb_out[l], g_ret_out[l], g_mem_out[l],
                  wo_bf, hd, _row_tile(ts, 256))

    wq_bf = w_peer_q[l].astype(BF16)
    sk_bf = peer_sub_keys[l].astype(BF16)
    u_packed = _pack_table(peer_u[l])
    v_packed = _pack_table(peer_v[l])
    y_p = _peer_and_final(x1_p.reshape(bp * tp, d), g_ffn[l], wq_bf, sk_bf, u_packed, v_packed, g_final)
    y_s = _peer_and_final(x1_s.reshape(bs * ts, d), g_ffn[l], wq_bf, sk_bf, u_packed, v_packed, g_final)

    return (y_p.reshape(bp, tp, d), y_s.reshape(bs, ts, d),
            k_p.reshape(1, bp, tp, sb_heads, hd), v_p.reshape(1, bp, tp, sb_heads, hd),
            _diag_blocks(st_p, ret_heads, hd)[None],
            mk_p.reshape(1, bp, nmem, mem_heads, hd), mv_p.reshape(1, bp, nmem, mem_heads, hd),
            k_s.reshape(1, bs, ts, sb_heads, hd), v_s.reshape(1, bs, ts, sb_heads, hd),
            _diag_blocks(st_s, ret_heads, hd)[None].astype(state_ret.dtype))
```

```python
import functools

import numpy as np
import jax
import jax.numpy as jnp
from jax import lax
from jax.experimental import pallas as pl
from jax.experimental.pallas import tpu as pltpu

F32 = jnp.float32
BF16 = jnp.bfloat16
I32 = jnp.int32

EPS = 1e-6
ROPE_BASE = 10000.0
PEER_TOPK = 16
RET_CHUNK = 64
LANES = 128
ROW_TILES = 8
HALF_ROWS = 4
VMEM_LIMIT = 56 * 1024 * 1024
SB_DEAD_LOG = -105.0


def _cparams(*sem):
    return pltpu.CompilerParams(dimension_semantics=sem, vmem_limit_bytes=VMEM_LIMIT)


def _dot(a, b):
    return jnp.dot(a, b, preferred_element_type=F32)


def _dot_nt(a, b):
    return lax.dot_general(a, b, (((1,), (1,)), ((), ())), preferred_element_type=F32)


def _split_bf16(x):
    hi = x.astype(BF16)
    lo = (x - hi.astype(F32)).astype(BF16)
    return hi, lo


def _dot_split(x, m_bf16):
    hi, lo = _split_bf16(x)
    return _dot(hi, m_bf16) + _dot(lo, m_bf16)


def _norm_proj_kernel(x_ref, g_ref, w_ref, *out_refs, cols):
    x = x_ref[...]
    ms = jnp.mean(x * x, axis=-1, keepdims=True)
    h = x * lax.rsqrt(ms + EPS) * g_ref[...]
    p = _dot(h.astype(BF16), w_ref[...])
    tm = x.shape[0]
    for ref, (lo, hi, scale, heads) in zip(out_refs, cols):
        v = p[:, lo:hi]
        if scale != 1.0:
            v = v * scale
        if heads:
            hd = (hi - lo) // heads
            for h in range(heads):
                ref[pl.ds(h, tm, stride=heads), :] = v[:, h * hd:(h + 1) * hd].astype(ref.dtype)
        else:
            ref[...] = v.astype(ref.dtype)


def _norm_proj(x, g, w_bf16, outs, tm):
    n, d = x.shape
    wcols = w_bf16.shape[1]
    shape = lambda rows, lo, hi, hds: (rows * hds, (hi - lo) // hds) if hds else (rows, hi - lo)
    out_shape = [jax.ShapeDtypeStruct(shape(n, lo, hi, hds), dt) for lo, hi, _, dt, hds in outs]
    out_specs = [pl.BlockSpec(shape(tm, lo, hi, hds), lambda i: (i, 0)) for lo, hi, _, _, hds in outs]
    return pl.pallas_call(
        functools.partial(_norm_proj_kernel, cols=tuple((lo, hi, sc, hds) for lo, hi, sc, _, hds in outs)),
        grid=(n // tm,),
        in_specs=[pl.BlockSpec((tm, d), lambda i: (i, 0)),
                  pl.BlockSpec((1, d), lambda i: (0, 0)),
                  pl.BlockSpec((d, wcols), lambda i: (0, 0))],
        out_specs=out_specs,
        out_shape=out_shape,
        compiler_params=_cparams("parallel"),
        name="norm_proj",
    )(x, g.reshape(1, d), w_bf16)


def _suffix_matrix(tk):
    j = np.arange(2 * tk)[:, None]
    s = np.arange(2 * tk)[None, :]
    return jnp.asarray(((j // tk == s // tk) & (j >= s)).astype(np.float32), dtype=BF16)


def _sb_step(q2, kb, vb, u, carry, tk, hd, valid):
    c0, c1, acc = carry
    tq = q2.shape[0]
    lane = lax.broadcasted_iota(I32, kb.shape, 1)
    first = lane < hd
    zero = jnp.zeros_like(kb)
    kk = jnp.concatenate([jnp.where(first, kb, zero), jnp.where(first, zero, kb)], axis=0)
    vv = jnp.concatenate([jnp.where(first, vb, zero), jnp.where(first, zero, vb)], axis=0)
    z = _dot_nt(q2, kk)
    sp = jnp.log(1.0 + jnp.exp(-jnp.abs(z)))
    lk = jnp.minimum(-z, 0.0) - sp
    if valid is not None:
        lk = jnp.where(valid, lk, 0.0)
    m = _dot_split(lk, u)
    c = jnp.concatenate([jnp.broadcast_to(c0, (tq, tk)), jnp.broadcast_to(c1, (tq, tk))], axis=1)
    a = jnp.exp(z + c + m)
    if valid is not None:
        a = jnp.where(valid, a, 0.0)
    acc = acc + _dot(a.astype(BF16), vv)
    c0 = c0 + jnp.sum(lk[:, :tk], axis=-1, keepdims=True)
    c1 = c1 + jnp.sum(lk[:, tk:], axis=-1, keepdims=True)
    return c0, c1, acc


def _sb_alive(carries):
    m = functools.reduce(jnp.maximum, [jnp.maximum(c0, c1) for c0, c1, _ in carries])
    return jnp.max(m) > SB_DEAD_LOG


def _sb_far_blocks(qs, block, nblk, u, carries, tk, hd, first=0):
    n = len(qs)
    flatten = lambda cs: tuple(x for c in cs for x in c)
    unflatten = lambda flat: [tuple(flat[3 * p:3 * p + 3]) for p in range(n)]

    def cond(state):
        return jnp.logical_and(state[0] < nblk, state[1])

    def body(state):
        i = state[0]
        kvs = block(i)
        cs = [_sb_step(qs[p], kvs[p][0], kvs[p][1], u, c, tk, hd, None) for p, c in enumerate(unflatten(state[2:]))]
        return (i + 1, _sb_alive(cs)) + flatten(cs)

    state = lax.while_loop(cond, body, (jnp.int32(first), _sb_alive(carries)) + flatten(carries))
    return unflatten(state[2:])


SB_PAIRS_PER_STEP = 4


def _sb_prompt_kernel(q_ref, k_ref, v_ref, u_ref, o_ref, *, tq, tk, hd, npairs):
    q0 = pl.program_id(2) * tq
    lanes = [slice(p * LANES, (p + 1) * LANES) for p in range(npairs)]
    qs = [q_ref[0, :, ln] for ln in lanes]
    u = u_ref[...]
    row = lax.broadcasted_iota(I32, (tq, 2 * tk), 0)
    koff = lax.broadcasted_iota(I32, (tq, 2 * tk), 1) & (tk - 1)
    carries = [(jnp.zeros((tq, 1), F32), jnp.zeros((tq, 1), F32), jnp.zeros((tq, LANES), F32)) for _ in lanes]
    nd = tq // tk
    for d in range(nd + 1):
        off = (nd - 1 - d) * tk
        limit = row if d < nd else row + jnp.where(q0 >= tk, 0, -4 * tq)
        valid = (koff + off) < limit
        ks = pl.multiple_of(jnp.maximum(q0 + off, 0), tk)
        carries = [_sb_step(qs[p], k_ref[0, pl.ds(ks, tk), ln], v_ref[0, pl.ds(ks, tk), ln], u, carries[p], tk, hd, valid)
                   for p, ln in enumerate(lanes)]

    def block(i):
        ks = pl.multiple_of(q0 - (i + 1) * tk, tk)
        return [(k_ref[0, pl.ds(ks, tk), ln], v_ref[0, pl.ds(ks, tk), ln]) for ln in lanes]

    carries = _sb_far_blocks(qs, block, q0 // tk, u, carries, tk, hd, first=1)
    for p, ln in enumerate(lanes):
        o_ref[0, :, ln] = carries[p][2]


def _sb_prompt(q, k, v, hd, tq=256, tk=128):
    b, t, w = q.shape
    npairs = SB_PAIRS_PER_STEP
    wb = npairs * LANES
    assert w % wb == 0
    return pl.pallas_call(
        functools.partial(_sb_prompt_kernel, tq=tq, tk=tk, hd=hd, npairs=npairs),
        grid=(b, w // wb, t // tq),
        in_specs=[pl.BlockSpec((1, tq, wb), lambda bi, hp, qi: (bi, qi, hp)),
                  pl.BlockSpec((1, t, wb), lambda bi, hp, qi: (bi, 0, hp)),
                  pl.BlockSpec((1, t, wb), lambda bi, hp, qi: (bi, 0, hp)),
                  pl.BlockSpec((2 * tk, 2 * tk), lambda bi, hp, qi: (0, 0))],
        out_specs=pl.BlockSpec((1, tq, wb), lambda bi, hp, qi: (bi, qi, hp)),
        out_shape=jax.ShapeDtypeStruct((b, t, w), F32),
        compiler_params=_cparams("parallel", "parallel", "arbitrary"),
        name="sb_prompt",
    )(q, k, v, _suffix_matrix(tk))


def _sb_sample_kernel(q_ref, kn_ref, vn_ref, kc_ref, vc_ref, un_ref, u_ref, o_ref, *, ts, tk, hd, past):
    q2 = q_ref[0]
    row = lax.broadcasted_iota(I32, (ts, 2 * ts), 0)
    koff = lax.broadcasted_iota(I32, (ts, 2 * ts), 1) & (ts - 1)
    carry = (jnp.zeros((ts, 1), F32), jnp.zeros((ts, 1), F32), jnp.zeros((ts, LANES), F32))
    carry = _sb_step(q2, kn_ref[0], vn_ref[0], un_ref[...], carry, ts, hd, koff < row)
    u = u_ref[...]

    def block(i):
        ks = pl.multiple_of(past - (i + 1) * tk, tk)
        return kc_ref[0, pl.ds(ks, tk), :].astype(BF16), vc_ref[0, pl.ds(ks, tk), :].astype(BF16)

    (carry,) = _sb_far_blocks([q2], lambda i: [block(i)], past // tk, u, [carry], tk, hd)
    o_ref[0] = carry[2]


def _sb_sample(q, kn, vn, kc, vc, hd, tk=128):
    b, ts, w = q.shape
    past = kc.shape[1]
    new_spec = pl.BlockSpec((1, ts, LANES), lambda bi, hp: (bi, 0, hp))
    cache_spec = pl.BlockSpec((1, past, LANES), lambda bi, hp: (bi, 0, hp))
    return pl.pallas_call(
        functools.partial(_sb_sample_kernel, ts=ts, tk=tk, hd=hd, past=past),
        grid=(b, w // LANES),
        in_specs=[new_spec, new_spec, new_spec, cache_spec, cache_spec,
                  pl.BlockSpec((2 * ts, 2 * ts), lambda bi, hp: (0, 0)),
                  pl.BlockSpec((2 * tk, 2 * tk), lambda bi, hp: (0, 0))],
        out_specs=new_spec,
        out_shape=jax.ShapeDtypeStruct((b, ts, w), F32),
        compiler_params=_cparams("parallel", "parallel"),
        name="sb_sample",
    )(q, kn, vn, kc, vc, _suffix_matrix(ts), _suffix_matrix(tk))


def _ret_kernel(q_ref, k_ref, v_ref, cos_ref, sin_ref, dec_ref, qd_ref, kd_ref, gl_ref, bd_ref, s0_ref,
                o_ref, st_ref, s_scr, *, tc, nh, hd, k_scale):
    t = pl.program_id(1)
    w = nh * hd
    L = RET_CHUNK

    @pl.when(t == 0)
    def _():
        s_scr[...] = s0_ref[0]

    lane = lax.broadcasted_iota(I32, (L, w), 1)
    first_half = (lane & (hd - 1)) < hd // 2
    head = lane // hd

    def rope(x, cos, sin):
        swapped = jnp.where(first_half, pltpu.roll(x, w - hd // 2, 1), pltpu.roll(x, hd // 2, 1))
        return x * cos + swapped * sin

    s = s_scr[...]
    for ci in range(tc // L):
        sl = slice(ci * L, (ci + 1) * L)
        cos = cos_ref[sl, :]
        sin = sin_ref[sl, :]
        q = rope(q_ref[0, sl, :], cos, sin)
        k = rope(k_ref[0, sl, :], cos, sin) * k_scale
        qb = q.astype(BF16)
        kb = k.astype(BF16)
        vb = v_ref[0, sl, :].astype(BF16)
        zero = jnp.zeros_like(qb)
        out = _dot(qb, s.astype(BF16)) * qd_ref[...]
        for h in range(nh):
            mh = head == h
            sc = _dot_nt(jnp.where(mh, qb, zero), kb)
            p = (sc * dec_ref[h]).astype(BF16)
            out = out + _dot(p, jnp.where(mh, vb, zero))
        o_ref[0, sl, :] = out
        kd = (k * kd_ref[...]).astype(BF16)
        outer = lax.dot_general(kd, vb, (((0,), (0,)), ((), ())), preferred_element_type=F32)
        s = gl_ref[...] * s + bd_ref[...] * outer
    s_scr[...] = s

    @pl.when(t == pl.num_programs(1) - 1)
    def _():
        st_ref[0] = s


def _retention(ret, pos, s0, nh, hd, tc):
    b, t, _ = ret.shape
    w = nh * hd
    L = RET_CHUNK
    half = hd // 2
    freqs = ROPE_BASE ** (-jnp.arange(half, dtype=F32) / half)
    ang = pos.astype(F32)[:, None] * freqs[None, :]
    cos = jnp.tile(jnp.cos(ang), (1, 2 * nh))
    sin = jnp.tile(jnp.concatenate([-jnp.sin(ang), jnp.sin(ang)], axis=1), (1, nh))
    lg = jnp.log1p(-jnp.exp2(-5.0 - jnp.arange(nh, dtype=F32)))
    idx = jnp.arange(L, dtype=F32)
    diff = idx[:, None] - idx[None, :]
    causal = diff >= 0
    dec = jnp.where(causal[None], jnp.exp(jnp.where(causal, diff, 0.0)[None] * lg[:, None, None]), 0.0)
    lg_lane = jnp.repeat(lg, hd)[None, :]
    qd = jnp.exp((idx + 1.0)[:, None] * lg_lane)
    kd = jnp.exp((L - 1.0 - idx)[:, None] * lg_lane)
    gl = jnp.exp(L * lg_lane)
    hid = jnp.arange(w) // hd
    bd = (hid[:, None] == hid[None, :]).astype(F32)
    tok = lambda j: pl.BlockSpec((1, tc, w), lambda bi, ti: (bi, ti, j))
    tab = pl.BlockSpec((tc, w), lambda bi, ti: (ti, 0))
    const = lambda shape: pl.BlockSpec(shape, lambda bi, ti: (0,) * len(shape))
    state = pl.BlockSpec((1, w, w), lambda bi, ti: (bi, 0, 0))
    return pl.pallas_call(
        functools.partial(_ret_kernel, tc=tc, nh=nh, hd=hd, k_scale=hd ** -0.5),
        grid=(b, t // tc),
        in_specs=[tok(0), tok(1), tok(2), tab, tab, const((nh, L, L)), const((L, w)), const((L, w)),
                  const((1, w)), const((w, w)), state],
        out_specs=[pl.BlockSpec((1, tc, w), lambda bi, ti: (bi, ti, 0)), state],
        out_shape=[jax.ShapeDtypeStruct((b, t, w), F32), jax.ShapeDtypeStruct((b, w, w), F32)],
        scratch_shapes=[pltpu.VMEM((w, w), F32)],
        compiler_params=_cparams("parallel", "arbitrary"),
        name="retention",
    )(ret, ret, ret, cos, sin, dec, qd, kd, gl, bd, s0)


def _block_diag_state(st, nh, hd):
    b = st.shape[0]
    eye = jnp.eye(nh, dtype=st.dtype)
    return jnp.einsum('bhde,hg->bhdge', st, eye).reshape(b, nh * hd, nh * hd)


def _diag_blocks(s, nh, hd):
    b = s.shape[0]
    s5 = s.reshape(b, nh, hd, nh, hd)
    return jnp.stack([s5[:, h, :, h, :] for h in range(nh)], axis=1)


def _head_mean_matrix(w, hd):
    hid = np.arange(w) // hd
    return jnp.asarray((hid[:, None] == hid[None, :]).astype(np.float32) / hd, dtype=BF16)


def _headnorm(x, g_mat, gain):
    ms = _dot_split(x * x, g_mat)
    return x * lax.rsqrt(ms + EPS) * gain


def _merge_kernel(x_ref, sb_ref, ret_ref, gate_ref, qm_ref, mk_ref, mv_ref, gsb_ref, gret_ref, gmo_ref,
                  msb_ref, mrt_ref, wo_ref, x1_ref, *, nh_mem, hd):
    wsb = sb_ref.shape[2]
    wrt = ret_ref.shape[2]
    qm = qm_ref[0]
    mkb = mk_ref[0].astype(BF16)
    mvb = mv_ref[0].astype(BF16)
    head_q = lax.broadcasted_iota(I32, qm.shape, 1) // hd
    head_m = lax.broadcasted_iota(I32, mkb.shape, 1) // hd
    mem_o = jnp.zeros(qm.shape, F32)
    for h in range(nh_mem):
        s = _dot_nt(jnp.where(head_q == h, qm, jnp.zeros_like(qm)), mkb)
        e = jnp.exp(s - jnp.max(s, axis=-1, keepdims=True))
        p = e / jnp.sum(e, axis=-1, keepdims=True)
        mem_o = mem_o + _dot(p.astype(BF16), jnp.where(head_m == h, mvb, jnp.zeros_like(mvb)))
    sb_n = _headnorm(sb_ref[0], msb_ref[...], gsb_ref[...])
    gate = gate_ref[0]
    ret_n = _headnorm(ret_ref[0], mrt_ref[...], gret_ref[...]) * (gate / (1.0 + jnp.exp(-gate)))
    mem_n = _headnorm(mem_o, mrt_ref[...], gmo_ref[...])
    y = (_dot(sb_n.astype(BF16), wo_ref[0:wsb, :])
         + _dot(ret_n.astype(BF16), wo_ref[wsb:wsb + wrt, :])
         + _dot(mem_n.astype(BF16), wo_ref[wsb + wrt:, :]))
    x1_ref[0] = x_ref[0] + y


def _merge(x, sb_o, ret_o, ret, qm, mk, mv, g_sb, g_ret, g_mo, wo_bf16, hd, tm):
    b, t, d = x.shape
    wsb = sb_o.shape[2]
    wrt = ret_o.shape[2]
    nmem = mk.shape[1]
    tok = lambda wd, j=0: pl.BlockSpec((1, tm, wd), lambda bi, ti: (bi, ti, j))
    const = lambda shape: pl.BlockSpec(shape, lambda bi, ti: (0,) * len(shape))
    memspec = pl.BlockSpec((1, nmem, wrt), lambda bi, ti: (bi, 0, 0))
    return pl.pallas_call(
        functools.partial(_merge_kernel, nh_mem=wrt // hd, hd=hd),
        grid=(b, t // tm),
        in_specs=[tok(d), tok(wsb), tok(wrt), tok(wrt, 3), tok(wrt), memspec, memspec,
                  const((1, wsb)), const((1, wrt)), const((1, wrt)),
                  const((wsb, wsb)), const((wrt, wrt)), const((d, d))],
        out_specs=tok(d),
        out_shape=jax.ShapeDtypeStruct((b, t, d), F32),
        compiler_params=_cparams("parallel", "parallel"),
        name="merge",
    )(x, sb_o, ret_o, ret, qm, mk, mv, g_sb.reshape(1, wsb), g_ret.reshape(1, wrt), g_mo.reshape(1, wrt),
      _head_mean_matrix(wsb, hd), _head_mean_matrix(wrt, hd), wo_bf16)


def _topk_rows(s, k, rid=None):
    if rid is None:
        rid = lax.broadcasted_iota(I32, s.shape, 0)
    big = jnp.int32(2 ** 30)
    vals, idxs = [], []
    for _ in range(k):
        m = jnp.max(s, axis=0, keepdims=True)
        i = jnp.min(jnp.where(s == m, rid, big), axis=0, keepdims=True)
        vals.append(m)
        idxs.append(i)
        s = jnp.where(rid == i, -jnp.inf, s)
    return jnp.concatenate(vals, axis=0), jnp.concatenate(idxs, axis=0)


def _staircase_blocks(k, rows=8):
    blocks = []
    singles = [a for a in range(k) if k // (a + 1) == 1]
    for a in range(k):
        nb = k // (a + 1)
        if nb > 1:
            blocks.extend(('row', a, b0) for b0 in range(0, nb, rows))
    assert singles and singles[0] % rows == 0 and len(singles) % rows == 0 and singles[-1] == k - 1
    blocks.extend(('col', a0, 0) for a0 in range(singles[0], k, rows))
    flat = []
    for kind, a, b in blocks:
        flat.extend([a * k + b + i for i in range(rows)] if kind == 'row' else [(a + i) * k + b for i in range(rows)])
    need = {(a, b) for a in range(k) for b in range(k) if (a + 1) * (b + 1) <= k}
    assert need <= {(f // k, f % k) for f in flat} and len(set(flat)) == len(flat)
    return blocks, np.array(flat, np.int32)[:, None]


def _take_rows(tbl, idx):
    out = jnp.zeros(idx.shape, tbl.dtype)
    for a in range(tbl.shape[0]):
        out = out + jnp.where(idx == a, tbl[a:a + 1, :], 0)
    return out


def _peer_score_kernel(x_ref, g_ref, wq_ref, sk_ref, cf_ref, h_ref, eid_ref, gw_ref, hb_scr,
                       eid_scr, gw_scr, sc_scr, *, topk, nkeys, blocks):
    head = pl.program_id(1)

    @pl.when(head == 0)
    def _():
        x = x_ref[...]
        ms = jnp.mean(x * x, axis=-1, keepdims=True)
        hn = x * lax.rsqrt(ms + EPS) * g_ref[...]
        for r in range(ROW_TILES):
            h_ref[pl.ds(r, hn.shape[0], stride=ROW_TILES), :] = hn[:, r * LANES:(r + 1) * LANES]
        hb_scr[...] = hn.astype(BF16)

    qb = _dot(hb_scr[...], wq_ref[...]).astype(BF16)
    hq = qb.shape[1] // 2
    sc_scr[0] = _dot_nt(sk_ref[0], qb[:, :hq])
    sc_scr[1] = _dot_nt(sk_ref[1], qb[:, hq:])
    brows = cf_ref.shape[0] // len(blocks)
    out_rows = pl.ds(pl.multiple_of(head * topk, topk), topk)
    width = min(LANES, qb.shape[0])

    def column_group(c):
        cols = slice(c * width, (c + 1) * width)
        sv0, si0 = _topk_rows(sc_scr[0, :, cols], topk)
        sv1, si1 = _topk_rows(sc_scr[1, :, cols], topk)
        cand = jnp.concatenate(
            [sv0[a:a + 1, :] + sv1[b:b + brows, :] if kind == 'row' else sv0[a:a + brows, :] + sv1[b:b + 1, :]
             for kind, a, b in blocks], axis=0)
        cv, ci = _topk_rows(cand, topk, cf_ref[:, cols])
        e1 = _take_rows(si0, ci // topk)
        e2 = _take_rows(si1, ci % topk)
        eid_scr[out_rows, cols] = ((e1 * nkeys + e2) * HALF_ROWS).astype(F32)
        e = jnp.exp(cv - cv[0:1, :])
        gw_scr[out_rows, cols] = e / jnp.sum(e, axis=0, keepdims=True)

    for c in range(qb.shape[0] // width):
        column_group(c)

    @pl.when(head == pl.num_programs(1) - 1)
    def _():
        eid_ref[...] = eid_scr[...].T.astype(I32)
        gw_ref[...] = gw_scr[...].T


def _peer_score(x, g, wq_bf16, sk_bf16, tm):
    n, d = x.shape
    nkeys, khalf = sk_bf16.shape[1], sk_bf16.shape[2]
    nheads = wq_bf16.shape[1] // (2 * khalf)
    k = PEER_TOPK
    blocks, flat = _staircase_blocks(k)
    cf = jnp.asarray(np.broadcast_to(flat, (flat.shape[0], tm)))
    xs = pl.BlockSpec((tm, d), lambda i, h: (i, 0))
    cs = pl.BlockSpec(cf.shape, lambda i, h: (0, 0))
    return pl.pallas_call(
        functools.partial(_peer_score_kernel, topk=k, nkeys=nkeys, blocks=tuple(blocks)),
        grid=(n // tm, nheads),
        in_specs=[xs, pl.BlockSpec((1, d), lambda i, h: (0, 0)),
                  pl.BlockSpec((d, 2 * khalf), lambda i, h: (0, h)),
                  pl.BlockSpec((2, nkeys, khalf), lambda i, h: (0, 0, 0)), cs],
        out_specs=[pl.BlockSpec((tm * ROW_TILES, LANES), lambda i, h: (i, 0)),
                   pl.BlockSpec((tm, nheads * k), lambda i, h: (i, 0)),
                   pl.BlockSpec((tm, nheads * k), lambda i, h: (i, 0))],
        out_shape=[jax.ShapeDtypeStruct((n * ROW_TILES, LANES), F32),
                   jax.ShapeDtypeStruct((n, nheads * k), I32), jax.ShapeDtypeStruct((n, nheads * k), F32)],
        scratch_shapes=[pltpu.VMEM((tm, d), BF16), pltpu.VMEM((nheads * k, tm), F32),
                        pltpu.VMEM((nheads * k, tm), F32), pltpu.VMEM((2, nkeys, tm), F32)],
        compiler_params=_cparams("parallel", "arbitrary"),
        name="peer_score",
    )(x, g.reshape(1, d), wq_bf16, sk_bf16, cf)


def _pack_table(tab):
    e, d = tab.shape
    assert d == ROW_TILES * LANES
    te = _row_tile(e, 512)
    return pl.pallas_call(
        _pack_kernel,
        grid=(e // te,),
        in_specs=[pl.BlockSpec((te, d), lambda i: (i, 0))],
        out_specs=pl.BlockSpec((te * HALF_ROWS, LANES), lambda i: (i, 0)),
        out_shape=jax.ShapeDtypeStruct((e * HALF_ROWS, LANES), I32),
        compiler_params=_cparams("parallel"),
        name="pack_table",
    )(tab)


def _pack_words(even, odd):
    odd = lax.bitcast_convert_type(odd, I32)
    low = lax.shift_right_logical(lax.bitcast_convert_type(even.astype(BF16).astype(F32), I32), 16)
    sign = odd & jnp.int32(-2 ** 31)
    mag = (odd & jnp.int32(2 ** 31 - 1)) + jnp.int32(0x8000)
    high = jnp.where(mag >= low, (mag - low) >> 16, 0)
    return sign | (high << 16) | low


def _pack_kernel(t_ref, o_ref):
    te = t_ref.shape[0]
    x = t_ref[...]
    for s in range(HALF_ROWS):
        o_ref[pl.ds(s, te, stride=HALF_ROWS), :] = _pack_words(x[:, (2 * s) * LANES:(2 * s + 1) * LANES],
                                                               x[:, (2 * s + 1) * LANES:(2 * s + 2) * LANES])


def _unpack(wd):
    even = lax.bitcast_convert_type(wd << 16, F32)
    odd = lax.bitcast_convert_type(wd, F32)
    return even, odd


def _erf(x):
    return lax.erf(x)


def _token_tiles(ref, t):
    base = pl.multiple_of(t * ROW_TILES, ROW_TILES)
    return ref[pl.ds(base, HALF_ROWS, stride=2), :], ref[pl.ds(base + 1, HALF_ROWS, stride=2), :]


PEER_U_SET = 4


def _peer_u_kernel(eid_ref, h_ref, gw_ref, tab_ref, w_ref, pa, pb, q_scr, act_scr, *, tg, npairs):
    ones = jnp.ones((LANES, LANES), BF16)
    row = lax.broadcasted_iota(I32, (npairs, LANES), 0)
    eye = (row == lax.broadcasted_iota(I32, (npairs, LANES), 1)).astype(F32)
    ns = PEER_U_SET

    def products(t, p_scr, qset, k):
        he, ho = _token_tiles(h_ref, t)
        he2 = jnp.concatenate([he, he], axis=0)
        ho2 = jnp.concatenate([ho, ho], axis=0)
        for j in range(0, npairs, 2):
            wa = tab_ref[pl.ds(pl.multiple_of(eid_ref[t, j], HALF_ROWS), HALF_ROWS), :]
            wb = tab_ref[pl.ds(pl.multiple_of(eid_ref[t, j + 1], HALF_ROWS), HALF_ROWS), :]
            even, odd = _unpack(jnp.concatenate([wa, wb], axis=0))
            p_scr[j * HALF_ROWS:(j + 2) * HALF_ROWS, :] = even * he2 + odd * ho2
        q = p_scr[pl.ds(0, npairs, stride=HALF_ROWS), :]
        for s in range(1, HALF_ROWS):
            q = q + p_scr[pl.ds(s, npairs, stride=HALF_ROWS), :]
        hi, lo = _split_bf16(q)
        q_scr[qset, 2 * k * npairs:(2 * k + 1) * npairs, :] = hi
        q_scr[qset, (2 * k + 1) * npairs:(2 * k + 2) * npairs, :] = lo

    def lane_sums(qset, tokens):
        r = _dot(q_scr[qset], ones)
        for k, t in enumerate(tokens):
            rk = r[2 * k * npairs:(2 * k + 1) * npairs, :] + r[(2 * k + 1) * npairs:(2 * k + 2) * npairs, :]
            act_scr[pl.ds(t, 1), :] = jnp.sum(rk * eye, axis=0, keepdims=True)

    q_scr[1] = jnp.zeros(q_scr.shape[1:], BF16)

    def body(i, carry):
        t0 = 2 * ns * i
        lane_sums(1, [jnp.maximum(t0 - ns + k, 0) for k in range(ns)])
        for k in range(ns):
            products(t0 + k, (pa, pb)[k % 2], 0, k)
        lane_sums(0, [t0 + k for k in range(ns)])
        for k in range(ns):
            products(t0 + ns + k, (pa, pb)[k % 2], 1, k)
        return carry

    lax.fori_loop(0, tg // (2 * ns), body, 0)
    lane_sums(1, [tg - ns + k for k in range(ns)])
    act = act_scr[...]
    gelu = 0.5 * act * (1.0 + _erf(act * (2.0 ** -0.5)))
    w = gw_ref[...] * gelu
    w_ref[...] = _pack_words(w, pltpu.roll(w, npairs - 1, 1))


def _peer_u(eid, h, gw, tab_packed, tg):
    n, npairs = eid.shape
    assert npairs == LANES and h.shape == (n * ROW_TILES, LANES) and tg % (2 * PEER_U_SET) == 0
    tspec = pl.BlockSpec((tg, npairs), lambda i: (i, 0))
    pshape = pltpu.VMEM((npairs * HALF_ROWS, LANES), F32)
    qshape = pltpu.VMEM((2, 2 * PEER_U_SET * npairs, LANES), BF16)
    return pl.pallas_call(
        functools.partial(_peer_u_kernel, tg=tg, npairs=npairs),
        grid=(n // tg,),
        in_specs=[pl.BlockSpec((tg, npairs), lambda i: (i, 0), memory_space=pltpu.SMEM),
                  pl.BlockSpec((tg * ROW_TILES, LANES), lambda i: (i, 0)), tspec,
                  pl.BlockSpec(memory_space=pltpu.VMEM)],
        out_specs=tspec,
        out_shape=jax.ShapeDtypeStruct((n, npairs), I32),
        scratch_shapes=[pshape, pshape, qshape, pltpu.VMEM((tg, npairs), F32)],
        compiler_params=_cparams("arbitrary"),
        name="peer_u",
    )(eid, h, gw, tab_packed)


PEER_V_ACCUMULATORS = 2


def _peer_v_kernel(eid_ref, w_ref, tab_ref, o_ref, *, tg, npairs):
    nacc = PEER_V_ACCUMULATORS

    def token(t):
        acc_e = [jnp.zeros((HALF_ROWS, LANES), F32) for _ in range(nacc)]
        acc_o = [jnp.zeros((HALF_ROWS, LANES), F32) for _ in range(nacc)]
        for j in range(npairs):
            if j % 2 == 0:
                wpair = _unpack(jnp.full((HALF_ROWS, LANES), w_ref[t, j], I32))
            wt = wpair[j % 2]
            e4 = pl.multiple_of(eid_ref[t, j], HALF_ROWS)
            even, odd = _unpack(tab_ref[pl.ds(e4, HALF_ROWS), :])
            acc_e[j % nacc] = acc_e[j % nacc] + wt * even
            acc_o[j % nacc] = acc_o[j % nacc] + wt * odd
        base = pl.multiple_of(t * ROW_TILES, ROW_TILES)
        o_ref[pl.ds(base, HALF_ROWS, stride=2), :] = functools.reduce(lambda a, b: a + b, acc_e)
        o_ref[pl.ds(base + 1, HALF_ROWS, stride=2), :] = functools.reduce(lambda a, b: a + b, acc_o)

    def body(i, carry):
        token(2 * i)
        token(2 * i + 1)
        return carry

    lax.fori_loop(0, tg // 2, body, 0)


def _peer_v(eid, w, tab_packed, tg):
    n, npairs = eid.shape
    assert tg % 2 == 0
    sspec = pl.BlockSpec((tg, npairs), lambda i: (i, 0), memory_space=pltpu.SMEM)
    return pl.pallas_call(
        functools.partial(_peer_v_kernel, tg=tg, npairs=npairs),
        grid=(n // tg,),
        in_specs=[sspec, sspec, pl.BlockSpec(memory_space=pltpu.VMEM)],
        out_specs=pl.BlockSpec((tg * ROW_TILES, LANES), lambda i: (i, 0)),
        out_shape=jax.ShapeDtypeStruct((n * ROW_TILES, LANES), F32),
        compiler_params=_cparams("arbitrary"),
        name="peer_v",
    )(eid, w, tab_packed)


def _final_kernel(x_ref, p_ref, g_ref, y_ref):
    tm = x_ref.shape[0]
    p = jnp.concatenate([p_ref[pl.ds(r, tm, stride=ROW_TILES), :] for r in range(ROW_TILES)], axis=1)
    x = x_ref[...] + p
    ms = jnp.mean(x * x, axis=-1, keepdims=True)
    y_ref[...] = x * lax.rsqrt(ms + EPS) * g_ref[...]


def _final(x, p, g, tm):
    n, d = x.shape
    xs = pl.BlockSpec((tm, d), lambda i: (i, 0))
    return pl.pallas_call(
        _final_kernel,
        grid=(n // tm,),
        in_specs=[xs, pl.BlockSpec((tm * ROW_TILES, LANES), lambda i: (i, 0)), pl.BlockSpec((1, d), lambda i: (0, 0))],
        out_specs=xs,
        out_shape=jax.ShapeDtypeStruct((n, d), F32),
        compiler_params=_cparams("parallel"),
        name="final_norm",
    )(x, p, g.reshape(1, d))


def _row_tile(n, pref):
    t = min(pref, n)
    assert n % t == 0, (n, t)
    return t


def _stream(x, w_in_bf, g_mix, dims):
    b, t, d = x.shape
    wsb, wrt, wmm, hd = dims
    scale = hd ** -0.5
    o = 3 * wsb
    outs = [(0, wsb, scale, BF16, 0),
            (wsb, 2 * wsb, 1.0, F32, wsb // hd), (2 * wsb, o, 1.0, F32, wsb // hd),
            (wsb, 2 * wsb, 1.0, BF16, 0), (2 * wsb, o, 1.0, BF16, 0),
            (o, o + 4 * wrt, 1.0, F32, 0),
            (o + 4 * wrt, o + 4 * wrt + wmm, scale, BF16, 0)]
    res = _norm_proj(x.reshape(b * t, d), g_mix, w_in_bf, outs, _row_tile(b * t, 512))
    return [r.reshape(b, t, r.shape[1]) if r.shape[0] == b * t else r for r in res]


def _peer_and_final(x1, g_ffn, wq_bf, sk_bf, u_packed, v_packed, g_final):
    n = x1.shape[0]
    h, eid, gw = _peer_score(x1, g_ffn, wq_bf, sk_bf, _row_tile(n, 1024))
    tg = _row_tile(n, 128)
    w = _peer_u(eid, h, gw, u_packed, tg)
    p = _peer_v(eid, w, v_packed, tg)
    return _final(x1, p, g_final, _row_tile(n, 512))


def kernel(x_prompt, x_sample, mem_prompt, cache_sb_k, cache_sb_v, state_ret, cache_mem_k, cache_mem_v,
           g_mix, w_in, w_out, g_sb_out, g_ret_out, g_mem_out, g_mem, w_mem_kv,
           g_ffn, w_peer_q, peer_sub_keys, peer_u, peer_v, g_final):
    depth = w_in.shape[0]
    assert depth == 1, "single-layer step"
    bp, tp, d = x_prompt.shape
    bs, ts, _ = x_sample.shape
    _, _, past, sb_heads, hd = cache_sb_k.shape
    ret_heads = state_ret.shape[2]
    mem_heads = cache_mem_k.shape[3]
    nmem = mem_prompt.shape[1]
    wsb, wrt, wmm = sb_heads * hd, ret_heads * hd, mem_heads * hd
    assert hd == 64 and wrt == wmm and d == 2 * HALF_ROWS * LANES
    l = 0
    w_in_bf = w_in[l].astype(BF16)
    wo_bf = w_out[l].astype(BF16)
    dims = (wsb, wrt, wmm, hd)

    q_p, k_p, v_p, kb_p, vb_p, ret_p, qm_p = _stream(x_prompt, w_in_bf, g_mix[l], dims)
    sb_p = _sb_prompt(q_p, kb_p, vb_p, hd)
    zeros_state = jnp.zeros((bp, wrt, wrt), F32)
    ro_p, st_p = _retention(ret_p, jnp.arange(tp, dtype=I32), zeros_state, ret_heads, hd, _row_tile(tp, 512))
    mk_p, mv_p = _norm_proj(mem_prompt.reshape(bp * nmem, d), g_mem[l], w_mem_kv[l].astype(BF16),
                            [(0, wmm, 1.0, F32, 0), (wmm, 2 * wmm, 1.0, F32, 0)], _row_tile(bp * nmem, 512))
    mk_p = mk_p.reshape(bp, nmem, wmm)
    mv_p = mv_p.reshape(bp, nmem, wmm)
    x1_p = _merge(x_prompt, sb_p, ro_p, ret_p, qm_p, mk_p, mv_p, g_sb_out[l], g_ret_out[l], g_mem_out[l],
                  wo_bf, hd, _row_tile(tp, 1024))

    q_s, k_s, v_s, kb_s, vb_s, ret_s, qm_s = _stream(x_sample, w_in_bf, g_mix[l], dims)
    sb_s = _sb_sample(q_s, kb_s, vb_s, cache_sb_k[l].reshape(bs, past, wsb), cache_sb_v[l].reshape(bs, past, wsb), hd)
    s0 = _block_diag_state(state_ret[l].astype(F32), ret_heads, hd)
    ro_s, st_s = _retention(ret_s, past + jnp.arange(ts, dtype=I32), s0, ret_heads, hd, _row_tile(ts, 512))
    x1_s = _merge(x_sample, sb_s, ro_s, ret_s, qm_s, cache_mem_k[l].reshape(bs, nmem, wmm),
                  cache_mem_v[l].reshape(bs, nmem, wmm), g_sb_out[l], g_ret_out[l], g_mem_out[l],
                  wo_bf, hd, _row_tile(ts, 256))

    wq_bf = w_peer_q[l].astype(BF16)
    sk_bf = peer_sub_keys[l].astype(BF16)
    u_packed = _pack_table(peer_u[l])
    v_packed = _pack_table(peer_v[l])
    y_p = _peer_and_final(x1_p.reshape(bp * tp, d), g_ffn[l], wq_bf, sk_bf, u_packed, v_packed, g_final)
    y_s = _peer_and_final(x1_s.reshape(bs * ts, d), g_ffn[l], wq_bf, sk_bf, u_packed, v_packed, g_final)

    return (y_p.reshape(bp, tp, d), y_s.reshape(bs, ts, d),
            k_p.reshape(1, bp, tp, sb_heads, hd), v_p.reshape(1, bp, tp, sb_heads, hd),
            _diag_blocks(st_p, ret_heads, hd)[None],
            mk_p.reshape(1, bp, nmem, mem_heads, hd), mv_p.reshape(1, bp, nmem, mem_heads, hd),
            k_s.reshape(1, bs, ts, sb_heads, hd), v_s.reshape(1, bs, ts, sb_heads, hd),
            _diag_blocks(st_s, ret_heads, hd)[None].astype(state_ret.dtype))
```
